```python
import math
import jax, jax.numpy as jnp
from jax import lax
import numpy as np

D_MODEL = 1024
BATCH = 2
SEQ = 8192
DEPTH = 4
DEC_BATCH = 32
DEC_SEQ = 8
PAST_LEN = 8192
PAGE_SIZE = 128

N_AB_LAYERS = (DEPTH + 1) // 2
N_C_LAYERS = DEPTH // 2
CHUNK = 128
W_A = D_MODEL // 2
G_A = 4
GA_CH = W_A // G_A
W_B = D_MODEL // 2
CONV_W = 3
WINDOWS = (128, 512, 2048)
DILATIONS = (1, 4, 16)
N_GROUPS = 3
H_G = 8
HEAD_DIM = 64
C_WIDTH = H_G * HEAD_DIM
QKV_COLS = N_GROUPS * C_WIDTH
D_FF = 4 * D_MODEL
ALPHA = (2.0 * DEPTH) ** 0.25
BETA = (8.0 * DEPTH) ** -0.25
LN_EPS = 1e-5
NEG_INF = -1e30

kernel_name = 'hybrid_gmlp_shortconv_dilated_attn_step'


def layer_norm(x, g, b):
    x32 = x.astype(jnp.float32)
    mu = jnp.mean(x32, -1, keepdims=True)
    var = jnp.mean(jnp.square(x32 - mu), -1, keepdims=True)
    y = (x32 - mu) * lax.rsqrt(var + LN_EPS)
    return (y * g.astype(jnp.float32) + b.astype(jnp.float32)).astype(x.dtype)


def alibi_slopes():
    return 2.0 ** (-(8.0 / H_G) * jnp.arange(1, H_G + 1, dtype=jnp.float32))


def chunk_spatial_mix(v, w_s, b_s):
    B, L, _ = v.shape
    nc = -(-L // CHUNK)
    Lp = nc * CHUNK
    vp = jnp.pad(v, ((0, 0), (0, Lp - L), (0, 0))).reshape(B, nc, CHUNK, G_A, GA_CH)
    causal = jnp.tril(jnp.ones((CHUNK, CHUNK), dtype=bool))
    w = jnp.where(causal[None], w_s, 0.0).astype(v.dtype)
    y = jnp.einsum('gij,bnjgc->bnigc', w, vp) + b_s.T.astype(v.dtype)[None, None, :, :, None]
    return y.reshape(B, Lp, W_A)[:, :L]


def causal_conv(buf, xin, w):
    L = xin.shape[1]
    xc = jnp.concatenate([buf, xin], axis=1)
    y = sum(xc[:, k:k + L] * w[k] for k in range(CONV_W))
    return y, xc[:, -(CONV_W - 1):]


def ab_mixer(x, conv_buf, w_in, ln_v_g, ln_v_b, w_s, b_s, conv_w, w_out):
    z = jnp.einsum('bld,de->ble', x, w_in)
    z_u, z_v, z_bg, z_cg, z_h = jnp.split(
        z, [W_A, 2 * W_A, 2 * W_A + W_B, 2 * W_A + 2 * W_B], axis=-1)
    u = jax.nn.gelu(z_u)
    v = layer_norm(jax.nn.gelu(z_v), ln_v_g, ln_v_b)
    a_out = u * chunk_spatial_mix(v, w_s, b_s)
    conv_out, new_buf = causal_conv(conv_buf, z_cg * z_h, conv_w)
    b_out = z_bg * conv_out
    out = jnp.einsum('ble,ed->bld', jnp.concatenate([a_out, b_out], axis=-1), w_out)
    return out, new_buf, v


def split_qkv(x, w_qkv):
    B, L, _ = x.shape
    qkv = jnp.einsum('bld,de->ble', x, w_qkv).reshape(B, L, 3, N_GROUPS, H_G, HEAD_DIM)
    return qkv[:, :, 0], qkv[:, :, 1], qkv[:, :, 2]


def dilated_group_prompt(q, k, v, d, n_back, slopes):
    B, S, H, dh = q.shape
    M = S // d
    nb = -(-M // CHUNK)
    Mp = nb * CHUNK

    def by_residue(a):
        a = a.reshape(B, M, d, H, dh).transpose(0, 2, 1, 3, 4)
        return jnp.pad(a, ((0, 0), (0, 0), (0, Mp - M), (0, 0), (0, 0)))

    def key_blocks(a):
        a = jnp.pad(by_residue(a), ((0, 0), (0, 0), (CHUNK, 0), (0, 0), (0, 0)))
        a = a.reshape(B, d, nb + 1, CHUNK, H, dh)
        return jnp.concatenate([a[:, :, :-1], a[:, :, 1:]], axis=3)

    qb = by_residue(q).reshape(B, d, nb, CHUNK, H, dh)
    kb, vb = key_blocks(k), key_blocks(v)
    logits = jnp.einsum('brnqhc,brnkhc->brnhqk', qb, kb).astype(jnp.float32) * (dh ** -0.5)
    qi = jnp.arange(CHUNK)[:, None]
    kj = jnp.arange(2 * CHUNK)[None, :]
    steps = CHUNK + qi - kj
    key_m = jnp.arange(nb)[:, None] * CHUNK + jnp.arange(2 * CHUNK)[None, :] - CHUNK
    valid = ((steps >= 0) & (steps <= n_back))[None] & (key_m >= 0)[:, None, :]
    bias = -slopes[:, None, None] * (steps * d).astype(jnp.float32)[None]
    logits = jnp.where(valid[:, None], logits + bias, NEG_INF)
    lse = jax.nn.logsumexp(logits, axis=-1)
    p = jnp.exp(logits - lse[..., None]).astype(v.dtype)
    o = jnp.einsum('brnhqk,brnkhc->brnqhc', p, vb)
    o = o.reshape(B, d, Mp, H, dh)[:, :, :M].transpose(0, 2, 1, 3, 4).reshape(B, S, H, dh)
    lse = lse.transpose(0, 1, 2, 4, 3).reshape(B, d, Mp, H)[:, :, :M]
    lse = lse.transpose(0, 2, 1, 3).reshape(B, S, H)
    return o, lse


def dilated_group_sample(q, k, v, k_buf, v_buf, d, n_back, slopes):
    T, dh = q.shape[1], q.shape[-1]
    L = k_buf.shape[1]
    kc = jnp.concatenate([k_buf, k], axis=1)
    vc = jnp.concatenate([v_buf, v], axis=1)
    steps = jnp.arange(n_back + 1)[None, :]
    idx = L + jnp.arange(T)[:, None] - steps * d
    valid = idx >= 0
    idx = jnp.maximum(idx, 0)
    kg = jnp.take(kc, idx, axis=1)
    vg = jnp.take(vc, idx, axis=1)
    logits = jnp.einsum('bthc,btkhc->bhtk', q, kg).astype(jnp.float32) * (dh ** -0.5)
    bias = -slopes[:, None, None] * (steps * d).astype(jnp.float32)[None]
    logits = jnp.where(valid[None, None], logits + bias, NEG_INF)
    lse = jax.nn.logsumexp(logits, axis=-1)
    p = jnp.exp(logits - lse[..., None]).astype(v.dtype)
    o = jnp.einsum('bhtk,btkhc->bthc', p, vg)
    return o, lse.transpose(0, 2, 1), kc[:, -L:], vc[:, -L:]


def merge_groups(outs, lses, w_out):
    alpha = jax.nn.softmax(jnp.stack(lses, axis=0), axis=0)
    o = sum(alpha[g][..., None].astype(outs[g].dtype) * outs[g] for g in range(N_GROUPS))
    B, L = o.shape[:2]
    return jnp.einsum('ble,ed->bld', o.reshape(B, L, C_WIDTH), w_out)


def c_mixer_prompt(x, w_qkv, w_out):
    q, k, v = split_qkv(x, w_qkv)
    slopes = alibi_slopes()
    S = x.shape[1]
    outs, lses, new_kv = [], [], []
    for g in range(N_GROUPS):
        d = DILATIONS[g]
        o, l = dilated_group_prompt(q[:, :, g], k[:, :, g], v[:, :, g], d, WINDOWS[g] // d, slopes)
        outs.append(o)
        lses.append(l)
        n_keep = min(WINDOWS[g], S)
        new_kv.append(jnp.stack([k[:, -n_keep:, g], v[:, -n_keep:, g]], axis=2))
    return merge_groups(outs, lses, w_out), new_kv


def c_mixer_sample(x, bufs, w_qkv, w_out):
    q, k, v = split_qkv(x, w_qkv)
    slopes = alibi_slopes()
    outs, lses, new_kv = [], [], []
    for g in range(N_GROUPS):
        d = DILATIONS[g]
        o, l, kb, vb = dilated_group_sample(q[:, :, g], k[:, :, g], v[:, :, g],
                                            bufs[g][:, :, 0], bufs[g][:, :, 1],
                                            d, WINDOWS[g] // d, slopes)
        outs.append(o)
        lses.append(l)
        new_kv.append(jnp.stack([kb, vb], axis=2))
    return merge_groups(outs, lses, w_out), new_kv


def sq_relu_mlp(x, w_up, w_down):
    h = jnp.square(jax.nn.relu(jnp.einsum('bld,df->blf', x, w_up)))
    return jnp.einsum('blf,fd->bld', h, w_down)


def setup_inputs(seed: int = 0) -> dict:
    key = jax.random.key(seed)
    ks = jax.random.split(key, 24)
    f32 = jnp.float32

    def nrm(k, shape, scale):
        return jax.random.normal(k, shape, f32) * scale

    win_lens = [min(w, PAST_LEN) for w in WINDOWS]
    col_scale = jnp.concatenate([jnp.ones((2 * QKV_COLS,), f32), jnp.full((QKV_COLS,), BETA, f32)])
    return {
        'x_prompt': nrm(ks[0], (BATCH, SEQ, D_MODEL), 1.0),
        'x_sample': nrm(ks[1], (DEC_BATCH, DEC_SEQ, D_MODEL), 1.0),
        'state_conv': nrm(ks[2], (N_AB_LAYERS, DEC_BATCH, CONV_W - 1, W_B), 1.0),
        'cache_kv_w128': nrm(ks[3], (N_C_LAYERS, DEC_BATCH, win_lens[0], 2, H_G, HEAD_DIM), 1.0),
        'cache_kv_w512': nrm(ks[4], (N_C_LAYERS, DEC_BATCH, win_lens[1], 2, H_G, HEAD_DIM), 1.0),
        'cache_kv_w2048': nrm(ks[5], (N_C_LAYERS, DEC_BATCH, win_lens[2], 2, H_G, HEAD_DIM), 1.0),
        'w_in_ab': nrm(ks[6], (N_AB_LAYERS, D_MODEL, 2 * W_A + 3 * W_B), D_MODEL ** -0.5),
        'ln_v_g': 1.0 + nrm(ks[7], (N_AB_LAYERS, W_A), 0.05),
        'ln_v_b': nrm(ks[8], (N_AB_LAYERS, W_A), 0.05),
        'w_spatial': nrm(ks[9], (N_AB_LAYERS, G_A, CHUNK, CHUNK), CHUNK ** -0.5),
        'b_spatial': 1.0 + nrm(ks[10], (N_AB_LAYERS, G_A, CHUNK), 0.1),
        'conv_w': nrm(ks[11], (N_AB_LAYERS, CONV_W, W_B), CONV_W ** -0.5),
        'w_out_ab': nrm(ks[12], (N_AB_LAYERS, W_A + W_B, D_MODEL), BETA * (W_A + W_B) ** -0.5),
        'w_qkv_c': nrm(ks[13], (N_C_LAYERS, D_MODEL, 3 * QKV_COLS), D_MODEL ** -0.5) * col_scale,
        'w_out_c': nrm(ks[14], (N_C_LAYERS, C_WIDTH, D_MODEL), BETA * C_WIDTH ** -0.5),
        'ln1_g': 1.0 + nrm(ks[15], (DEPTH, D_MODEL), 0.05),
        'ln1_b': nrm(ks[16], (DEPTH, D_MODEL), 0.05),
        'ln2_g': 1.0 + nrm(ks[17], (DEPTH, D_MODEL), 0.05),
        'ln2_b': nrm(ks[18], (DEPTH, D_MODEL), 0.05),
        'w_mlp_up': nrm(ks[19], (DEPTH, D_MODEL, D_FF), BETA * D_MODEL ** -0.5),
        'w_mlp_down': nrm(ks[20], (DEPTH, D_FF, D_MODEL), BETA * D_FF ** -0.5),
    }


def reference(x_prompt, x_sample, state_conv, cache_kv_w128, cache_kv_w512, cache_kv_w2048,
              w_in_ab, ln_v_g, ln_v_b, w_spatial, b_spatial, conv_w, w_out_ab,
              w_qkv_c, w_out_c, ln1_g, ln1_b, ln2_g, ln2_b, w_mlp_up, w_mlp_down):
    caches = (cache_kv_w128, cache_kv_w512, cache_kv_w2048)
    xp, xs = x_prompt, x_sample
    conv_p, conv_s, chunk_v_s = [], [], []
    kv_p = [[] for _ in range(N_GROUPS)]
    kv_s = [[] for _ in range(N_GROUPS)]
    for layer in range(DEPTH):
        i = layer // 2
        if layer % 2 == 0:
            params = (w_in_ab[i], ln_v_g[i], ln_v_b[i], w_spatial[i], b_spatial[i], conv_w[i], w_out_ab[i])
            zero_buf = jnp.zeros((xp.shape[0], CONV_W - 1, W_B), xp.dtype)
            mp, buf_p, _ = ab_mixer(xp, zero_buf, *params)
            ms, buf_s, v_s = ab_mixer(xs, state_conv[i], *params)
            conv_p.append(buf_p)
            conv_s.append(buf_s)
            chunk_v_s.append(v_s)
        else:
            mp, new_p = c_mixer_prompt(xp, w_qkv_c[i], w_out_c[i])
            ms, new_s = c_mixer_sample(xs, [c[i] for c in caches], w_qkv_c[i], w_out_c[i])
            for g in range(N_GROUPS):
                kv_p[g].append(new_p[g])
                kv_s[g].append(new_s[g])
        xp = layer_norm(ALPHA * xp + mp, ln1_g[layer], ln1_b[layer])
        xs = layer_norm(ALPHA * xs + ms, ln1_g[layer], ln1_b[layer])
        xp = layer_norm(ALPHA * xp + sq_relu_mlp(xp, w_mlp_up[layer], w_mlp_down[layer]), ln2_g[layer], ln2_b[layer])
        xs = layer_norm(ALPHA * xs + sq_relu_mlp(xs, w_mlp_up[layer], w_mlp_down[layer]), ln2_g[layer], ln2_b[layer])
    return (xp, xs,
            jnp.stack(conv_p), jnp.stack(conv_s), jnp.stack(chunk_v_s),
            jnp.stack(kv_p[0]), jnp.stack(kv_p[1]), jnp.stack(kv_p[2]),
            jnp.stack(kv_s[0]), jnp.stack(kv_s[1]), jnp.stack(kv_s[2]))
```

```python
import functools
import math

import jax
import jax.numpy as jnp
from jax import lax
from jax.experimental import pallas as pl
from jax.experimental.pallas import tpu as pltpu

F32 = jnp.float32
BF16 = jnp.bfloat16

D_MODEL = 1024
DEPTH = 4
CHUNK = 128
W_A = 512
G_A = 4
W_B = 512
N_GROUPS = 3
WINDOWS = (128, 512, 2048)
DILATIONS = (1, 4, 16)
N_BACK = 128
H_G = 8
HEAD_DIM = 64
C_WIDTH = H_G * HEAD_DIM
QKV_COLS = N_GROUPS * C_WIDTH
QKV_BLOCKS = 3 * N_GROUPS
D_FF = 4 * D_MODEL
ALPHA = (2.0 * DEPTH) ** 0.25
LN_EPS = 1e-5
NEG_INF = -1e30
SLOPES = tuple(2.0 ** (-(8.0 / H_G) * j) for j in range(1, H_G + 1))
QK_SCALE = HEAD_DIM ** -0.5

VMEM_LIMIT_BYTES = 52 * 1024 * 1024
ROW_TILE = 512
ATTN_TQ = 512


def _layer_norm(x, g, b):
    mu = jnp.mean(x, axis=-1, keepdims=True)
    xc = x - mu
    var = jnp.mean(xc * xc, axis=-1, keepdims=True)
    return xc * lax.rsqrt(var + LN_EPS) * g + b


def _gelu(x):
    c = math.sqrt(2.0 / math.pi)
    return x * (0.5 * (1.0 + jnp.tanh(c * (x + 0.044715 * (x * x * x)))))


def _const_spec(shape):
    nd = len(shape)
    return pl.BlockSpec(shape, lambda *_: (0,) * nd, pipeline_mode=pl.Buffered(1))


def _params(n_grid):
    return pltpu.CompilerParams(dimension_semantics=("arbitrary",) * n_grid,
                                vmem_limit_bytes=VMEM_LIMIT_BYTES)


def _ab_kernel(*refs, tm, chunk, sample):
    if sample:
        (x_ref, s1_ref, s2_ref, w_in_ref, lvg_ref, lvb_ref, wmix_ref, bmix_ref, cw_ref, w_out_ref,
         g1_ref, b1_ref, o_ref, hc_ref, v_ref, y_scr) = refs
    else:
        (x_ref, w_in_ref, lvg_ref, lvb_ref, wmix_ref, bmix_ref, cw_ref, w_out_ref,
         g1_ref, b1_ref, o_ref, cb_ref, y_scr, h_scr) = refs

    x = x_ref[...]
    xb = x.astype(BF16)

    def proj(k):
        return jnp.dot(xb, w_in_ref[:, k * W_A:(k + 1) * W_A], preferred_element_type=F32)

    u = _gelu(proj(0))
    v = _layer_norm(_gelu(proj(1)), lvg_ref[...], lvb_ref[...])
    if sample:
        v_ref[...] = v
    vb = v.astype(BF16)
    for c in range(tm // chunk):
        rows = slice(c * chunk, (c + 1) * chunk)
        for g in range(G_A):
            cols = slice(g * 128, (g + 1) * 128)
            y_scr[rows, cols] = (jnp.dot(wmix_ref[g], vb[rows, cols], preferred_element_type=F32)
                                 + bmix_ref[:, cols])
    a_out = (u * y_scr[...]).astype(BF16)

    hc = proj(3) * proj(4)
    if sample:
        hc_ref[...] = hc
        pos = lax.broadcasted_iota(jnp.int32, hc.shape, 0) & 7
        sh1 = jnp.where(pos == 0, s1_ref[...], pltpu.roll(hc, 1, 0))
        sh2 = jnp.where(pos < 2, s2_ref[...], pltpu.roll(hc, 2, 0))
    else:
        @pl.when(pl.program_id(1) == 0)
        def _():
            h_scr[0:8, :] = jnp.zeros((8, W_B), F32)
        h_scr[8:tm + 8, :] = hc
        sh1 = h_scr[7:tm + 7, :]
        sh2 = h_scr[6:tm + 6, :]
        cb_ref[...] = h_scr[tm + 6:tm + 8, :]
        h_scr[0:8, :] = h_scr[tm:tm + 8, :]
    conv = cw_ref[0:1, :] * sh2 + cw_ref[1:2, :] * sh1 + cw_ref[2:3, :] * hc
    b_out = (proj(2) * conv).astype(BF16)

    mix = (jnp.dot(a_out, w_out_ref[0:W_A, :], preferred_element_type=F32)
           + jnp.dot(b_out, w_out_ref[W_A:W_A + W_B, :], preferred_element_type=F32))
    o_ref[...] = _layer_norm(ALPHA * x + mix, g1_ref[...], b1_ref[...])


def _ab_weight_specs(chunk):
    return [_const_spec((D_MODEL, 2 * W_A + 3 * W_B)), _const_spec((1, W_A)), _const_spec((1, W_A)),
            _const_spec((G_A, chunk, chunk)), _const_spec((chunk, W_A)), _const_spec((3, W_B)),
            _const_spec((W_A + W_B, D_MODEL)), _const_spec((1, D_MODEL)), _const_spec((1, D_MODEL))]


def _ab_layer_prompt(x, batch, seq, weights):
    tm = ROW_TILE
    nt = seq // tm
    row_spec = pl.BlockSpec((tm, D_MODEL), lambda b, t: (b * nt + t, 0))
    return pl.pallas_call(
        functools.partial(_ab_kernel, tm=tm, chunk=CHUNK, sample=False),
        grid=(batch, nt),
        in_specs=[row_spec] + _ab_weight_specs(CHUNK),
        out_specs=[row_spec, pl.BlockSpec((None, 2, W_B), lambda b, t: (b, 0, 0))],
        out_shape=[jax.ShapeDtypeStruct((batch * seq, D_MODEL), F32),
                   jax.ShapeDtypeStruct((batch, 2, W_B), F32)],
        scratch_shapes=[pltpu.VMEM((tm, W_A), F32), pltpu.VMEM((tm + 8, W_B), F32)],
        compiler_params=_params(2),
        name="ab_layer_prompt",
    )(x, *weights)


def _ab_layer_sample(x, s1, s2, weights):
    tm = x.shape[0]
    full = lambda cols: pl.BlockSpec((tm, cols), lambda i: (0, 0))
    return pl.pallas_call(
        functools.partial(_ab_kernel, tm=tm, chunk=tm, sample=True),
        grid=(1,),
        in_specs=[full(D_MODEL), full(W_B), full(W_B)] + _ab_weight_specs(tm),
        out_specs=[full(D_MODEL), full(W_B), full(W_A)],
        out_shape=[jax.ShapeDtypeStruct((tm, D_MODEL), F32),
                   jax.ShapeDtypeStruct((tm, W_B), F32),
                   jax.ShapeDtypeStruct((tm, W_A), F32)],
        scratch_shapes=[pltpu.VMEM((tm, W_A), F32)],
        compiler_params=_params(1),
        name="ab_layer_sample",
    )(x, s1, s2, *weights)


def _mlp_kernel(x_ref, w_up_ref, w_down_ref, g_ref, b_ref, o_ref):
    x = x_ref[...]
    xb = x.astype(BF16)
    acc = None
    for c in range(D_FF // D_MODEL):
        cols = slice(c * D_MODEL, (c + 1) * D_MODEL)
        h = jnp.maximum(jnp.dot(xb, w_up_ref[:, cols], preferred_element_type=F32), 0.0)
        part = jnp.dot((h * h).astype(BF16), w_down_ref[cols, :], preferred_element_type=F32)
        acc = part if acc is None else acc + part
    o_ref[...] = _layer_norm(ALPHA * x + acc, g_ref[...], b_ref[...])


def _mlp_layer(x, w_up, w_down, g, b):
    rows = x.shape[0]
    tm = min(ROW_TILE, rows)
    row_spec = pl.BlockSpec((tm, D_MODEL), lambda t: (t, 0))
    return pl.pallas_call(
        _mlp_kernel,
        grid=(rows // tm,),
        in_specs=[row_spec, _const_spec((D_MODEL, D_FF)), _const_spec((D_FF, D_MODEL)),
                  _const_spec((1, D_MODEL)), _const_spec((1, D_MODEL))],
        out_specs=row_spec,
        out_shape=jax.ShapeDtypeStruct((rows, D_MODEL), F32),
        compiler_params=_params(1),
        name="mlp_layer",
    )(x, w_up, w_down, g, b)


def _qkv_kernel(x_ref, w_ref, o_ref):
    xb = x_ref[...].astype(BF16)
    for c in range(QKV_BLOCKS):
        cols = slice(c * C_WIDTH, (c + 1) * C_WIDTH)
        o_ref[:, cols] = jnp.dot(xb, w_ref[:, cols], preferred_element_type=F32)


def _qkv_proj(x, w):
    rows = x.shape[0]
    tm = min(256, rows)
    return pl.pallas_call(
        _qkv_kernel,
        grid=(rows // tm,),
        in_specs=[pl.BlockSpec((tm, D_MODEL), lambda t: (t, 0)), _const_spec((D_MODEL, 3 * QKV_COLS))],
        out_specs=pl.BlockSpec((tm, 3 * QKV_COLS), lambda t: (t, 0)),
        out_shape=jax.ShapeDtypeStruct((rows, 3 * QKV_COLS), F32),
        compiler_params=_params(1),
        name="qkv_proj",
    )(x, w)


def _attn_kernel(q_ref, kc_ref, kp_ref, vc_ref, vp_ref, o_ref, lse_ref,
                 q_scr, k_scr, v_scr, bias_scr, *, tq, dil):
    first_step = (pl.program_id(0) == 0) & (pl.program_id(1) == 0) & (pl.program_id(2) == 0)

    @pl.when(first_step)
    def _():
        qi = lax.broadcasted_iota(jnp.int32, (CHUNK, 2 * CHUNK), 0)
        kj = lax.broadcasted_iota(jnp.int32, (CHUNK, 2 * CHUNK), 1)
        steps = CHUNK + qi - kj
        dist = (steps * dil).astype(F32)
        for h in range(H_G):
            b0 = jnp.where(steps >= 0, jnp.where(steps <= N_BACK, -SLOPES[h] * dist, NEG_INF), NEG_INF)
            bias_scr[0, h] = b0
            bias_scr[1, h] = jnp.where(kj < CHUNK, NEG_INF, b0)

    q_scr[...] = (q_ref[...] * QK_SCALE).astype(BF16)
    k_scr[0:CHUNK, :] = kp_ref[...].astype(BF16)
    k_scr[CHUNK:CHUNK + tq, :] = kc_ref[...].astype(BF16)
    v_scr[0:CHUNK, :] = vp_ref[...].astype(BF16)
    v_scr[CHUNK:CHUNK + tq, :] = vc_ref[...].astype(BF16)
    at_start = pl.program_id(2) == 0

    def sub_block(s, carry):
        r0 = pl.multiple_of(s * CHUNK, CHUNK)
        bsel = jnp.where(at_start & (s == 0), 1, 0)
        for h in range(H_G):
            cols = slice(h * HEAD_DIM, (h + 1) * HEAD_DIM)
            qh = q_scr[pl.ds(r0, CHUNK), cols]
            kh = k_scr[pl.ds(r0, 2 * CHUNK), cols]
            vh = v_scr[pl.ds(r0, 2 * CHUNK), cols]
            logits = lax.dot_general(qh, kh, (((1,), (1,)), ((), ())), preferred_element_type=F32)
            logits = logits + bias_scr[bsel, h]
            m = jnp.max(logits, axis=-1, keepdims=True)
            p = jnp.exp(logits - m)
            l = jnp.sum(p, axis=-1, keepdims=True)
            o = jnp.dot(p.astype(BF16), vh, preferred_element_type=F32) / l
            o_ref[pl.ds(r0, CHUNK), cols] = o
            lse_ref[pl.ds(r0, CHUNK), cols] = jnp.broadcast_to(m + jnp.log(l), (CHUNK, HEAD_DIM))
        return carry

    lax.fori_loop(0, tq // CHUNK, sub_block, 0)


def _attn_prompt(qkv, batch, seq, g):
    d = DILATIONS[g]
    sub = seq // d
    tq = min(ATTN_TQ, sub)
    ni = sub // tq
    qkv3 = qkv.reshape(batch, sub, d * QKV_BLOCKS * C_WIDTH)
    blk = lambda rows, idx: pl.BlockSpec((None, rows, C_WIDTH), idx)
    prev = lambda i: jnp.maximum(i * (tq // CHUNK) - 1, 0)
    col = lambda r, sec: r * QKV_BLOCKS + sec * N_GROUPS + g
    out_spec = blk(tq, lambda b, r, i: (b, i, r))
    o, lse = pl.pallas_call(
        functools.partial(_attn_kernel, tq=tq, dil=d),
        grid=(batch, d, ni),
        in_specs=[blk(tq, lambda b, r, i: (b, i, col(r, 0))),
                  blk(tq, lambda b, r, i: (b, i, col(r, 1))),
                  blk(CHUNK, lambda b, r, i: (b, prev(i), col(r, 1))),
                  blk(tq, lambda b, r, i: (b, i, col(r, 2))),
                  blk(CHUNK, lambda b, r, i: (b, prev(i), col(r, 2)))],
        out_specs=[out_spec, out_spec],
        out_shape=[jax.ShapeDtypeStruct((batch, sub, d * C_WIDTH), F32)] * 2,
        scratch_shapes=[pltpu.VMEM((tq, C_WIDTH), BF16), pltpu.VMEM((CHUNK + tq, C_WIDTH), BF16),
                        pltpu.VMEM((CHUNK + tq, C_WIDTH), BF16),
                        pltpu.VMEM((2, H_G, CHUNK, 2 * CHUNK), F32)],
        compiler_params=_params(3),
        name=f"attn_prompt_d{d}",
    )(qkv3, qkv3, qkv3, qkv3, qkv3)
    return o.reshape(batch * seq, C_WIDTH), lse.reshape(batch * seq, C_WIDTH)


def _sattn_kernel(*refs, buf_len, dil, t_new):
    q_ref, kn_ref, vn_ref, cache_ref = refs[:4]
    o_ref, lse_ref, newc_ref, k_scr, v_scr, bias_scr = refs[-6:]
    n_keys = buf_len + CHUNK

    @pl.when(pl.program_id(0) == 0)
    def _():
        t = lax.broadcasted_iota(jnp.int32, (t_new, n_keys), 0)
        p = lax.broadcasted_iota(jnp.int32, (t_new, n_keys), 1)
        dist = buf_len + t - p
        on_stride = (dist & (dil - 1)) == 0
        distf = dist.astype(F32)
        for h in range(H_G):
            b0 = jnp.where(on_stride, -SLOPES[h] * distf, NEG_INF)
            b0 = jnp.where(dist >= 0, b0, NEG_INF)
            bias_scr[h] = jnp.where(dist <= N_BACK * dil, b0, NEG_INF)

    kn = kn_ref[...]
    vn = vn_ref[...]
    pad = jnp.zeros((CHUNK - t_new, C_WIDTH), F32)
    k_scr[0:buf_len, :] = cache_ref[:, 0:C_WIDTH].astype(BF16)
    k_scr[buf_len:n_keys, :] = jnp.concatenate([kn, pad], axis=0).astype(BF16)
    v_scr[0:buf_len, :] = cache_ref[:, C_WIDTH:2 * C_WIDTH].astype(BF16)
    v_scr[buf_len:n_keys, :] = jnp.concatenate([vn, pad], axis=0).astype(BF16)

    qb = (q_ref[...] * QK_SCALE).astype(BF16)
    outs, lses = [], []
    for h in range(H_G):
        cols = slice(h * HEAD_DIM, (h + 1) * HEAD_DIM)
        logits = lax.dot_general(qb[:, cols], k_scr[:, cols], (((1,), (1,)), ((), ())),
                                 preferred_element_type=F32) + bias_scr[h]
        m = jnp.max(logits, axis=-1, keepdims=True)
        p = jnp.exp(logits - m)
        l = jnp.sum(p, axis=-1, keepdims=True)
        outs.append(jnp.dot(p.astype(BF16), v_scr[:, cols], preferred_element_type=F32) / l)
        lses.append(jnp.broadcast_to(m + jnp.log(l), (t_new, HEAD_DIM)))
    o_ref[...] = jnp.concatenate(outs, axis=-1)
    lse_ref[...] = jnp.concatenate(lses, axis=-1)

    newc_ref[0:buf_len - t_new, :] = cache_ref[t_new:buf_len, :]
    newc_ref[buf_len - t_new:buf_len, :] = jnp.concatenate([kn, vn], axis=-1)


def _attn_sample(qkv, cache, prev_out, layer_idx, g, t_new):
    n_layers, n_seq, buf_len, width = cache.shape
    d = DILATIONS[g]
    row = lambda sec: pl.BlockSpec((t_new, C_WIDTH), lambda b: (b, sec * N_GROUPS + g))
    cache_spec = pl.BlockSpec((None, None, buf_len, width), lambda b: (layer_idx, b, 0, 0))
    out_row = pl.BlockSpec((t_new, C_WIDTH), lambda b: (b, 0))
    in_specs = [row(0), row(1), row(2), cache_spec]
    args = [qkv, qkv, qkv, cache]
    aliases = {}
    if prev_out is not None:
        in_specs.append(pl.BlockSpec(memory_space=pl.ANY))
        args.append(prev_out)
        aliases = {4: 2}
    return pl.pallas_call(
        functools.partial(_sattn_kernel, buf_len=buf_len, dil=d, t_new=t_new),
        grid=(n_seq,),
        in_specs=in_specs,
        out_specs=[out_row, out_row, cache_spec],
        out_shape=[jax.ShapeDtypeStruct((n_seq * t_new, C_WIDTH), F32)] * 2
                  + [jax.ShapeDtypeStruct(cache.shape, F32)],
        scratch_shapes=[pltpu.VMEM((buf_len + CHUNK, C_WIDTH), BF16),
                        pltpu.VMEM((buf_len + CHUNK, C_WIDTH), BF16),
                        pltpu.VMEM((H_G, t_new, buf_len + CHUNK), F32)],
        input_output_aliases=aliases,
        compiler_params=_params(1),
        name=f"attn_sample_d{d}",
    )(*args)


def _merge_kernel(x_ref, o0_ref, o1_ref, o2_ref, l0_ref, l1_ref, l2_ref, w_ref, g_ref, b_ref, out_ref):
    l0, l1, l2 = l0_ref[...], l1_ref[...], l2_ref[...]
    m = jnp.maximum(jnp.maximum(l0, l1), l2)
    e0, e1, e2 = jnp.exp(l0 - m), jnp.exp(l1 - m), jnp.exp(l2 - m)
    o = (e0 * o0_ref[...] + e1 * o1_ref[...] + e2 * o2_ref[...]) / (e0 + e1 + e2)
    x = x_ref[...]
    mix = jnp.dot(o.astype(BF16), w_ref[...], preferred_element_type=F32)
    out_ref[...] = _layer_norm(ALPHA * x + mix, g_ref[...], b_ref[...])


def _merge_layer(x, outs, lses, w_out, g, b):
    rows = x.shape[0]
    tm = min(ROW_TILE, rows)
    row_spec = lambda cols: pl.BlockSpec((tm, cols), lambda t: (t, 0))
    return pl.pallas_call(
        _merge_kernel,
        grid=(rows // tm,),
        in_specs=[row_spec(D_MODEL)] + [row_spec(C_WIDTH)] * 6
                 + [_const_spec((C_WIDTH, D_MODEL)), _const_spec((1, D_MODEL)), _const_spec((1, D_MODEL))],
        out_specs=row_spec(D_MODEL),
        out_shape=jax.ShapeDtypeStruct((rows, D_MODEL), F32),
        compiler_params=_params(1),
        name="merge_layer",
    )(x, *outs, *lses, w_out, g, b)


def kernel(x_prompt, x_sample, state_conv, cache_kv_w128, cache_kv_w512, cache_kv_w2048, w_in_ab, ln_v_g, ln_v_b, w_spatial, b_spatial, conv_w, w_out_ab, w_qkv_c, w_out_c, ln1_g, ln1_b, ln2_g, ln2_b, w_mlp_up, w_mlp_down):
    batch, seq, _ = x_prompt.shape
    n_seq, t_new, _ = x_sample.shape
    n_tok_s = n_seq * t_new
    xp = x_prompt.reshape(batch * seq, D_MODEL)
    xs = x_sample.reshape(n_tok_s, D_MODEL)
    caches = [c.reshape(c.shape[0], n_seq, c.shape[2], 2 * C_WIDTH)
              for c in (cache_kv_w128, cache_kv_w512, cache_kv_w2048)]
    new_caches = [None] * N_GROUPS
    row = lambda a: a.reshape(1, -1)

    causal = jnp.tril(jnp.ones((CHUNK, CHUNK), dtype=bool))
    conv_p, conv_s, chunk_v_s = [], [], []
    kv_p = [[] for _ in range(N_GROUPS)]

    for layer in range(DEPTH):
        i = layer // 2
        if layer % 2 == 0:
            w_tril = jnp.where(causal[None], w_spatial[i], 0.0)
            bmix = jnp.repeat(b_spatial[i].T, CHUNK, axis=1)
            shared = (w_in_ab[i].astype(BF16), row(ln_v_g[i]), row(ln_v_b[i]))
            tail = (conv_w[i], w_out_ab[i].astype(BF16), row(ln1_g[layer]), row(ln1_b[layer]))
            xp, buf_p = _ab_layer_prompt(xp, batch, seq,
                                         shared + (w_tril.astype(BF16), bmix) + tail)
            eye = jnp.eye(n_seq, dtype=F32)
            w_blk = jnp.stack([jnp.kron(eye, w_tril[g, :t_new, :t_new]) for g in range(G_A)])
            bmix_s = jnp.tile(bmix[:t_new], (n_seq, 1))
            st = state_conv[i]
            s1 = jnp.concatenate([st[:, 1:2], jnp.zeros((n_seq, t_new - 1, W_B), F32)], axis=1)
            s2 = jnp.concatenate([st, jnp.zeros((n_seq, t_new - 2, W_B), F32)], axis=1)
            xs, hc_s, v_s = _ab_layer_sample(xs, s1.reshape(n_tok_s, W_B), s2.reshape(n_tok_s, W_B),
                                             shared + (w_blk.astype(BF16), bmix_s) + tail)
            conv_p.append(buf_p)
            conv_s.append(hc_s.reshape(n_seq, t_new, W_B)[:, t_new - 2:])
            chunk_v_s.append(v_s.reshape(n_seq, t_new, W_A))
        else:
            wq = w_qkv_c[i].astype(BF16)
            wo = w_out_c[i].astype(BF16)
            g1, b1 = row(ln1_g[layer]), row(ln1_b[layer])
            qkv_p = _qkv_proj(xp, wq)
            outs, lses = zip(*[_attn_prompt(qkv_p, batch, seq, g) for g in range(N_GROUPS)])
            xp = _merge_layer(xp, outs, lses, wo, g1, b1)
            qkv_p3 = qkv_p.reshape(batch, seq, 3, N_GROUPS, H_G, HEAD_DIM)
            for g in range(N_GROUPS):
                n_keep = min(WINDOWS[g], seq)
                kv_p[g].append(jnp.stack([qkv_p3[:, seq - n_keep:, 1, g], qkv_p3[:, seq - n_keep:, 2, g]], axis=2))
            qkv_s = _qkv_proj(xs, wq)
            outs, lses = [], []
            for g in range(N_GROUPS):
                o, lse, new_caches[g] = _attn_sample(qkv_s, caches[g], new_caches[g], i, g, t_new)
                outs.append(o)
                lses.append(lse)
            xs = _merge_layer(xs, outs, lses, wo, g1, b1)
        w_up, w_down = w_mlp_up[layer].astype(BF16), w_mlp_down[layer].astype(BF16)
        g2, b2 = row(ln2_g[layer]), row(ln2_b[layer])
        xp = _mlp_layer(xp, w_up, w_down, g2, b2)
        xs = _mlp_layer(xs, w_up, w_down, g2, b2)

    kv_s = [nc.reshape(nc.shape[0], n_seq, nc.shape[2], 2, H_G, HEAD_DIM) for nc in new_caches]
    return (xp.reshape(batch, seq, D_MODEL), xs.reshape(n_seq, t_new, D_MODEL),
            jnp.stack(conv_p), jnp.stack(conv_s), jnp.stack(chunk_v_s),
            jnp.stack(kv_p[0]), jnp.stack(kv_p[1]), jnp.stack(kv_p[2]),
            kv_s[0], kv_s[1], kv_s[2])
```

```python
import functools
import math

import jax
import jax.numpy as jnp
from jax import lax
from jax.experimental import pallas as pl
from jax.experimental.pallas import tpu as pltpu

F32 = jnp.float32
BF16 = jnp.bfloat16

D_MODEL = 1024
DEPTH = 4
CHUNK = 128
W_A = 512
G_A = 4
W_B = 512
N_GROUPS = 3
WINDOWS = (128, 512, 2048)
DILATIONS = (1, 4, 16)
N_BACK = 128
H_G = 8
HEAD_DIM = 64
C_WIDTH = H_G * HEAD_DIM
QKV_COLS = N_GROUPS * C_WIDTH
QKV_BLOCKS = 3 * N_GROUPS
D_FF = 4 * D_MODEL
ALPHA = (2.0 * DEPTH) ** 0.25
LN_EPS = 1e-5
NEG_INF = -1e30
SLOPES = tuple(2.0 ** (-(8.0 / H_G) * j) for j in range(1, H_G + 1))
QK_SCALE = HEAD_DIM ** -0.5

VMEM_LIMIT_BYTES = 52 * 1024 * 1024
ROW_TILE = 512
PERM_TILE = CHUNK * max(DILATIONS)
X_SLABS = D_MODEL // 128
MERGE_TILE = 256


def _layer_norm(x, g, b):
    mu = jnp.mean(x, axis=-1, keepdims=True)
    xc = x - mu
    var = jnp.mean(xc * xc, axis=-1, keepdims=True)
    return xc * lax.rsqrt(var + LN_EPS) * g + b


def _gelu(x):
    c = math.sqrt(2.0 / math.pi)
    return x * (0.5 * (1.0 + jnp.tanh(c * (x + 0.044715 * (x * x * x)))))


def _const_spec(shape):
    nd = len(shape)
    return pl.BlockSpec(shape, lambda *_: (0,) * nd, pipeline_mode=pl.Buffered(1))


def _params(n_grid):
    return pltpu.CompilerParams(dimension_semantics=("arbitrary",) * n_grid,
                                vmem_limit_bytes=VMEM_LIMIT_BYTES)


def _ab_kernel(*refs, tm, chunk, sample):
    if sample:
        (x_ref, s1_ref, s2_ref, w_in_ref, lvg_ref, lvb_ref, wmix_ref, bmix_ref, cw_ref, w_out_ref,
         g1_ref, b1_ref, o_ref, hc_ref, v_ref, y_scr) = refs
    else:
        (x_ref, w_in_ref, lvg_ref, lvb_ref, wmix_ref, bmix_ref, cw_ref, w_out_ref,
         g1_ref, b1_ref, o_ref, cb_ref, y_scr, h_scr) = refs

    x = x_ref[...]
    xb = x.astype(BF16)

    def proj(k):
        return jnp.dot(xb, w_in_ref[:, k * W_A:(k + 1) * W_A], preferred_element_type=F32)

    u = _gelu(proj(0))
    v = _layer_norm(_gelu(proj(1)), lvg_ref[...], lvb_ref[...])
    if sample:
        v_ref[...] = v
    vb = v.astype(BF16)
    for c in range(tm // chunk):
        rows = slice(c * chunk, (c + 1) * chunk)
        for g in range(G_A):
            cols = slice(g * 128, (g + 1) * 128)
            y_scr[rows, cols] = (jnp.dot(wmix_ref[g], vb[rows, cols], preferred_element_type=F32)
                                 + bmix_ref[:, cols])
    a_out = (u * y_scr[...]).astype(BF16)

    hc = proj(3) * proj(4)
    if sample:
        hc_ref[...] = hc
        pos = lax.broadcasted_iota(jnp.int32, hc.shape, 0) & 7
        sh1 = jnp.where(pos == 0, s1_ref[...], pltpu.roll(hc, 1, 0))
        sh2 = jnp.where(pos < 2, s2_ref[...], pltpu.roll(hc, 2, 0))
    else:
        @pl.when(pl.program_id(1) == 0)
        def _():
            h_scr[0:8, :] = jnp.zeros((8, W_B), F32)
        h_scr[8:tm + 8, :] = hc
        sh1 = h_scr[7:tm + 7, :]
        sh2 = h_scr[6:tm + 6, :]
        cb_ref[...] = h_scr[tm + 6:tm + 8, :]
        h_scr[0:8, :] = h_scr[tm:tm + 8, :]
    conv = cw_ref[0:1, :] * sh2 + cw_ref[1:2, :] * sh1 + cw_ref[2:3, :] * hc
    b_out = (proj(2) * conv).astype(BF16)

    mix = (jnp.dot(a_out, w_out_ref[0:W_A, :], preferred_element_type=F32)
           + jnp.dot(b_out, w_out_ref[W_A:W_A + W_B, :], preferred_element_type=F32))
    o_ref[...] = _layer_norm(ALPHA * x + mix, g1_ref[...], b1_ref[...])


def _ab_weight_specs(chunk):
    return [_const_spec((D_MODEL, 2 * W_A + 3 * W_B)), _const_spec((1, W_A)), _const_spec((1, W_A)),
            _const_spec((G_A, chunk, chunk)), _const_spec((chunk, W_A)), _const_spec((3, W_B)),
            _const_spec((W_A + W_B, D_MODEL)), _const_spec((1, D_MODEL)), _const_spec((1, D_MODEL))]


def _ab_layer_prompt(x, batch, seq, weights):
    tm = ROW_TILE
    nt = seq // tm
    row_spec = pl.BlockSpec((tm, D_MODEL), lambda b, t: (b * nt + t, 0))
    return pl.pallas_call(
        functools.partial(_ab_kernel, tm=tm, chunk=CHUNK, sample=False),
        grid=(batch, nt),
        in_specs=[row_spec] + _ab_weight_specs(CHUNK),
        out_specs=[row_spec, pl.BlockSpec((None, 2, W_B), lambda b, t: (b, 0, 0))],
        out_shape=[jax.ShapeDtypeStruct((batch * seq, D_MODEL), F32),
                   jax.ShapeDtypeStruct((batch, 2, W_B), F32)],
        scratch_shapes=[pltpu.VMEM((tm, W_A), F32), pltpu.VMEM((tm + 8, W_B), F32)],
        compiler_params=_params(2),
        name="ab_layer_prompt",
    )(x, *weights)


def _ab_layer_sample(x, s1, s2, weights):
    tm = x.shape[0]
    full = lambda cols: pl.BlockSpec((tm, cols), lambda i: (0, 0))
    return pl.pallas_call(
        functools.partial(_ab_kernel, tm=tm, chunk=tm, sample=True),
        grid=(1,),
        in_specs=[full(D_MODEL), full(W_B), full(W_B)] + _ab_weight_specs(tm),
        out_specs=[full(D_MODEL), full(W_B), full(W_A)],
        out_shape=[jax.ShapeDtypeStruct((tm, D_MODEL), F32),
                   jax.ShapeDtypeStruct((tm, W_B), F32),
                   jax.ShapeDtypeStruct((tm, W_A), F32)],
        scratch_shapes=[pltpu.VMEM((tm, W_A), F32)],
        compiler_params=_params(1),
        name="ab_layer_sample",
    )(x, s1, s2, *weights)


def _mlp_kernel(x_ref, w_up_ref, w_down_ref, g_ref, b_ref, o_ref):
    x = x_ref[...]
    xb = x.astype(BF16)
    acc = None
    for c in range(D_FF // D_MODEL):
        cols = slice(c * D_MODEL, (c + 1) * D_MODEL)
        h = jnp.maximum(jnp.dot(xb, w_up_ref[:, cols], preferred_element_type=F32), 0.0)
        part = jnp.dot((h * h).astype(BF16), w_down_ref[cols, :], preferred_element_type=F32)
        acc = part if acc is None else acc + part
    o_ref[...] = _layer_norm(ALPHA * x + acc, g_ref[...], b_ref[...])


def _mlp_layer(x, w_up, w_down, g, b):
    rows = x.shape[0]
    tm = min(ROW_TILE, rows)
    row_spec = pl.BlockSpec((tm, D_MODEL), lambda t: (t, 0))
    return pl.pallas_call(
        _mlp_kernel,
        grid=(rows // tm,),
        in_specs=[row_spec, _const_spec((D_MODEL, D_FF)), _const_spec((D_FF, D_MODEL)),
                  _const_spec((1, D_MODEL)), _const_spec((1, D_MODEL))],
        out_specs=row_spec,
        out_shape=jax.ShapeDtypeStruct((rows, D_MODEL), F32),
        compiler_params=_params(1),
        name="mlp_layer",
    )(x, w_up, w_down, g, b)


def _qkv_kernel(x_ref, w_ref, o_ref):
    xb = x_ref[...].astype(BF16)
    for c in range(QKV_BLOCKS):
        cols = slice(c * C_WIDTH, (c + 1) * C_WIDTH)
        o_ref[:, cols] = jnp.dot(xb, w_ref[:, cols], preferred_element_type=F32)


def _qkv_proj(x, w):
    rows = x.shape[0]
    tm = min(256, rows)
    return pl.pallas_call(
        _qkv_kernel,
        grid=(rows // tm,),
        in_specs=[pl.BlockSpec((tm, D_MODEL), lambda t: (t, 0)), _const_spec((D_MODEL, 3 * QKV_COLS))],
        out_specs=pl.BlockSpec((tm, 3 * QKV_COLS), lambda t: (t, 0)),
        out_shape=jax.ShapeDtypeStruct((rows, 3 * QKV_COLS), F32),
        compiler_params=_params(1),
        name="qkv_proj",
    )(x, w)


def _qkv_perm_kernel(*refs):
    x_refs, (w_ref, o_ref, xp_scr) = refs[:X_SLABS], refs[X_SLABS:]
    for g in range(N_GROUPS):
        @pl.when(pl.program_id(2) == g)
        def _(d=DILATIONS[g]):
            n = PERM_TILE // d
            for r in range(d):
                for s in range(X_SLABS):
                    rows = x_refs[s][...] if d == 1 else x_refs[s][pl.ds(r, n, stride=d), :]
                    xp_scr[r * n:(r + 1) * n, s * 128:(s + 1) * 128] = rows.astype(BF16)
    for c in range(PERM_TILE // ROW_TILE):
        rows = slice(c * ROW_TILE, (c + 1) * ROW_TILE)
        o_ref[rows, :] = jnp.dot(xp_scr[rows, :], w_ref[...], preferred_element_type=F32).astype(BF16)


def _qkv_proj_perm(x, w_groups, batch, seq):
    nt = seq // PERM_TILE
    slab = lambda s: pl.BlockSpec((PERM_TILE, 128), lambda b, t, g: (b * nt + t, s))
    return pl.pallas_call(
        _qkv_perm_kernel,
        grid=(batch, nt, N_GROUPS),
        in_specs=[slab(s) for s in range(X_SLABS)]
                 + [pl.BlockSpec((None, D_MODEL, 3 * C_WIDTH), lambda b, t, g: (g, 0, 0))],
        out_specs=pl.BlockSpec((None, PERM_TILE, 3 * C_WIDTH), lambda b, t, g: (g, b * nt + t, 0)),
        out_shape=jax.ShapeDtypeStruct((N_GROUPS, batch * seq, 3 * C_WIDTH), BF16),
        scratch_shapes=[pltpu.VMEM((PERM_TILE, D_MODEL), BF16)],
        compiler_params=_params(3),
        name="qkv_proj_perm",
    )(*([x] * X_SLABS), w_groups)


def _kv_tail_kernel(x_ref, wt_ref, o0_ref, o1_ref, o2_ref):
    xb = x_ref[...].astype(BF16)
    nt = (((1,), (1,)), ((), ()))
    o2_ref[...] = lax.dot_general(wt_ref[2], xb, nt, preferred_element_type=F32)

    @pl.when(pl.program_id(1) == pl.num_programs(1) - 1)
    def _():
        o1_ref[...] = lax.dot_general(wt_ref[1], xb, nt, preferred_element_type=F32)
        o0_ref[...] = lax.dot_general(wt_ref[0], xb[ROW_TILE - WINDOWS[0]:, :], nt, preferred_element_type=F32)


def _kv_tail(x, wt_groups, batch, seq):
    nj = WINDOWS[2] // ROW_TILE
    first = (seq - WINDOWS[2]) // ROW_TILE
    nblk = seq // ROW_TILE
    last = lambda n: pl.BlockSpec((None, 2 * C_WIDTH, n), lambda b, j: (b, 0, 0))
    return pl.pallas_call(
        _kv_tail_kernel,
        grid=(batch, nj),
        in_specs=[pl.BlockSpec((ROW_TILE, D_MODEL), lambda b, j: (b * nblk + first + j, 0)),
                  _const_spec((N_GROUPS, 2 * C_WIDTH, D_MODEL))],
        out_specs=[last(WINDOWS[0]), last(WINDOWS[1]),
                   pl.BlockSpec((None, 2 * C_WIDTH, ROW_TILE), lambda b, j: (b, 0, j))],
        out_shape=[jax.ShapeDtypeStruct((batch, 2 * C_WIDTH, n), F32) for n in WINDOWS],
        compiler_params=_params(2),
        name="kv_tail",
    )(x, wt_groups)


def _attn_kernel(q_ref, kc_ref, kp_ref, vc_ref, vp_ref, o_ref, lse_ref,
                 q_scr, k_scr, v_scr, bias_scr, *, dil, chain):
    first_step = (pl.program_id(0) == 0) & (pl.program_id(1) == 0) & (pl.program_id(2) == 0)

    @pl.when(first_step)
    def _():
        qi = lax.broadcasted_iota(jnp.int32, (CHUNK, 2 * CHUNK), 0)
        kj = lax.broadcasted_iota(jnp.int32, (CHUNK, 2 * CHUNK), 1)
        steps = CHUNK + qi - kj
        dist = (steps * dil).astype(F32)
        for h in range(H_G):
            b0 = jnp.where(steps >= 0, jnp.where(steps <= N_BACK, -SLOPES[h] * dist, NEG_INF), NEG_INF)
            bias_scr[0, h] = b0
            bias_scr[1, h] = jnp.where(kj < CHUNK, NEG_INF, b0)

    n_sub = ROW_TILE // CHUNK
    q_scr[...] = q_ref[...] * QK_SCALE
    if chain:
        key_stride = CHUNK
        for scr, prev, cur in ((k_scr, kp_ref, kc_ref), (v_scr, vp_ref, vc_ref)):
            scr[0:CHUNK, :] = prev[...]
            scr[CHUNK:CHUNK + ROW_TILE, :] = cur[...]
        run_steps = PERM_TILE // dil // ROW_TILE
        at_start = (pl.program_id(1) == 0) & (pl.program_id(2) % run_steps == 0)
    else:
        key_stride = 2 * CHUNK
        for scr, prev, cur in ((k_scr, kp_ref, kc_ref), (v_scr, vp_ref, vc_ref)):
            for s in range(n_sub):
                scr[s * key_stride:s * key_stride + CHUNK, :] = prev[s * CHUNK:(s + 1) * CHUNK, :]
                scr[s * key_stride + CHUNK:(s + 1) * key_stride, :] = cur[s * CHUNK:(s + 1) * CHUNK, :]
        at_start = pl.program_id(1) == 0

    def sub_block(s, carry):
        r0 = pl.multiple_of(s * CHUNK, CHUNK)
        k0 = pl.multiple_of(s * key_stride, CHUNK)
        bsel = jnp.where(at_start & (s == 0), 1, 0) if chain else jnp.where(at_start, 1, 0)
        for h in range(H_G):
            cols = slice(h * HEAD_DIM, (h + 1) * HEAD_DIM)
            qh = q_scr[pl.ds(r0, CHUNK), cols]
            kh = k_scr[pl.ds(k0, 2 * CHUNK), cols]
            vh = v_scr[pl.ds(k0, 2 * CHUNK), cols]
            logits = lax.dot_general(qh, kh, (((1,), (1,)), ((), ())), preferred_element_type=F32)
            logits = logits + bias_scr[bsel, h]
            m = jnp.max(logits, axis=-1, keepdims=True)
            p = jnp.exp(logits - m)
            l = jnp.sum(p, axis=-1, keepdims=True)
            o = jnp.dot(p.astype(BF16), vh, preferred_element_type=F32) / l
            o_ref[pl.ds(r0, CHUNK), cols] = o
            lse_ref[pl.ds(r0, CHUNK), cols] = jnp.broadcast_to(m + jnp.log(l), (CHUNK, HEAD_DIM))
        return carry

    lax.fori_loop(0, n_sub, sub_block, 0)


def _attn_prompt(qkv_perm, batch, seq, g):
    d = DILATIONS[g]
    nt = seq // PERM_TILE
    nj = PERM_TILE // ROW_TILE
    sub_per_step = ROW_TILE // CHUNK
    run = PERM_TILE // d
    chain = run > CHUNK
    assert run % ROW_TILE == 0 if chain else run == CHUNK
    cur = lambda b, t, j: (b * nt + t) * nj + j
    if chain:
        back = (PERM_TILE - run) // CHUNK + 1
        prev_rows = CHUNK
        prev = lambda b, t, j: jnp.maximum(cur(b, t, j) * sub_per_step - back, 0)
    else:
        prev_rows = ROW_TILE
        prev = lambda b, t, j: jnp.maximum(cur(b, t, j) - nj, 0)
    blk = lambda rows, idx, sec: pl.BlockSpec((None, rows, C_WIDTH), lambda b, t, j: (g, idx(b, t, j), sec))
    out_spec = pl.BlockSpec((ROW_TILE, C_WIDTH), lambda b, t, j: (cur(b, t, j), 0))
    key_rows = CHUNK + ROW_TILE if chain else 2 * ROW_TILE
    return pl.pallas_call(
        functools.partial(_attn_kernel, dil=d, chain=chain),
        grid=(batch, nt, nj),
        in_specs=[blk(ROW_TILE, cur, 0), blk(ROW_TILE, cur, 1), blk(prev_rows, prev, 1),
                  blk(ROW_TILE, cur, 2), blk(prev_rows, prev, 2)],
        out_specs=[out_spec, out_spec],
        out_shape=[jax.ShapeDtypeStruct((batch * seq, C_WIDTH), F32)] * 2,
        scratch_shapes=[pltpu.VMEM((ROW_TILE, C_WIDTH), BF16), pltpu.VMEM((key_rows, C_WIDTH), BF16),
                        pltpu.VMEM((key_rows, C_WIDTH), BF16),
                        pltpu.VMEM((2, H_G, CHUNK, 2 * CHUNK), F32)],
        compiler_params=_params(3),
        name=f"attn_prompt_d{d}",
    )(qkv_perm, qkv_perm, qkv_perm, qkv_perm, qkv_perm)


def _sattn_kernel(*refs, buf_len, dil, t_new):
    q_ref, kn_ref, vn_ref, cache_ref = refs[:4]
    o_ref, lse_ref, newc_ref, bias_c, bias_n = refs[-5:]
    new_lane0 = CHUNK - t_new

    @pl.when(pl.program_id(0) == 0)
    def _():
        def bias(dist, ok, h):
            b0 = jnp.where((dist & (dil - 1)) == 0, -SLOPES[h] * dist.astype(F32), NEG_INF)
            b0 = jnp.where(dist >= 0, b0, NEG_INF)
            b0 = jnp.where(dist <= N_BACK * dil, b0, NEG_INF)
            return b0 if ok is None else jnp.where(ok, b0, NEG_INF)
        t_c = lax.broadcasted_iota(jnp.int32, (t_new, buf_len), 0)
        p_c = lax.broadcasted_iota(jnp.int32, (t_new, buf_len), 1)
        t_n = lax.broadcasted_iota(jnp.int32, (t_new, CHUNK), 0)
        j_n = lax.broadcasted_iota(jnp.int32, (t_new, CHUNK), 1)
        for h in range(H_G):
            bias_c[h] = bias(buf_len + t_c - p_c, None, h)
            bias_n[h] = bias(t_n - (j_n - new_lane0), j_n >= new_lane0, h)

    kn = kn_ref[...]
    vn = vn_ref[...]
    pad = jnp.zeros((new_lane0, C_WIDTH), F32)
    kn_t = jnp.concatenate([pad, kn], axis=0).T
    vn_t = jnp.concatenate([pad, vn], axis=0).T
    lane = lax.broadcasted_iota(jnp.int32, (HEAD_DIM, CHUNK), 1)
    nt = (((1,), (1,)), ((), ()))

    qb = (q_ref[...] * QK_SCALE).astype(BF16)
    outs, lses = [], []
    for h in range(H_G):
        cols = slice(h * HEAD_DIM, (h + 1) * HEAD_DIM)
        k_t, v_t = cache_ref[0, h], cache_ref[1, h]
        knh_t, vnh_t = kn_t[cols], vn_t[cols]
        qh = qb[:, cols]
        lc = jnp.dot(qh, k_t.astype(BF16), preferred_element_type=F32) + bias_c[h]
        ln = jnp.dot(qh, knh_t.astype(BF16), preferred_element_type=F32) + bias_n[h]
        m = jnp.maximum(jnp.max(lc, axis=-1, keepdims=True), jnp.max(ln, axis=-1, keepdims=True))
        pc = jnp.exp(lc - m)
        pn = jnp.exp(ln - m)
        l = jnp.sum(pc, axis=-1, keepdims=True) + jnp.sum(pn, axis=-1, keepdims=True)
        o = (lax.dot_general(pc.astype(BF16), v_t.astype(BF16), nt, preferred_element_type=F32)
             + lax.dot_general(pn.astype(BF16), vnh_t.astype(BF16), nt, preferred_element_type=F32))
        outs.append(o / l)
        lses.append(jnp.broadcast_to(m + jnp.log(l), (t_new, HEAD_DIM)))
        for kv, old, new in ((0, k_t, knh_t), (1, v_t, vnh_t)):
            rolled = pltpu.roll(old, buf_len - t_new, 1)
            if buf_len > CHUNK:
                newc_ref[kv, h, :, 0:buf_len - CHUNK] = rolled[:, 0:buf_len - CHUNK]
            newc_ref[kv, h, :, buf_len - CHUNK:buf_len] = jnp.where(
                lane >= new_lane0, new, rolled[:, buf_len - CHUNK:buf_len])
    o_ref[...] = jnp.concatenate(outs, axis=-1)
    lse_ref[...] = jnp.concatenate(lses, axis=-1)


def _attn_sample(qkv, cache, prev_out, layer_idx, g, t_new):
    n_layers, n_seq, _, _, _, buf_len = cache.shape
    d = DILATIONS[g]
    row = lambda sec: pl.BlockSpec((t_new, C_WIDTH), lambda b: (b, sec * N_GROUPS + g))
    cache_spec = pl.BlockSpec((None, None, 2, H_G, HEAD_DIM, buf_len), lambda b: (layer_idx, b, 0, 0, 0, 0))
    out_row = pl.BlockSpec((t_new, C_WIDTH), lambda b: (b, 0))
    in_specs = [row(0), row(1), row(2), cache_spec]
    args = [qkv, qkv, qkv, cache]
    aliases = {}
    if prev_out is not None:
        in_specs.append(pl.BlockSpec(memory_space=pl.ANY))
        args.append(prev_out)
        aliases = {4: 2}
    return pl.pallas_call(
        functools.partial(_sattn_kernel, buf_len=buf_len, dil=d, t_new=t_new),
        grid=(n_seq,),
        in_specs=in_specs,
        out_specs=[out_row, out_row, cache_spec],
        out_shape=[jax.ShapeDtypeStruct((n_seq * t_new, C_WIDTH), F32)] * 2
                  + [jax.ShapeDtypeStruct(cache.shape, F32)],
        scratch_shapes=[pltpu.VMEM((H_G, t_new, buf_len), F32), pltpu.VMEM((H_G, t_new, CHUNK), F32)],
        input_output_aliases=aliases,
        compiler_params=_params(1),
        name=f"attn_sample_d{d}",
    )(*args)


def _merge_kernel(*refs, tm, dils):
    x_ref = refs[0]
    ol_refs = refs[1:1 + 2 * N_GROUPS]
    w_ref, g_ref, b_ref, out_ref = refs[1 + 2 * N_GROUPS:5 + 2 * N_GROUPS]
    scrs = list(refs[5 + 2 * N_GROUPS:])

    def natural(ref, d):
        if d == 1:
            return ref[...]
        scr = scrs.pop()
        n = tm // d
        for r in range(d):
            for s in range(C_WIDTH // 128):
                scr[s, pl.ds(r, n, stride=d), :] = ref[r, :, s * 128:(s + 1) * 128]
        return jnp.concatenate([scr[s] for s in range(C_WIDTH // 128)], axis=-1)

    vals = [natural(ol_refs[k], dils[k % N_GROUPS]) for k in range(2 * N_GROUPS)]
    (o0, o1, o2), (l0, l1, l2) = vals[:N_GROUPS], vals[N_GROUPS:]
    m = jnp.maximum(jnp.maximum(l0, l1), l2)
    e0, e1, e2 = jnp.exp(l0 - m), jnp.exp(l1 - m), jnp.exp(l2 - m)
    o = (e0 * o0 + e1 * o1 + e2 * o2) / (e0 + e1 + e2)
    x = x_ref[...]
    mix = jnp.dot(o.astype(BF16), w_ref[...], preferred_element_type=F32)
    out_ref[...] = _layer_norm(ALPHA * x + mix, g_ref[...], b_ref[...])


def _merge_layer(x, outs, lses, w_out, g, b, dils):
    rows = x.shape[0]
    tm = min(MERGE_TILE, rows)
    per_tile = PERM_TILE // tm
    row_spec = lambda cols: pl.BlockSpec((tm, cols), lambda t: (t, 0))

    def group_arg(a, d):
        if d == 1:
            return a, row_spec(C_WIDTH)
        run = PERM_TILE // d
        view = a.reshape(rows // PERM_TILE, d, run, C_WIDTH)
        return view, pl.BlockSpec((None, d, tm // d, C_WIDTH), lambda t: (t // per_tile, 0, t % per_tile, 0))

    args, specs = zip(*[group_arg(a, dils[k % N_GROUPS]) for k, a in enumerate(list(outs) + list(lses))])
    n_perm = 2 * sum(d > 1 for d in dils)
    return pl.pallas_call(
        functools.partial(_merge_kernel, tm=tm, dils=dils),
        grid=(rows // tm,),
        in_specs=[row_spec(D_MODEL)] + list(specs)
                 + [_const_spec((C_WIDTH, D_MODEL)), _const_spec((1, D_MODEL)), _const_spec((1, D_MODEL))],
        out_specs=row_spec(D_MODEL),
        out_shape=jax.ShapeDtypeStruct((rows, D_MODEL), F32),
        scratch_shapes=[pltpu.VMEM((C_WIDTH // 128, tm, 128), F32)] * n_perm,
        compiler_params=_params(1),
        name="merge_layer",
    )(x, *args, w_out, g, b)


def kernel(x_prompt, x_sample, state_conv, cache_kv_w128, cache_kv_w512, cache_kv_w2048, w_in_ab, ln_v_g, ln_v_b, w_spatial, b_spatial, conv_w, w_out_ab, w_qkv_c, w_out_c, ln1_g, ln1_b, ln2_g, ln2_b, w_mlp_up, w_mlp_down):
    batch, seq, _ = x_prompt.shape
    n_seq, t_new, _ = x_sample.shape
    n_tok_s = n_seq * t_new
    xp = x_prompt.reshape(batch * seq, D_MODEL)
    xs = x_sample.reshape(n_tok_s, D_MODEL)
    caches = [jnp.transpose(c, (0, 1, 3, 4, 5, 2)) for c in (cache_kv_w128, cache_kv_w512, cache_kv_w2048)]
    new_caches = [None] * N_GROUPS
    row = lambda a: a.reshape(1, -1)

    causal = jnp.tril(jnp.ones((CHUNK, CHUNK), dtype=bool))
    conv_p, conv_s, chunk_v_s = [], [], []
    kv_p = [[] for _ in range(N_GROUPS)]

    for layer in range(DEPTH):
        i = layer // 2
        if layer % 2 == 0:
            w_tril = jnp.where(causal[None], w_spatial[i], 0.0)
            bmix = jnp.repeat(b_spatial[i].T, CHUNK, axis=1)
            shared = (w_in_ab[i].astype(BF16), row(ln_v_g[i]), row(ln_v_b[i]))
            tail = (conv_w[i], w_out_ab[i].astype(BF16), row(ln1_g[layer]), row(ln1_b[layer]))
            xp, buf_p = _ab_layer_prompt(xp, batch, seq,
                                         shared + (w_tril.astype(BF16), bmix) + tail)
            eye = jnp.eye(n_seq, dtype=F32)
            w_blk = jnp.stack([jnp.kron(eye, w_tril[g, :t_new, :t_new]) for g in range(G_A)])
            bmix_s = jnp.tile(bmix[:t_new], (n_seq, 1))
            st = state_conv[i]
            s1 = jnp.concatenate([st[:, 1:2], jnp.zeros((n_seq, t_new - 1, W_B), F32)], axis=1)
            s2 = jnp.concatenate([st, jnp.zeros((n_seq, t_new - 2, W_B), F32)], axis=1)
            xs, hc_s, v_s = _ab_layer_sample(xs, s1.reshape(n_tok_s, W_B), s2.reshape(n_tok_s, W_B),
                                             shared + (w_blk.astype(BF16), bmix_s) + tail)
            conv_p.append(buf_p)
            conv_s.append(hc_s.reshape(n_seq, t_new, W_B)[:, t_new - 2:])
            chunk_v_s.append(v_s.reshape(n_seq, t_new, W_A))
        else:
            wq = w_qkv_c[i].astype(BF16)
            wo = w_out_c[i].astype(BF16)
            g1, b1 = row(ln1_g[layer]), row(ln1_b[layer])
            wq4 = wq.reshape(D_MODEL, 3, N_GROUPS, C_WIDTH)
            w_groups = jnp.transpose(wq4, (2, 0, 1, 3)).reshape(N_GROUPS, D_MODEL, 3 * C_WIDTH)
            wt_groups = jnp.transpose(wq4[:, 1:], (2, 1, 3, 0)).reshape(N_GROUPS, 2 * C_WIDTH, D_MODEL)
            tails = _kv_tail(xp, wt_groups, batch, seq)
            for g in range(N_GROUPS):
                kv_p[g].append(tails[g].reshape(batch, 2, H_G, HEAD_DIM, WINDOWS[g]))
            qkv_p = _qkv_proj_perm(xp, w_groups, batch, seq)
            outs, lses = zip(*[_attn_prompt(qkv_p, batch, seq, g) for g in range(N_GROUPS)])
            xp = _merge_layer(xp, outs, lses, wo, g1, b1, DILATIONS)
            qkv_s = _qkv_proj(xs, wq)
            outs, lses = [], []
            for g in range(N_GROUPS):
                o, lse, new_caches[g] = _attn_sample(qkv_s, caches[g], new_caches[g], i, g, t_new)
                outs.append(o)
                lses.append(lse)
            xs = _merge_layer(xs, outs, lses, wo, g1, b1, (1,) * N_GROUPS)
        w_up, w_down = w_mlp_up[layer].astype(BF16), w_mlp_down[layer].astype(BF16)
        g2, b2 = row(ln2_g[layer]), row(ln2_b[layer])
        xp = _mlp_layer(xp, w_up, w_down, g2, b2)
        xs = _mlp_layer(xs, w_up, w_down, g2, b2)

    kv_s = [jnp.transpose(nc, (0, 1, 5, 2, 3, 4)) for nc in new_caches]
    return (xp.reshape(batch, seq, D_MODEL), xs.reshape(n_seq, t_new, D_MODEL),
            jnp.stack(conv_p), jnp.stack(conv_s), jnp.stack(chunk_v_s),
            *[jnp.transpose(jnp.stack(kv_p[g]), (0, 1, 5, 2, 3, 4)) for g in range(N_GROUPS)],
            kv_s[0], kv_s[1], kv_s[2])
```

```python
import functools
import math

import jax
import jax.numpy as jnp
from jax import lax
from jax.experimental import pallas as pl
from jax.experimental.pallas import tpu as pltpu

F32 = jnp.float32
BF16 = jnp.bfloat16

D_MODEL = 1024
DEPTH = 4
CHUNK = 128
W_A = 512
G_A = 4
W_B = 512
N_GROUPS = 3
WINDOWS = (128, 512, 2048)
DILATIONS = (1, 4, 16)
N_BACK = 128
H_G = 8
HEAD_DIM = 64
C_WIDTH = H_G * HEAD_DIM
QKV_COLS = N_GROUPS * C_WIDTH
QKV_BLOCKS = 3 * N_GROUPS
D_FF = 4 * D_MODEL
ALPHA = (2.0 * DEPTH) ** 0.25
LN_EPS = 1e-5
NEG_INF = -1e30
SLOPES = tuple(2.0 ** (-(8.0 / H_G) * j) for j in range(1, H_G + 1))
QK_SCALE = HEAD_DIM ** -0.5

VMEM_LIMIT_BYTES = 52 * 1024 * 1024
ROW_TILE = 512
PERM_TILE = CHUNK * max(DILATIONS)
X_SLABS = D_MODEL // 128
MERGE_TILE = 256
LSE_LANES = 128


def _layer_norm(x, g, b):
    mu = jnp.mean(x, axis=-1, keepdims=True)
    xc = x - mu
    var = jnp.mean(xc * xc, axis=-1, keepdims=True)
    return xc * lax.rsqrt(var + LN_EPS) * g + b


def _gelu(x):
    c = math.sqrt(2.0 / math.pi)
    return x * (0.5 * (1.0 + jnp.tanh(c * (x + 0.044715 * (x * x * x)))))


def _const_spec(shape):
    nd = len(shape)
    return pl.BlockSpec(shape, lambda *_: (0,) * nd, pipeline_mode=pl.Buffered(1))


def _params(n_grid):
    return pltpu.CompilerParams(dimension_semantics=("arbitrary",) * n_grid,
                                vmem_limit_bytes=VMEM_LIMIT_BYTES)


def _ab_kernel(*refs, tm, chunk, sample):
    if sample:
        (x_ref, s1_ref, s2_ref, w_in_ref, lvg_ref, lvb_ref, wmix_ref, bmix_ref, cw_ref, w_out_ref,
         g1_ref, b1_ref, o_ref, hc_ref, v_ref, y_scr) = refs
    else:
        (x_ref, w_in_ref, lvg_ref, lvb_ref, wmix_ref, bmix_ref, cw_ref, w_out_ref,
         g1_ref, b1_ref, o_ref, cb_ref, y_scr, h_scr) = refs

    x = x_ref[...]
    xb = x.astype(BF16)

    def proj(k):
        return jnp.dot(xb, w_in_ref[:, k * W_A:(k + 1) * W_A], preferred_element_type=F32)

    u = _gelu(proj(0))
    v = _layer_norm(_gelu(proj(1)), lvg_ref[...], lvb_ref[...])
    if sample:
        v_ref[...] = v
    vb = v.astype(BF16)
    for c in range(tm // chunk):
        rows = slice(c * chunk, (c + 1) * chunk)
        for g in range(G_A):
            cols = slice(g * 128, (g + 1) * 128)
            y_scr[rows, cols] = (jnp.dot(wmix_ref[g], vb[rows, cols], preferred_element_type=F32)
                                 + bmix_ref[:, cols])
    a_out = (u * y_scr[...]).astype(BF16)

    hc = proj(3) * proj(4)
    if sample:
        hc_ref[...] = hc
        pos = lax.broadcasted_iota(jnp.int32, hc.shape, 0) & 7
        sh1 = jnp.where(pos == 0, s1_ref[...], pltpu.roll(hc, 1, 0))
        sh2 = jnp.where(pos < 2, s2_ref[...], pltpu.roll(hc, 2, 0))
    else:
        @pl.when(pl.program_id(1) == 0)
        def _():
            h_scr[0:8, :] = jnp.zeros((8, W_B), F32)
        h_scr[8:tm + 8, :] = hc
        sh1 = h_scr[7:tm + 7, :]
        sh2 = h_scr[6:tm + 6, :]
        cb_ref[...] = h_scr[tm + 6:tm + 8, :]
        h_scr[0:8, :] = h_scr[tm:tm + 8, :]
    conv = cw_ref[0:1, :] * sh2 + cw_ref[1:2, :] * sh1 + cw_ref[2:3, :] * hc
    b_out = (proj(2) * conv).astype(BF16)

    mix = (jnp.dot(a_out, w_out_ref[0:W_A, :], preferred_element_type=F32)
           + jnp.dot(b_out, w_out_ref[W_A:W_A + W_B, :], preferred_element_type=F32))
    o_ref[...] = _layer_norm(ALPHA * x + mix, g1_ref[...], b1_ref[...])


def _ab_weight_specs(chunk):
    return [_const_spec((D_MODEL, 2 * W_A + 3 * W_B)), _const_spec((1, W_A)), _const_spec((1, W_A)),
            _const_spec((G_A, chunk, chunk)), _const_spec((chunk, W_A)), _const_spec((3, W_B)),
            _const_spec((W_A + W_B, D_MODEL)), _const_spec((1, D_MODEL)), _const_spec((1, D_MODEL))]


def _ab_layer_prompt(x, batch, seq, weights):
    tm = ROW_TILE
    nt = seq // tm
    row_spec = pl.BlockSpec((tm, D_MODEL), lambda b, t: (b * nt + t, 0))
    return pl.pallas_call(
        functools.partial(_ab_kernel, tm=tm, chunk=CHUNK, sample=False),
        grid=(batch, nt),
        in_specs=[row_spec] + _ab_weight_specs(CHUNK),
        out_specs=[row_spec, pl.BlockSpec((None, 2, W_B), lambda b, t: (b, 0, 0))],
        out_shape=[jax.ShapeDtypeStruct((batch * seq, D_MODEL), F32),
                   jax.ShapeDtypeStruct((batch, 2, W_B), F32)],
        scratch_shapes=[pltpu.VMEM((tm, W_A), F32), pltpu.VMEM((tm + 8, W_B), F32)],
        compiler_params=_params(2),
        name="ab_layer_prompt",
    )(x, *weights)


def _ab_layer_sample(x, s1, s2, weights):
    tm = x.shape[0]
    full = lambda cols: pl.BlockSpec((tm, cols), lambda i: (0, 0))
    return pl.pallas_call(
        functools.partial(_ab_kernel, tm=tm, chunk=tm, sample=True),
        grid=(1,),
        in_specs=[full(D_MODEL), full(W_B), full(W_B)] + _ab_weight_specs(tm),
        out_specs=[full(D_MODEL), full(W_B), full(W_A)],
        out_shape=[jax.ShapeDtypeStruct((tm, D_MODEL), F32),
                   jax.ShapeDtypeStruct((tm, W_B), F32),
                   jax.ShapeDtypeStruct((tm, W_A), F32)],
        scratch_shapes=[pltpu.VMEM((tm, W_A), F32)],
        compiler_params=_params(1),
        name="ab_layer_sample",
    )(x, s1, s2, *weights)


def _mlp_kernel(x_ref, w_up_ref, w_down_ref, g_ref, b_ref, o_ref):
    x = x_ref[...]
    xb = x.astype(BF16)
    acc = None
    for c in range(D_FF // D_MODEL):
        cols = slice(c * D_MODEL, (c + 1) * D_MODEL)
        h = jnp.maximum(jnp.dot(xb, w_up_ref[:, cols], preferred_element_type=F32), 0.0)
        part = jnp.dot((h * h).astype(BF16), w_down_ref[cols, :], preferred_element_type=F32)
        acc = part if acc is None else acc + part
    o_ref[...] = _layer_norm(ALPHA * x + acc, g_ref[...], b_ref[...])


def _mlp_layer(x, w_up, w_down, g, b):
    rows = x.shape[0]
    tm = min(ROW_TILE, rows)
    row_spec = pl.BlockSpec((tm, D_MODEL), lambda t: (t, 0))
    return pl.pallas_call(
        _mlp_kernel,
        grid=(rows // tm,),
        in_specs=[row_spec, _const_spec((D_MODEL, D_FF)), _const_spec((D_FF, D_MODEL)),
                  _const_spec((1, D_MODEL)), _const_spec((1, D_MODEL))],
        out_specs=row_spec,
        out_shape=jax.ShapeDtypeStruct((rows, D_MODEL), F32),
        compiler_params=_params(1),
        name="mlp_layer",
    )(x, w_up, w_down, g, b)


def _qkv_kernel(x_ref, w_ref, o_ref):
    xb = x_ref[...].astype(BF16)
    for c in range(QKV_BLOCKS):
        cols = slice(c * C_WIDTH, (c + 1) * C_WIDTH)
        o_ref[:, cols] = jnp.dot(xb, w_ref[:, cols], preferred_element_type=F32)


def _qkv_proj(x, w):
    rows = x.shape[0]
    tm = min(256, rows)
    return pl.pallas_call(
        _qkv_kernel,
        grid=(rows // tm,),
        in_specs=[pl.BlockSpec((tm, D_MODEL), lambda t: (t, 0)), _const_spec((D_MODEL, 3 * QKV_COLS))],
        out_specs=pl.BlockSpec((tm, 3 * QKV_COLS), lambda t: (t, 0)),
        out_shape=jax.ShapeDtypeStruct((rows, 3 * QKV_COLS), F32),
        compiler_params=_params(1),
        name="qkv_proj",
    )(x, w)


def _qkv_perm_kernel(*refs):
    x_refs, (wk_ref, wqv_t_ref, k_ref, qt_ref, vt_ref, xp_scr) = refs[:X_SLABS], refs[X_SLABS:]
    for g in range(N_GROUPS):
        @pl.when(pl.program_id(2) == g)
        def _(d=DILATIONS[g]):
            n = PERM_TILE // d
            for r in range(d):
                for s in range(X_SLABS):
                    rows = x_refs[s][...] if d == 1 else x_refs[s][pl.ds(r, n, stride=d), :]
                    xp_scr[r * n:(r + 1) * n, s * 128:(s + 1) * 128] = rows.astype(BF16)
    nt_dims = (((1,), (1,)), ((), ()))
    for c in range(PERM_TILE // ROW_TILE):
        rows = slice(c * ROW_TILE, (c + 1) * ROW_TILE)
        xs = xp_scr[rows, :]
        k_ref[rows, :] = jnp.dot(xs, wk_ref[...], preferred_element_type=F32).astype(BF16)
        qv_t = lax.dot_general(wqv_t_ref[...], xs, nt_dims, preferred_element_type=F32)
        qt_ref[:, rows] = (qv_t[0:C_WIDTH] * QK_SCALE).astype(BF16)
        vt_ref[:, rows] = qv_t[C_WIDTH:2 * C_WIDTH].astype(BF16)


def _qkv_proj_perm(x, wk_groups, wqv_t_groups, batch, seq):
    nt = seq // PERM_TILE
    slab = lambda s: pl.BlockSpec((PERM_TILE, 128), lambda b, t, g: (b * nt + t, s))
    chan_spec = pl.BlockSpec((None, C_WIDTH, PERM_TILE), lambda b, t, g: (g, 0, b * nt + t))
    chan_shape = jax.ShapeDtypeStruct((N_GROUPS, C_WIDTH, batch * seq), BF16)
    return pl.pallas_call(
        _qkv_perm_kernel,
        grid=(batch, nt, N_GROUPS),
        in_specs=[slab(s) for s in range(X_SLABS)]
                 + [pl.BlockSpec((None, D_MODEL, C_WIDTH), lambda b, t, g: (g, 0, 0)),
                    pl.BlockSpec((None, 2 * C_WIDTH, D_MODEL), lambda b, t, g: (g, 0, 0))],
        out_specs=[pl.BlockSpec((None, PERM_TILE, C_WIDTH), lambda b, t, g: (g, b * nt + t, 0)),
                   chan_spec, chan_spec],
        out_shape=[jax.ShapeDtypeStruct((N_GROUPS, batch * seq, C_WIDTH), BF16), chan_shape, chan_shape],
        scratch_shapes=[pltpu.VMEM((PERM_TILE, D_MODEL), BF16)],
        compiler_params=_params(3),
        name="qkv_proj_perm",
    )(*([x] * X_SLABS), wk_groups, wqv_t_groups)


def _kv_tail_kernel(x_ref, wt_ref, o0_ref, o1_ref, o2_ref):
    xb = x_ref[...].astype(BF16)
    nt = (((1,), (1,)), ((), ()))
    o2_ref[...] = lax.dot_general(wt_ref[2], xb, nt, preferred_element_type=F32)

    @pl.when(pl.program_id(1) == pl.num_programs(1) - 1)
    def _():
        o1_ref[...] = lax.dot_general(wt_ref[1], xb, nt, preferred_element_type=F32)
        o0_ref[...] = lax.dot_general(wt_ref[0], xb[ROW_TILE - WINDOWS[0]:, :], nt, preferred_element_type=F32)


def _kv_tail(x, wt_groups, batch, seq):
    nj = WINDOWS[2] // ROW_TILE
    first = (seq - WINDOWS[2]) // ROW_TILE
    nblk = seq // ROW_TILE
    last = lambda n: pl.BlockSpec((None, 2 * C_WIDTH, n), lambda b, j: (b, 0, 0))
    return pl.pallas_call(
        _kv_tail_kernel,
        grid=(batch, nj),
        in_specs=[pl.BlockSpec((ROW_TILE, D_MODEL), lambda b, j: (b * nblk + first + j, 0)),
                  _const_spec((N_GROUPS, 2 * C_WIDTH, D_MODEL))],
        out_specs=[last(WINDOWS[0]), last(WINDOWS[1]),
                   pl.BlockSpec((None, 2 * C_WIDTH, ROW_TILE), lambda b, j: (b, 0, j))],
        out_shape=[jax.ShapeDtypeStruct((batch, 2 * C_WIDTH, n), F32) for n in WINDOWS],
        compiler_params=_params(2),
        name="kv_tail",
    )(x, wt_groups)


def _attn_kernel(qt_ref, kc_ref, kp_ref, vtc_ref, vtp_ref, o_ref, lse_ref, bias_scr, *, dil, chain):
    first_step = (pl.program_id(0) == 0) & (pl.program_id(1) == 0) & (pl.program_id(2) == 0)

    @pl.when(first_step)
    def _():
        kj = lax.broadcasted_iota(jnp.int32, (CHUNK, CHUNK), 0)
        qi = lax.broadcasted_iota(jnp.int32, (CHUNK, CHUNK), 1)
        back_own = qi - kj
        back_prev = back_own + CHUNK
        for h in range(H_G):
            slope = SLOPES[h] * dil
            bias_scr[0, h] = jnp.where(back_prev <= N_BACK, -slope * back_prev.astype(F32), NEG_INF)
            bias_scr[1, h] = jnp.full((CHUNK, CHUNK), NEG_INF, F32)
            bias_scr[2, h] = jnp.where(back_own >= 0, -slope * back_own.astype(F32), NEG_INF)

    n_sub = ROW_TILE // CHUNK
    if chain:
        run_steps = PERM_TILE // dil // ROW_TILE
        at_start = (pl.program_id(1) == 0) & (pl.program_id(2) % run_steps == 0)
    else:
        at_start = pl.program_id(1) == 0
    zeros_half = jnp.zeros((HEAD_DIM, CHUNK), BF16)
    lse_pad = jnp.zeros((CHUNK - H_G, CHUNK), F32)

    def half_tile(k_rows, qt_h, vt_cols, bias):
        st = jnp.dot(k_rows, qt_h, preferred_element_type=F32) + bias
        m = jnp.max(st, axis=0, keepdims=True)
        p = jnp.exp(st - m)
        l = jnp.sum(p, axis=0, keepdims=True)
        return m, l, jnp.dot(vt_cols, p.astype(BF16), preferred_element_type=F32)

    for s in range(n_sub):
        own = slice(s * CHUNK, (s + 1) * CHUNK)
        if chain and s > 0:
            prev_k, prev_vt = kc_ref, vtc_ref
            prv = slice((s - 1) * CHUNK, s * CHUNK)
        else:
            prev_k, prev_vt = kp_ref, vtp_ref
            prv = own if not chain else slice(0, CHUNK)
        masked = at_start if (s == 0 or not chain) else None
        lses = []
        for pair in range(H_G // 2):
            lanes = slice(pair * 2 * HEAD_DIM, (pair + 1) * 2 * HEAD_DIM)
            qt_pair = qt_ref[lanes, own]
            outs = []
            for half in range(2):
                h = 2 * pair + half
                chans = slice(h * HEAD_DIM, (h + 1) * HEAD_DIM)
                qt_h = (jnp.concatenate([qt_pair[0:HEAD_DIM], zeros_half], axis=0) if half == 0
                        else jnp.concatenate([zeros_half, qt_pair[HEAD_DIM:]], axis=0))
                bias_prev = bias_scr[0, h] if masked is None else bias_scr[jnp.where(masked, 1, 0), h]
                m_a, l_a, o_a = half_tile(prev_k[prv, lanes], qt_h, prev_vt[chans, prv], bias_prev)
                m_b, l_b, o_b = half_tile(kc_ref[own, lanes], qt_h, vtc_ref[chans, own], bias_scr[2, h])
                m = jnp.maximum(m_a, m_b)
                w_a, w_b = jnp.exp(m_a - m), jnp.exp(m_b - m)
                l = w_a * l_a + w_b * l_b
                outs.append((w_a * o_a + w_b * o_b) * (1.0 / l))
                lses.append(m + jnp.log(l))
            o_ref[own, lanes] = jnp.concatenate(outs, axis=0).T
        lse_ref[own, :] = jnp.concatenate(lses + [lse_pad], axis=0).T


def _attn_prompt(k_all, qt_all, vt_all, batch, seq, g):
    d = DILATIONS[g]
    nt = seq // PERM_TILE
    nj = PERM_TILE // ROW_TILE
    sub_per_step = ROW_TILE // CHUNK
    run = PERM_TILE // d
    chain = run > CHUNK
    assert run % ROW_TILE == 0 if chain else run == CHUNK
    cur = lambda b, t, j: (b * nt + t) * nj + j
    if chain:
        back = (PERM_TILE - run) // CHUNK + 1
        prev_rows = CHUNK
        prev = lambda b, t, j: jnp.maximum(cur(b, t, j) * sub_per_step - back, 0)
    else:
        prev_rows = ROW_TILE
        prev = lambda b, t, j: jnp.maximum(cur(b, t, j) - nj, 0)
    tok = lambda rows, idx: pl.BlockSpec((None, rows, C_WIDTH), lambda b, t, j: (g, idx(b, t, j), 0))
    chn = lambda cols, idx: pl.BlockSpec((None, C_WIDTH, cols), lambda b, t, j: (g, 0, idx(b, t, j)))
    return pl.pallas_call(
        functools.partial(_attn_kernel, dil=d, chain=chain),
        grid=(batch, nt, nj),
        in_specs=[chn(ROW_TILE, cur), tok(ROW_TILE, cur), tok(prev_rows, prev),
                  chn(ROW_TILE, cur), chn(prev_rows, prev)],
        out_specs=[pl.BlockSpec((ROW_TILE, C_WIDTH), lambda b, t, j: (cur(b, t, j), 0)),
                   pl.BlockSpec((ROW_TILE, LSE_LANES), lambda b, t, j: (cur(b, t, j), 0))],
        out_shape=[jax.ShapeDtypeStruct((batch * seq, C_WIDTH), F32),
                   jax.ShapeDtypeStruct((batch * seq, LSE_LANES), F32)],
        scratch_shapes=[pltpu.VMEM((3, H_G, CHUNK, CHUNK), F32)],
        compiler_params=_params(3),
        name=f"attn_prompt_d{d}",
    )(qt_all, k_all, k_all, vt_all, vt_all)


def _sattn_kernel(*refs, buf_len, dil, t_new):
    q_ref, kn_ref, vn_ref, cache_ref = refs[:4]
    o_ref, lse_ref, newc_ref, bias_c, bias_n = refs[-5:]
    new_lane0 = CHUNK - t_new

    @pl.when(pl.program_id(0) == 0)
    def _():
        def bias(dist, ok, h):
            b0 = jnp.where((dist & (dil - 1)) == 0, -SLOPES[h] * dist.astype(F32), NEG_INF)
            b0 = jnp.where(dist >= 0, b0, NEG_INF)
            b0 = jnp.where(dist <= N_BACK * dil, b0, NEG_INF)
            return b0 if ok is None else jnp.where(ok, b0, NEG_INF)
        t_c = lax.broadcasted_iota(jnp.int32, (t_new, buf_len), 0)
        p_c = lax.broadcasted_iota(jnp.int32, (t_new, buf_len), 1)
        t_n = lax.broadcasted_iota(jnp.int32, (t_new, CHUNK), 0)
        j_n = lax.broadcasted_iota(jnp.int32, (t_new, CHUNK), 1)
        for h in range(H_G):
            bias_c[h] = bias(buf_len + t_c - p_c, None, h)
            bias_n[h] = bias(t_n - (j_n - new_lane0), j_n >= new_lane0, h)

    kn = kn_ref[...]
    vn = vn_ref[...]
    pad = jnp.zeros((new_lane0, C_WIDTH), F32)
    kn_t = jnp.concatenate([pad, kn], axis=0).T
    vn_t = jnp.concatenate([pad, vn], axis=0).T
    lane = lax.broadcasted_iota(jnp.int32, (HEAD_DIM, CHUNK), 1)
    nt = (((1,), (1,)), ((), ()))

    qb = (q_ref[...] * QK_SCALE).astype(BF16)
    outs, lses = [], []
    for h in range(H_G):
        cols = slice(h * HEAD_DIM, (h + 1) * HEAD_DIM)
        k_t, v_t = cache_ref[0, h], cache_ref[1, h]
        knh_t, vnh_t = kn_t[cols], vn_t[cols]
        qh = qb[:, cols]
        lc = jnp.dot(qh, k_t.astype(BF16), preferred_element_type=F32) + bias_c[h]
        ln = jnp.dot(qh, knh_t.astype(BF16), preferred_element_type=F32) + bias_n[h]
        m = jnp.maximum(jnp.max(lc, axis=-1, keepdims=True), jnp.max(ln, axis=-1, keepdims=True))
        pc = jnp.exp(lc - m)
        pn = jnp.exp(ln - m)
        l = jnp.sum(pc, axis=-1, keepdims=True) + jnp.sum(pn, axis=-1, keepdims=True)
        o = (lax.dot_general(pc.astype(BF16), v_t.astype(BF16), nt, preferred_element_type=F32)
             + lax.dot_general(pn.astype(BF16), vnh_t.astype(BF16), nt, preferred_element_type=F32))
        outs.append(o / l)
        lses.append(jnp.broadcast_to(m + jnp.log(l), (t_new, HEAD_DIM)))
        for kv, old, new in ((0, k_t, knh_t), (1, v_t, vnh_t)):
            rolled = pltpu.roll(old, buf_len - t_new, 1)
            if buf_len > CHUNK:
                newc_ref[kv, h, :, 0:buf_len - CHUNK] = rolled[:, 0:buf_len - CHUNK]
            newc_ref[kv, h, :, buf_len - CHUNK:buf_len] = jnp.where(
                lane >= new_lane0, new, rolled[:, buf_len - CHUNK:buf_len])
    o_ref[...] = jnp.concatenate(outs, axis=-1)
    lse_ref[...] = jnp.concatenate(lses, axis=-1)


def _attn_sample(qkv, cache, prev_out, layer_idx, g, t_new):
    n_layers, n_seq, _, _, _, buf_len = cache.shape
    d = DILATIONS[g]
    row = lambda sec: pl.BlockSpec((t_new, C_WIDTH), lambda b: (b, sec * N_GROUPS + g))
    cache_spec = pl.BlockSpec((None, None, 2, H_G, HEAD_DIM, buf_len), lambda b: (layer_idx, b, 0, 0, 0, 0))
    out_row = pl.BlockSpec((t_new, C_WIDTH), lambda b: (b, 0))
    in_specs = [row(0), row(1), row(2), cache_spec]
    args = [qkv, qkv, qkv, cache]
    aliases = {}
    if prev_out is not None:
        in_specs.append(pl.BlockSpec(memory_space=pl.ANY))
        args.append(prev_out)
        aliases = {4: 2}
    return pl.pallas_call(
        functools.partial(_sattn_kernel, buf_len=buf_len, dil=d, t_new=t_new),
        grid=(n_seq,),
        in_specs=in_specs,
        out_specs=[out_row, out_row, cache_spec],
        out_shape=[jax.ShapeDtypeStruct((n_seq * t_new, C_WIDTH), F32)] * 2
                  + [jax.ShapeDtypeStruct(cache.shape, F32)],
        scratch_shapes=[pltpu.VMEM((H_G, t_new, buf_len), F32), pltpu.VMEM((H_G, t_new, CHUNK), F32)],
        input_output_aliases=aliases,
        compiler_params=_params(1),
        name=f"attn_sample_d{d}",
    )(*args)


def _merge_kernel(*refs, tm, dils):
    x_ref = refs[0]
    ol_refs = refs[1:1 + 2 * N_GROUPS]
    w_ref, g_ref, b_ref, out_ref = refs[1 + 2 * N_GROUPS:5 + 2 * N_GROUPS]
    scrs = list(refs[5 + 2 * N_GROUPS:])

    def natural(ref, d):
        if d == 1:
            return ref[...]
        scr = scrs.pop()
        n = tm // d
        slabs = ref.shape[-1] // 128
        for r in range(d):
            for s in range(slabs):
                scr[s, pl.ds(r, n, stride=d), :] = ref[r, :, s * 128:(s + 1) * 128]
        return jnp.concatenate([scr[s] for s in range(slabs)], axis=-1)

    vals = [natural(ol_refs[k], dils[k % N_GROUPS]) for k in range(2 * N_GROUPS)]
    (o0, o1, o2), (l0, l1, l2) = vals[:N_GROUPS], vals[N_GROUPS:]
    m = jnp.maximum(jnp.maximum(l0, l1), l2)
    e0, e1, e2 = jnp.exp(l0 - m), jnp.exp(l1 - m), jnp.exp(l2 - m)
    inv = 1.0 / (e0 + e1 + e2)
    alphas = [e0 * inv, e1 * inv, e2 * inv]
    if l0.shape[-1] != C_WIDTH:
        lane = lax.broadcasted_iota(jnp.int32, (LSE_LANES, C_WIDTH), 0)
        chan = lax.broadcasted_iota(jnp.int32, (LSE_LANES, C_WIDTH), 1)
        spread = jnp.where((chan >> (HEAD_DIM.bit_length() - 1)) == lane, 1.0, 0.0).astype(BF16)

        def expand(a):
            hi = a.astype(BF16)
            lo = (a - hi.astype(F32)).astype(BF16)
            return (jnp.dot(hi, spread, preferred_element_type=F32)
                    + jnp.dot(lo, spread, preferred_element_type=F32))
        alphas = [expand(a) for a in alphas]
    o = alphas[0] * o0 + alphas[1] * o1 + alphas[2] * o2
    x = x_ref[...]
    mix = jnp.dot(o.astype(BF16), w_ref[...], preferred_element_type=F32)
    out_ref[...] = _layer_norm(ALPHA * x + mix, g_ref[...], b_ref[...])


def _merge_layer(x, outs, lses, w_out, g, b, dils):
    rows = x.shape[0]
    tm = min(MERGE_TILE, rows)
    per_tile = PERM_TILE // tm
    row_spec = lambda cols: pl.BlockSpec((tm, cols), lambda t: (t, 0))

    def group_arg(a, d):
        cols = a.shape[-1]
        if d == 1:
            return a, row_spec(cols), None
        run = PERM_TILE // d
        view = a.reshape(rows // PERM_TILE, d, run, cols)
        spec = pl.BlockSpec((None, d, tm // d, cols), lambda t: (t // per_tile, 0, t % per_tile, 0))
        return view, spec, pltpu.VMEM((cols // 128, tm, 128), F32)

    args, specs, scratch = zip(*[group_arg(a, dils[k % N_GROUPS]) for k, a in enumerate(list(outs) + list(lses))])
    scratch = [s for s in scratch if s is not None][::-1]
    return pl.pallas_call(
        functools.partial(_merge_kernel, tm=tm, dils=dils),
        grid=(rows // tm,),
        in_specs=[row_spec(D_MODEL)] + list(specs)
                 + [_const_spec((C_WIDTH, D_MODEL)), _const_spec((1, D_MODEL)), _const_spec((1, D_MODEL))],
        out_specs=row_spec(D_MODEL),
        out_shape=jax.ShapeDtypeStruct((rows, D_MODEL), F32),
        scratch_shapes=scratch,
        compiler_params=_params(1),
        name="merge_layer",
    )(x, *args, w_out, g, b)


def kernel(x_prompt, x_sample, state_conv, cache_kv_w128, cache_kv_w512, cache_kv_w2048, w_in_ab, ln_v_g, ln_v_b, w_spatial, b_spatial, conv_w, w_out_ab, w_qkv_c, w_out_c, ln1_g, ln1_b, ln2_g, ln2_b, w_mlp_up, w_mlp_down):
    batch, seq, _ = x_prompt.shape
    n_seq, t_new, _ = x_sample.shape
    n_tok_s = n_seq * t_new
    xp = x_prompt.reshape(batch * seq, D_MODEL)
    xs = x_sample.reshape(n_tok_s, D_MODEL)
    caches = [jnp.transpose(c, (0, 1, 3, 4, 5, 2)) for c in (cache_kv_w128, cache_kv_w512, cache_kv_w2048)]
    new_caches = [None] * N_GROUPS
    row = lambda a: a.reshape(1, -1)

    causal = jnp.tril(jnp.ones((CHUNK, CHUNK), dtype=bool))
    conv_p, conv_s, chunk_v_s = [], [], []
    kv_p = [[] for _ in range(N_GROUPS)]

    for layer in range(DEPTH):
        i = layer // 2
        if layer % 2 == 0:
            w_tril = jnp.where(causal[None], w_spatial[i], 0.0)
            bmix = jnp.repeat(b_spatial[i].T, CHUNK, axis=1)
            shared = (w_in_ab[i].astype(BF16), row(ln_v_g[i]), row(ln_v_b[i]))
            tail = (conv_w[i], w_out_ab[i].astype(BF16), row(ln1_g[layer]), row(ln1_b[layer]))
            xp, buf_p = _ab_layer_prompt(xp, batch, seq,
                                         shared + (w_tril.astype(BF16), bmix) + tail)
            eye = jnp.eye(n_seq, dtype=F32)
            w_blk = jnp.stack([jnp.kron(eye, w_tril[g, :t_new, :t_new]) for g in range(G_A)])
            bmix_s = jnp.tile(bmix[:t_new], (n_seq, 1))
            st = state_conv[i]
            s1 = jnp.concatenate([st[:, 1:2], jnp.zeros((n_seq, t_new - 1, W_B), F32)], axis=1)
            s2 = jnp.concatenate([st, jnp.zeros((n_seq, t_new - 2, W_B), F32)], axis=1)
            xs, hc_s, v_s = _ab_layer_sample(xs, s1.reshape(n_tok_s, W_B), s2.reshape(n_tok_s, W_B),
                                             shared + (w_blk.astype(BF16), bmix_s) + tail)
            conv_p.append(buf_p)
            conv_s.append(hc_s.reshape(n_seq, t_new, W_B)[:, t_new - 2:])
            chunk_v_s.append(v_s.reshape(n_seq, t_new, W_A))
        else:
            wq = w_qkv_c[i].astype(BF16)
            wo = w_out_c[i].astype(BF16)
            g1, b1 = row(ln1_g[layer]), row(ln1_b[layer])
            wq4 = wq.reshape(D_MODEL, 3, N_GROUPS, C_WIDTH)
            wk_groups = jnp.transpose(wq4[:, 1], (1, 0, 2))
            wqv_t_groups = jnp.transpose(wq4[:, 0::2], (2, 1, 3, 0)).reshape(N_GROUPS, 2 * C_WIDTH, D_MODEL)
            wkv_t_groups = jnp.transpose(wq4[:, 1:], (2, 1, 3, 0)).reshape(N_GROUPS, 2 * C_WIDTH, D_MODEL)
            tails = _kv_tail(xp, wkv_t_groups, batch, seq)
            for g in range(N_GROUPS):
                kv_p[g].append(tails[g].reshape(batch, 2, H_G, HEAD_DIM, WINDOWS[g]))
            k_all, qt_all, vt_all = _qkv_proj_perm(xp, wk_groups, wqv_t_groups, batch, seq)
            outs, lses = zip(*[_attn_prompt(k_all, qt_all, vt_all, batch, seq, g) for g in range(N_GROUPS)])
            xp = _merge_layer(xp, outs, lses, wo, g1, b1, DILATIONS)
            qkv_s = _qkv_proj(xs, wq)
            outs, lses = [], []
            for g in range(N_GROUPS):
                o, lse, new_caches[g] = _attn_sample(qkv_s, caches[g], new_caches[g], i, g, t_new)
                outs.append(o)
                lses.append(lse)
            xs = _merge_layer(xs, outs, lses, wo, g1, b1, (1,) * N_GROUPS)
        w_up, w_down = w_mlp_up[layer].astype(BF16), w_mlp_down[layer].astype(BF16)
        g2, b2 = row(ln2_g[layer]), row(ln2_b[layer])
        xp = _mlp_layer(xp, w_up, w_down, g2, b2)
        xs = _mlp_layer(xs, w_up, w_down, g2, b2)

    kv_s = [jnp.transpose(nc, (0, 1, 5, 2, 3, 4)) for nc in new_caches]
    return (xp.reshape(batch, seq, D_MODEL), xs.reshape(n_seq, t_new, D_MODEL),
            jnp.stack(conv_p), jnp.stack(conv_s), jnp.stack(chunk_v_s),
            *[jnp.transpose(jnp.stack(kv_p[g]), (0, 1, 5, 2, 3, 4)) for g in range(N_GROUPS)],
            kv_s[0], kv_s[1], kv_s[2])
```

```python
import functools
import math
from typing import NamedTuple

import jax
import jax.numpy as jnp
from jax import lax
from jax.experimental import pallas as pl
from jax.experimental.pallas import tpu as pltpu

F32 = jnp.float32
BF16 = jnp.bfloat16

D_MODEL = 1024
DEPTH = 4
CHUNK = 128
W_A = 512
G_A = 4
W_B = 512
N_GROUPS = 3
WINDOWS = (128, 512, 2048)
DILATIONS = (1, 4, 16)
N_BACK = 128
H_G = 8
HEAD_DIM = 64
C_WIDTH = H_G * HEAD_DIM
QKV_COLS = N_GROUPS * C_WIDTH
QKV_BLOCKS = 3 * N_GROUPS
D_FF = 4 * D_MODEL
ALPHA = (2.0 * DEPTH) ** 0.25
LN_EPS = 1e-5
NEG_INF = -1e30
SLOPES = tuple(2.0 ** (-(8.0 / H_G) * j) for j in range(1, H_G + 1))
QK_SCALE = HEAD_DIM ** -0.5

VMEM_LIMIT_BYTES = 52 * 1024 * 1024
ROW_TILE = 512
PERM_TILE = CHUNK * max(DILATIONS)
X_SLABS = D_MODEL // 128
MERGE_TILE = 512
AB_PARTS = 1
LSE_LANES = 128


def _layer_norm(x, g, b):
    mu = jnp.mean(x, axis=-1, keepdims=True)
    xc = x - mu
    var = jnp.mean(xc * xc, axis=-1, keepdims=True)
    return xc * lax.rsqrt(var + LN_EPS) * g + b


def _gelu(x):
    c = math.sqrt(2.0 / math.pi)
    return x * (0.5 * (1.0 + jnp.tanh(c * (x + 0.044715 * (x * x * x)))))


class _Layer(NamedTuple):
    stacked: jax.Array
    index: int


def _resident(w):
    if isinstance(w, _Layer):
        shape = w.stacked.shape[1:]
        return w.stacked, pl.BlockSpec((None,) + shape, lambda *_: (w.index,) + (0,) * len(shape),
                                       pipeline_mode=pl.Buffered(1))
    return w, pl.BlockSpec(w.shape, lambda *_: (0,) * w.ndim, pipeline_mode=pl.Buffered(1))


def _resident_all(ws):
    arrays, specs = zip(*[_resident(w) for w in ws])
    return list(arrays), list(specs)


def _params(n_grid):
    return pltpu.CompilerParams(dimension_semantics=("arbitrary",) * n_grid,
                                vmem_limit_bytes=VMEM_LIMIT_BYTES)


def _ab_kernel(*refs, tm, chunk, sample, n_parts):
    if sample:
        (x_ref, s1_ref, s2_ref, w_in_ref, lvg_ref, lvb_ref, wmix_ref, bmix_ref, cw_ref, w_out_ref,
         g1_ref, b1_ref, o_ref, hc_ref, v_ref, y_scr) = refs
    else:
        (x_ref, w_in_ref, lvg_ref, lvb_ref, wmix_ref, bmix_ref, cw_ref, w_out_ref,
         g1_ref, b1_ref, o_ref, cb_ref, y_scr, h_scr) = refs

    if not sample:
        @pl.when(pl.program_id(1) == 0)
        def _():
            h_scr[0:8, :] = jnp.zeros((8, W_B), F32)

    def part(r0, n):
        rows_p = slice(r0, r0 + n)
        x = x_ref[rows_p, :]
        xb = x.astype(BF16)

        def proj(k):
            return jnp.dot(xb, w_in_ref[:, k * W_A:(k + 1) * W_A], preferred_element_type=F32)

        u = _gelu(proj(0))
        v = _layer_norm(_gelu(proj(1)), lvg_ref[...], lvb_ref[...])
        if sample:
            v_ref[...] = v
        vb = v.astype(BF16)
        for c in range(n // chunk):
            rows = slice(c * chunk, (c + 1) * chunk)
            for g in range(G_A):
                cols = slice(g * 128, (g + 1) * 128)
                y_scr[r0 + c * chunk:r0 + (c + 1) * chunk, cols] = (
                    jnp.dot(wmix_ref[g], vb[rows, cols], preferred_element_type=F32) + bmix_ref[:, cols])
        a_out = (u * y_scr[rows_p, :]).astype(BF16)

        hc = proj(3) * proj(4)
        if sample:
            hc_ref[...] = hc
            pos = lax.broadcasted_iota(jnp.int32, hc.shape, 0) & 7
            sh1 = jnp.where(pos == 0, s1_ref[...], pltpu.roll(hc, 1, 0))
            sh2 = jnp.where(pos < 2, s2_ref[...], pltpu.roll(hc, 2, 0))
        else:
            h_scr[8 + r0:8 + r0 + n, :] = hc
            sh1 = h_scr[7 + r0:7 + r0 + n, :]
            sh2 = h_scr[6 + r0:6 + r0 + n, :]
        conv = cw_ref[0:1, :] * sh2 + cw_ref[1:2, :] * sh1 + cw_ref[2:3, :] * hc
        b_out = (proj(2) * conv).astype(BF16)

        mix = (jnp.dot(a_out, w_out_ref[0:W_A, :], preferred_element_type=F32)
               + jnp.dot(b_out, w_out_ref[W_A:W_A + W_B, :], preferred_element_type=F32))
        o_ref[rows_p, :] = _layer_norm(ALPHA * x + mix, g1_ref[...], b1_ref[...])

    for p in range(n_parts):
        part(p * (tm // n_parts), tm // n_parts)
    if not sample:
        cb_ref[...] = h_scr[tm + 6:tm + 8, :]
        h_scr[0:8, :] = h_scr[tm:tm + 8, :]


def _ab_layer_prompt(x, batch, seq, weights):
    tm = ROW_TILE
    nt = seq // tm
    row_spec = pl.BlockSpec((tm, D_MODEL), lambda b, t: (b * nt + t, 0))
    weights, w_specs = _resident_all(weights)
    return pl.pallas_call(
        functools.partial(_ab_kernel, tm=tm, chunk=CHUNK, sample=False, n_parts=AB_PARTS),
        grid=(batch, nt),
        in_specs=[row_spec] + w_specs,
        out_specs=[row_spec, pl.BlockSpec((None, 2, W_B), lambda b, t: (b, 0, 0))],
        out_shape=[jax.ShapeDtypeStruct((batch * seq, D_MODEL), F32),
                   jax.ShapeDtypeStruct((batch, 2, W_B), F32)],
        scratch_shapes=[pltpu.VMEM((tm, W_A), F32), pltpu.VMEM((tm + 8, W_B), F32)],
        compiler_params=_params(2),
        name="ab_layer_prompt",
    )(x, *weights)


def _ab_layer_sample(x, s1, s2, weights):
    tm = x.shape[0]
    full = lambda cols: pl.BlockSpec((tm, cols), lambda i: (0, 0))
    weights, w_specs = _resident_all(weights)
    return pl.pallas_call(
        functools.partial(_ab_kernel, tm=tm, chunk=tm, sample=True, n_parts=1),
        grid=(1,),
        in_specs=[full(D_MODEL), full(W_B), full(W_B)] + w_specs,
        out_specs=[full(D_MODEL), full(W_B), full(W_A)],
        out_shape=[jax.ShapeDtypeStruct((tm, D_MODEL), F32),
                   jax.ShapeDtypeStruct((tm, W_B), F32),
                   jax.ShapeDtypeStruct((tm, W_A), F32)],
        scratch_shapes=[pltpu.VMEM((tm, W_A), F32)],
        compiler_params=_params(1),
        name="ab_layer_sample",
    )(x, s1, s2, *weights)


def _mlp_kernel(x_ref, w_up_ref, w_down_ref, g_ref, b_ref, o_ref):
    x = x_ref[...]
    xb = x.astype(BF16)
    acc = None
    for c in range(D_FF // D_MODEL):
        cols = slice(c * D_MODEL, (c + 1) * D_MODEL)
        h = jnp.maximum(jnp.dot(xb, w_up_ref[:, cols], preferred_element_type=F32), 0.0)
        part = jnp.dot((h * h).astype(BF16), w_down_ref[cols, :], preferred_element_type=F32)
        acc = part if acc is None else acc + part
    o_ref[...] = _layer_norm(ALPHA * x + acc, g_ref[...], b_ref[...])


def _mlp_layer(x, w_up, w_down, g, b):
    rows = x.shape[0]
    tm = min(ROW_TILE, rows)
    row_spec = pl.BlockSpec((tm, D_MODEL), lambda t: (t, 0))
    weights, w_specs = _resident_all((w_up, w_down, g, b))
    return pl.pallas_call(
        _mlp_kernel,
        grid=(rows // tm,),
        in_specs=[row_spec] + w_specs,
        out_specs=row_spec,
        out_shape=jax.ShapeDtypeStruct((rows, D_MODEL), F32),
        compiler_params=_params(1),
        name="mlp_layer",
    )(x, *weights)


def _qkv_kernel(x_ref, w_ref, o_ref):
    xb = x_ref[...].astype(BF16)
    for c in range(QKV_BLOCKS):
        cols = slice(c * C_WIDTH, (c + 1) * C_WIDTH)
        o_ref[:, cols] = jnp.dot(xb, w_ref[:, cols], preferred_element_type=F32)


def _qkv_proj(x, w):
    rows = x.shape[0]
    tm = min(256, rows)
    w, w_spec = _resident(w)
    return pl.pallas_call(
        _qkv_kernel,
        grid=(rows // tm,),
        in_specs=[pl.BlockSpec((tm, D_MODEL), lambda t: (t, 0)), w_spec],
        out_specs=pl.BlockSpec((tm, 3 * QKV_COLS), lambda t: (t, 0)),
        out_shape=jax.ShapeDtypeStruct((rows, 3 * QKV_COLS), F32),
        compiler_params=_params(1),
        name="qkv_proj",
    )(x, w)


def _qkv_perm_kernel(*refs):
    x_refs, (wk_ref, wqv_t_ref, k_ref, qt_ref, vt_ref, xp_scr) = refs[:X_SLABS], refs[X_SLABS:]
    nt_dims = (((1,), (1,)), ((), ()))

    def step(d):
        n = PERM_TILE // d
        for c in range(PERM_TILE // ROW_TILE):
            rows = slice(c * ROW_TILE, (c + 1) * ROW_TILE)
            for r in range(d):
                lo, hi = max(r * n, rows.start), min((r + 1) * n, rows.stop)
                if lo >= hi:
                    continue
                for s in range(X_SLABS):
                    src = x_refs[s]
                    vals = (src[lo:hi, :] if d == 1 else src[pl.ds(r + (lo - r * n) * d, hi - lo, stride=d), :])
                    xp_scr[lo:hi, s * 128:(s + 1) * 128] = vals.astype(BF16)
            xs = xp_scr[rows, :]
            k_ref[rows, :] = jnp.dot(xs, wk_ref[...], preferred_element_type=F32).astype(BF16)
            qv_t = lax.dot_general(wqv_t_ref[...], xs, nt_dims, preferred_element_type=F32)
            qt_ref[:, rows] = (qv_t[0:C_WIDTH] * QK_SCALE).astype(BF16)
            vt_ref[:, rows] = qv_t[C_WIDTH:2 * C_WIDTH].astype(BF16)

    for g in range(N_GROUPS):
        pl.when(pl.program_id(2) == g)(functools.partial(step, DILATIONS[g]))


def _qkv_proj_perm(x, wk_groups, wqv_t_groups, layer_idx, batch, seq):
    nt = seq // PERM_TILE
    slab = lambda s: pl.BlockSpec((PERM_TILE, 128), lambda b, t, g: (b * nt + t, s))
    chan_spec = pl.BlockSpec((None, C_WIDTH, PERM_TILE), lambda b, t, g: (g, 0, b * nt + t))
    chan_shape = jax.ShapeDtypeStruct((N_GROUPS, C_WIDTH, batch * seq), BF16)
    return pl.pallas_call(
        _qkv_perm_kernel,
        grid=(batch, nt, N_GROUPS),
        in_specs=[slab(s) for s in range(X_SLABS)]
                 + [pl.BlockSpec((None, None, D_MODEL, C_WIDTH), lambda b, t, g: (layer_idx, g, 0, 0)),
                    pl.BlockSpec((None, None, 2 * C_WIDTH, D_MODEL), lambda b, t, g: (layer_idx, g, 0, 0))],
        out_specs=[pl.BlockSpec((None, PERM_TILE, C_WIDTH), lambda b, t, g: (g, b * nt + t, 0)),
                   chan_spec, chan_spec],
        out_shape=[jax.ShapeDtypeStruct((N_GROUPS, batch * seq, C_WIDTH), BF16), chan_shape, chan_shape],
        scratch_shapes=[pltpu.VMEM((PERM_TILE, D_MODEL), BF16)],
        compiler_params=_params(3),
        name="qkv_proj_perm",
    )(*([x] * X_SLABS), wk_groups, wqv_t_groups)


def _kv_tail_kernel(x_ref, wt_ref, o0_ref, o1_ref, o2_ref):
    xb = x_ref[...].astype(BF16)
    nt = (((1,), (1,)), ((), ()))
    o2_ref[...] = lax.dot_general(wt_ref[2], xb, nt, preferred_element_type=F32)

    @pl.when(pl.program_id(1) == pl.num_programs(1) - 1)
    def _():
        o1_ref[...] = lax.dot_general(wt_ref[1], xb, nt, preferred_element_type=F32)
        o0_ref[...] = lax.dot_general(wt_ref[0], xb[ROW_TILE - WINDOWS[0]:, :], nt, preferred_element_type=F32)


def _kv_tail(x, wt_groups, batch, seq):
    nj = WINDOWS[2] // ROW_TILE
    first = (seq - WINDOWS[2]) // ROW_TILE
    nblk = seq // ROW_TILE
    last = lambda n: pl.BlockSpec((None, 2 * C_WIDTH, n), lambda b, j: (b, 0, 0))
    wt_groups, wt_spec = _resident(wt_groups)
    return pl.pallas_call(
        _kv_tail_kernel,
        grid=(batch, nj),
        in_specs=[pl.BlockSpec((ROW_TILE, D_MODEL), lambda b, j: (b * nblk + first + j, 0)), wt_spec],
        out_specs=[last(WINDOWS[0]), last(WINDOWS[1]),
                   pl.BlockSpec((None, 2 * C_WIDTH, ROW_TILE), lambda b, j: (b, 0, j))],
        out_shape=[jax.ShapeDtypeStruct((batch, 2 * C_WIDTH, n), F32) for n in WINDOWS],
        compiler_params=_params(2),
        name="kv_tail",
    )(x, wt_groups)


def _attn_kernel(qt_ref, kc_ref, kp_ref, vtc_ref, vtp_ref, o_ref, lse_ref, bias_scr, *, dil, chain):
    first_step = (pl.program_id(0) == 0) & (pl.program_id(1) == 0) & (pl.program_id(2) == 0)

    @pl.when(first_step)
    def _():
        kj = lax.broadcasted_iota(jnp.int32, (CHUNK, CHUNK), 0)
        qi = lax.broadcasted_iota(jnp.int32, (CHUNK, CHUNK), 1)
        back_own = qi - kj
        back_prev = back_own + CHUNK
        for h in range(H_G):
            slope = SLOPES[h] * dil
            bias_scr[0, h] = jnp.where(back_prev <= N_BACK, -slope * back_prev.astype(F32), NEG_INF)
            bias_scr[1, h] = jnp.full((CHUNK, CHUNK), NEG_INF, F32)
            bias_scr[2, h] = jnp.where(back_own >= 0, -slope * back_own.astype(F32), NEG_INF)

    n_sub = ROW_TILE // CHUNK
    if chain:
        run_steps = PERM_TILE // dil // ROW_TILE
        at_start = (pl.program_id(1) == 0) & (pl.program_id(2) % run_steps == 0)
    else:
        at_start = pl.program_id(1) == 0
    zeros_half = jnp.zeros((HEAD_DIM, CHUNK), BF16)
    lse_pad = jnp.zeros((CHUNK - H_G, CHUNK), F32)

    def half_tile(k_rows, qt_h, vt_cols, bias):
        st = jnp.dot(k_rows, qt_h, preferred_element_type=F32) + bias
        m = jnp.max(st, axis=0, keepdims=True)
        p = jnp.exp(st - m)
        l = jnp.sum(p, axis=0, keepdims=True)
        return m, l, jnp.dot(vt_cols, p.astype(BF16), preferred_element_type=F32)

    for s in range(n_sub):
        own = slice(s * CHUNK, (s + 1) * CHUNK)
        if chain and s > 0:
            prev_k, prev_vt = kc_ref, vtc_ref
            prv = slice((s - 1) * CHUNK, s * CHUNK)
        else:
            prev_k, prev_vt = kp_ref, vtp_ref
            prv = own if not chain else slice(0, CHUNK)
        masked = at_start if (s == 0 or not chain) else None
        lses = []
        for pair in range(H_G // 2):
            lanes = slice(pair * 2 * HEAD_DIM, (pair + 1) * 2 * HEAD_DIM)
            qt_pair = qt_ref[lanes, own]
            outs = []
            for half in range(2):
                h = 2 * pair + half
                chans = slice(h * HEAD_DIM, (h + 1) * HEAD_DIM)
                qt_h = (jnp.concatenate([qt_pair[0:HEAD_DIM], zeros_half], axis=0) if half == 0
                        else jnp.concatenate([zeros_half, qt_pair[HEAD_DIM:]], axis=0))
                bias_prev = bias_scr[0, h] if masked is None else bias_scr[jnp.where(masked, 1, 0), h]
                m_a, l_a, o_a = half_tile(prev_k[prv, lanes], qt_h, prev_vt[chans, prv], bias_prev)
                m_b, l_b, o_b = half_tile(kc_ref[own, lanes], qt_h, vtc_ref[chans, own], bias_scr[2, h])
                m = jnp.maximum(m_a, m_b)
                w_a, w_b = jnp.exp(m_a - m), jnp.exp(m_b - m)
                l = w_a * l_a + w_b * l_b
                outs.append((w_a * o_a + w_b * o_b) * (1.0 / l))
                lses.append(m + jnp.log(l))
            o_ref[own, lanes] = jnp.concatenate(outs, axis=0).T
        lse_ref[own, :] = jnp.concatenate(lses + [lse_pad], axis=0).T


def _attn_prompt(k_all, qt_all, vt_all, batch, seq, g):
    d = DILATIONS[g]
    nt = seq // PERM_TILE
    nj = PERM_TILE // ROW_TILE
    sub_per_step = ROW_TILE // CHUNK
    run = PERM_TILE // d
    chain = run > CHUNK
    assert run % ROW_TILE == 0 if chain else run == CHUNK
    cur = lambda b, t, j: (b * nt + t) * nj + j
    if chain:
        back = (PERM_TILE - run) // CHUNK + 1
        prev_rows = CHUNK
        prev = lambda b, t, j: jnp.maximum(cur(b, t, j) * sub_per_step - back, 0)
    else:
        prev_rows = ROW_TILE
        prev = lambda b, t, j: jnp.maximum(cur(b, t, j) - nj, 0)
    tok = lambda rows, idx: pl.BlockSpec((None, rows, C_WIDTH), lambda b, t, j: (g, idx(b, t, j), 0))
    chn = lambda cols, idx: pl.BlockSpec((None, C_WIDTH, cols), lambda b, t, j: (g, 0, idx(b, t, j)))
    return pl.pallas_call(
        functools.partial(_attn_kernel, dil=d, chain=chain),
        grid=(batch, nt, nj),
        in_specs=[chn(ROW_TILE, cur), tok(ROW_TILE, cur), tok(prev_rows, prev),
                  chn(ROW_TILE, cur), chn(prev_rows, prev)],
        out_specs=[pl.BlockSpec((ROW_TILE, C_WIDTH), lambda b, t, j: (cur(b, t, j), 0)),
                   pl.BlockSpec((ROW_TILE, LSE_LANES), lambda b, t, j: (cur(b, t, j), 0))],
        out_shape=[jax.ShapeDtypeStruct((batch * seq, C_WIDTH), F32),
                   jax.ShapeDtypeStruct((batch * seq, LSE_LANES), F32)],
        scratch_shapes=[pltpu.VMEM((3, H_G, CHUNK, CHUNK), F32)],
        compiler_params=_params(3),
        name=f"attn_prompt_d{d}",
    )(qt_all, k_all, k_all, vt_all, vt_all)


def _sattn_kernel(*refs, buf_len, dil, t_new):
    q_ref, kn_ref, vn_ref, cache_ref = refs[:4]
    o_ref, lse_ref, newc_ref, bias_c, bias_n = refs[-5:]
    new_lane0 = CHUNK - t_new

    @pl.when(pl.program_id(0) == 0)
    def _():
        def bias(dist, ok, h):
            b0 = jnp.where((dist & (dil - 1)) == 0, -SLOPES[h] * dist.astype(F32), NEG_INF)
            b0 = jnp.where(dist >= 0, b0, NEG_INF)
            b0 = jnp.where(dist <= N_BACK * dil, b0, NEG_INF)
            return b0 if ok is None else jnp.where(ok, b0, NEG_INF)
        t_c = lax.broadcasted_iota(jnp.int32, (t_new, buf_len), 0)
        p_c = lax.broadcasted_iota(jnp.int32, (t_new, buf_len), 1)
        t_n = lax.broadcasted_iota(jnp.int32, (t_new, CHUNK), 0)
        j_n = lax.broadcasted_iota(jnp.int32, (t_new, CHUNK), 1)
        for h in range(H_G):
            bias_c[h] = bias(buf_len + t_c - p_c, None, h)
            bias_n[h] = bias(t_n - (j_n - new_lane0), j_n >= new_lane0, h)

    kn = kn_ref[...]
    vn = vn_ref[...]
    pad = jnp.zeros((new_lane0, C_WIDTH), F32)
    kn_t = jnp.concatenate([pad, kn], axis=0).T
    vn_t = jnp.concatenate([pad, vn], axis=0).T
    lane = lax.broadcasted_iota(jnp.int32, (HEAD_DIM, CHUNK), 1)
    nt = (((1,), (1,)), ((), ()))

    qb = (q_ref[...] * QK_SCALE).astype(BF16)
    outs, lses = [], []
    for h in range(H_G):
        cols = slice(h * HEAD_DIM, (h + 1) * HEAD_DIM)
        k_t, v_t = cache_ref[0, h], cache_ref[1, h]
        knh_t, vnh_t = kn_t[cols], vn_t[cols]
        qh = qb[:, cols]
        lc = jnp.dot(qh, k_t.astype(BF16), preferred_element_type=F32) + bias_c[h]
        ln = jnp.dot(qh, knh_t.astype(BF16), preferred_element_type=F32) + bias_n[h]
        m = jnp.maximum(jnp.max(lc, axis=-1, keepdims=True), jnp.max(ln, axis=-1, keepdims=True))
        pc = jnp.exp(lc - m)
        pn = jnp.exp(ln - m)
        l = jnp.sum(pc, axis=-1, keepdims=True) + jnp.sum(pn, axis=-1, keepdims=True)
        o = (lax.dot_general(pc.astype(BF16), v_t.astype(BF16), nt, preferred_element_type=F32)
             + lax.dot_general(pn.astype(BF16), vnh_t.astype(BF16), nt, preferred_element_type=F32))
        outs.append(o / l)
        lses.append(jnp.broadcast_to(m + jnp.log(l), (t_new, HEAD_DIM)))
        for kv, old, new in ((0, k_t, knh_t), (1, v_t, vnh_t)):
            rolled = pltpu.roll(old, buf_len - t_new, 1)
            if buf_len > CHUNK:
                newc_ref[kv, h, :, 0:buf_len - CHUNK] = rolled[:, 0:buf_len - CHUNK]
            newc_ref[kv, h, :, buf_len - CHUNK:buf_len] = jnp.where(
                lane >= new_lane0, new, rolled[:, buf_len - CHUNK:buf_len])
    o_ref[...] = jnp.concatenate(outs, axis=-1)
    lse_ref[...] = jnp.concatenate(lses, axis=-1)


def _attn_sample(qkv, cache, prev_out, layer_idx, g, t_new):
    n_layers, n_seq, _, _, _, buf_len = cache.shape
    d = DILATIONS[g]
    row = lambda sec: pl.BlockSpec((t_new, C_WIDTH), lambda b: (b, sec * N_GROUPS + g))
    cache_spec = pl.BlockSpec((None, None, 2, H_G, HEAD_DIM, buf_len), lambda b: (layer_idx, b, 0, 0, 0, 0))
    out_row = pl.BlockSpec((t_new, C_WIDTH), lambda b: (b, 0))
    in_specs = [row(0), row(1), row(2), cache_spec]
    args = [qkv, qkv, qkv, cache]
    aliases = {}
    if prev_out is not None:
        in_specs.append(pl.BlockSpec(memory_space=pl.ANY))
        args.append(prev_out)
        aliases = {4: 2}
    return pl.pallas_call(
        functools.partial(_sattn_kernel, buf_len=buf_len, dil=d, t_new=t_new),
        grid=(n_seq,),
        in_specs=in_specs,
        out_specs=[out_row, out_row, cache_spec],
        out_shape=[jax.ShapeDtypeStruct((n_seq * t_new, C_WIDTH), F32)] * 2
                  + [jax.ShapeDtypeStruct(cache.shape, F32)],
        scratch_shapes=[pltpu.VMEM((H_G, t_new, buf_len), F32), pltpu.VMEM((H_G, t_new, CHUNK), F32)],
        input_output_aliases=aliases,
        compiler_params=_params(1),
        name=f"attn_sample_d{d}",
    )(*args)


def _merge_kernel(*refs, tm, dils):
    x_ref = refs[0]
    ol_refs = refs[1:1 + 2 * N_GROUPS]
    w_ref, g_ref, b_ref, out_ref = refs[1 + 2 * N_GROUPS:5 + 2 * N_GROUPS]
    scrs = list(refs[5 + 2 * N_GROUPS:])

    def natural(ref, d):
        if d == 1:
            return ref[...]
        scr = scrs.pop()
        n = tm // d
        slabs = ref.shape[-1] // 128
        for r in range(d):
            for s in range(slabs):
                scr[s, pl.ds(r, n, stride=d), :] = ref[r, :, s * 128:(s + 1) * 128]
        return jnp.concatenate([scr[s] for s in range(slabs)], axis=-1)

    vals = [natural(ol_refs[k], dils[k % N_GROUPS]) for k in range(2 * N_GROUPS)]
    (o0, o1, o2), (l0, l1, l2) = vals[:N_GROUPS], vals[N_GROUPS:]
    m = jnp.maximum(jnp.maximum(l0, l1), l2)
    e0, e1, e2 = jnp.exp(l0 - m), jnp.exp(l1 - m), jnp.exp(l2 - m)
    inv = 1.0 / (e0 + e1 + e2)
    alphas = [e0 * inv, e1 * inv, e2 * inv]
    if l0.shape[-1] != C_WIDTH:
        lane = lax.broadcasted_iota(jnp.int32, (2 * LSE_LANES, C_WIDTH), 0) & (LSE_LANES - 1)
        chan = lax.broadcasted_iota(jnp.int32, (2 * LSE_LANES, C_WIDTH), 1)
        spread = jnp.where((chan >> (HEAD_DIM.bit_length() - 1)) == lane, 1.0, 0.0).astype(BF16)

        def expand(a):
            hi = a.astype(BF16)
            lo = (a - hi.astype(F32)).astype(BF16)
            return jnp.dot(jnp.concatenate([hi, lo], axis=-1), spread, preferred_element_type=F32)
        alphas = [expand(a) for a in alphas]
    o = alphas[0] * o0 + alphas[1] * o1 + alphas[2] * o2
    x = x_ref[...]
    mix = jnp.dot(o.astype(BF16), w_ref[...], preferred_element_type=F32)
    out_ref[...] = _layer_norm(ALPHA * x + mix, g_ref[...], b_ref[...])


def _merge_layer(x, outs, lses, w_out, g, b, dils):
    rows = x.shape[0]
    tm = min(MERGE_TILE, rows)
    per_tile = PERM_TILE // tm
    row_spec = lambda cols: pl.BlockSpec((tm, cols), lambda t: (t, 0))

    def group_arg(a, d):
        cols = a.shape[-1]
        if d == 1:
            return a, row_spec(cols), None
        run = PERM_TILE // d
        view = a.reshape(rows // PERM_TILE, d, run, cols)
        spec = pl.BlockSpec((None, d, tm // d, cols), lambda t: (t // per_tile, 0, t % per_tile, 0))
        return view, spec, pltpu.VMEM((cols // 128, tm, 128), F32)

    args, specs, scratch = zip(*[group_arg(a, dils[k % N_GROUPS]) for k, a in enumerate(list(outs) + list(lses))])
    scratch = [s for s in scratch if s is not None][::-1]
    weights, w_specs = _resident_all((w_out, g, b))
    return pl.pallas_call(
        functools.partial(_merge_kernel, tm=tm, dils=dils),
        grid=(rows // tm,),
        in_specs=[row_spec(D_MODEL)] + list(specs) + w_specs,
        out_specs=row_spec(D_MODEL),
        out_shape=jax.ShapeDtypeStruct((rows, D_MODEL), F32),
        scratch_shapes=scratch,
        compiler_params=_params(1),
        name="merge_layer",
    )(x, *args, *weights)


def kernel(x_prompt, x_sample, state_conv, cache_kv_w128, cache_kv_w512, cache_kv_w2048, w_in_ab, ln_v_g, ln_v_b, w_spatial, b_spatial, conv_w, w_out_ab, w_qkv_c, w_out_c, ln1_g, ln1_b, ln2_g, ln2_b, w_mlp_up, w_mlp_down):
    batch, seq, _ = x_prompt.shape
    n_seq, t_new, _ = x_sample.shape
    n_tok_s = n_seq * t_new
    xp = x_prompt.reshape(batch * seq, D_MODEL)
    xs = x_sample.reshape(n_tok_s, D_MODEL)
    caches = [jnp.transpose(c, (0, 1, 3, 4, 5, 2)) for c in (cache_kv_w128, cache_kv_w512, cache_kv_w2048)]
    new_caches = [None] * N_GROUPS
    row = lambda a: a.reshape(1, -1)

    causal = jnp.tril(jnp.ones((CHUNK, CHUNK), dtype=bool))
    w_in_b, w_out_ab_b = w_in_ab.astype(BF16), w_out_ab.astype(BF16)
    w_up_b, w_down_b = w_mlp_up.astype(BF16), w_mlp_down.astype(BF16)
    w_qkv_b, w_out_c_b = w_qkv_c.astype(BF16), w_out_c.astype(BF16)
    n_c = w_qkv_b.shape[0]
    wq5 = w_qkv_b.reshape(n_c, D_MODEL, 3, N_GROUPS, C_WIDTH)
    wk_groups = jnp.transpose(wq5[:, :, 1], (0, 2, 1, 3))
    wqv_t_groups = jnp.transpose(wq5[:, :, 0::2], (0, 3, 2, 4, 1)).reshape(n_c, N_GROUPS, 2 * C_WIDTH, D_MODEL)
    wkv_t_groups = jnp.transpose(wq5[:, :, 1:], (0, 3, 2, 4, 1)).reshape(n_c, N_GROUPS, 2 * C_WIDTH, D_MODEL)
    conv_p, conv_s, chunk_v_s = [], [], []
    kv_p = [[] for _ in range(N_GROUPS)]

    for layer in range(DEPTH):
        i = layer // 2
        if layer % 2 == 0:
            w_tril = jnp.where(causal[None], w_spatial[i], 0.0)
            bmix = jnp.repeat(b_spatial[i].T, CHUNK, axis=1)
            shared = (_Layer(w_in_b, i), row(ln_v_g[i]), row(ln_v_b[i]))
            tail = (conv_w[i], _Layer(w_out_ab_b, i), row(ln1_g[layer]), row(ln1_b[layer]))
            xp, buf_p = _ab_layer_prompt(xp, batch, seq,
                                         shared + (w_tril.astype(BF16), bmix) + tail)
            eye = jnp.eye(n_seq, dtype=F32)
            w_blk = jnp.stack([jnp.kron(eye, w_tril[g, :t_new, :t_new]) for g in range(G_A)])
            bmix_s = jnp.tile(bmix[:t_new], (n_seq, 1))
            st = state_conv[i]
            s1 = jnp.concatenate([st[:, 1:2], jnp.zeros((n_seq, t_new - 1, W_B), F32)], axis=1)
            s2 = jnp.concatenate([st, jnp.zeros((n_seq, t_new - 2, W_B), F32)], axis=1)
            xs, hc_s, v_s = _ab_layer_sample(xs, s1.reshape(n_tok_s, W_B), s2.reshape(n_tok_s, W_B),
                                             shared + (w_blk.astype(BF16), bmix_s) + tail)
            conv_p.append(buf_p)
            conv_s.append(hc_s.reshape(n_seq, t_new, W_B)[:, t_new - 2:])
            chunk_v_s.append(v_s.reshape(n_seq, t_new, W_A))
        else:
            wo = _Layer(w_out_c_b, i)
            g1, b1 = row(ln1_g[layer]), row(ln1_b[layer])
            tails = _kv_tail(xp, _Layer(wkv_t_groups, i), batch, seq)
            for g in range(N_GROUPS):
                kv_p[g].append(tails[g].reshape(batch, 2, H_G, HEAD_DIM, WINDOWS[g]))
            k_all, qt_all, vt_all = _qkv_proj_perm(xp, wk_groups, wqv_t_groups, i, batch, seq)
            outs, lses = zip(*[_attn_prompt(k_all, qt_all, vt_all, batch, seq, g) for g in range(N_GROUPS)])
            xp = _merge_layer(xp, outs, lses, wo, g1, b1, DILATIONS)
            qkv_s = _qkv_proj(xs, _Layer(w_qkv_b, i))
            outs, lses = [], []
            for g in range(N_GROUPS):
                o, lse, new_caches[g] = _attn_sample(qkv_s, caches[g], new_caches[g], i, g, t_new)
                outs.append(o)
                lses.append(lse)
            xs = _merge_layer(xs, outs, lses, wo, g1, b1, (1,) * N_GROUPS)
        w_up, w_down = _Layer(w_up_b, layer), _Layer(w_down_b, layer)
        g2, b2 = row(ln2_g[layer]), row(ln2_b[layer])
        xp = _mlp_layer(xp, w_up, w_down, g2, b2)
        xs = _mlp_layer(xs, w_up, w_down, g2, b2)

    kv_s = [jnp.transpose(nc, (0, 1, 5, 2, 3, 4)) for nc in new_caches]
    return (xp.reshape(batch, seq, D_MODEL), xs.reshape(n_seq, t_new, D_MODEL),
            jnp.stack(conv_p), jnp.stack(conv_s), jnp.stack(chunk_v_s),
            *[jnp.transpose(jnp.stack(kv_p[g]), (0, 1, 5, 2, 3, 4)) for g in range(N_GROUPS)],
            kv_s[0], kv_s[1], kv_s[2])
```

```python
import functools
import math
from typing import NamedTuple

import jax
import jax.numpy as jnp
from jax import lax
from jax.experimental import pallas as pl
from jax.experimental.pallas import tpu as pltpu

F32 = jnp.float32
BF16 = jnp.bfloat16

D_MODEL = 1024
DEPTH = 4
CHUNK = 128
W_A = 512
G_A = 4
W_B = 512
N_GROUPS = 3
WINDOWS = (128, 512, 2048)
DILATIONS = (1, 4, 16)
N_BACK = 128
H_G = 8
HEAD_DIM = 64
C_WIDTH = H_G * HEAD_DIM
QKV_COLS = N_GROUPS * C_WIDTH
QKV_BLOCKS = 3 * N_GROUPS
D_FF = 4 * D_MODEL
ALPHA = (2.0 * DEPTH) ** 0.25
LN_EPS = 1e-5
NEG_INF = -1e30
SLOPES = tuple(2.0 ** (-(8.0 / H_G) * j) for j in range(1, H_G + 1))
QK_SCALE = HEAD_DIM ** -0.5
LOG2E = math.log2(math.e)
LN2 = math.log(2.0)

VMEM_LIMIT_BYTES = 52 * 1024 * 1024
ROW_TILE = 512
PERM_TILE = CHUNK * max(DILATIONS)
X_SLABS = D_MODEL // 128
MERGE_TILE = 512
AB_PARTS = 1
SAMPLE_STEP_POSITIONS = 1024
LSE_LANES = 128


def _layer_norm(x, g, b):
    mu = jnp.mean(x, axis=-1, keepdims=True)
    xc = x - mu
    var = jnp.mean(xc * xc, axis=-1, keepdims=True)
    return xc * lax.rsqrt(var + LN_EPS) * g + b


def _gelu(x):
    c = math.sqrt(2.0 / math.pi)
    return x * (0.5 * (1.0 + jnp.tanh(c * (x + 0.044715 * (x * x * x)))))


class _Layer(NamedTuple):
    stacked: jax.Array
    index: int


def _resident(w):
    if isinstance(w, _Layer):
        shape = w.stacked.shape[1:]
        return w.stacked, pl.BlockSpec((None,) + shape, lambda *_: (w.index,) + (0,) * len(shape),
                                       pipeline_mode=pl.Buffered(1))
    return w, pl.BlockSpec(w.shape, lambda *_: (0,) * w.ndim, pipeline_mode=pl.Buffered(1))


def _resident_all(ws):
    arrays, specs = zip(*[_resident(w) for w in ws])
    return list(arrays), list(specs)


def _params(n_grid):
    return pltpu.CompilerParams(dimension_semantics=("arbitrary",) * n_grid,
                                vmem_limit_bytes=VMEM_LIMIT_BYTES)


def _ab_kernel(*refs, tm, chunk, sample, n_parts):
    if sample:
        (x_ref, s1_ref, s2_ref, w_in_ref, lvg_ref, lvb_ref, wmix_ref, bmix_ref, cw_ref, w_out_ref,
         g1_ref, b1_ref, o_ref, hc_ref, v_ref, y_scr) = refs
    else:
        (x_ref, w_in_ref, lvg_ref, lvb_ref, wmix_ref, bmix_ref, cw_ref, w_out_ref,
         g1_ref, b1_ref, o_ref, cb_ref, y_scr, h_scr) = refs

    if not sample:
        @pl.when(pl.program_id(1) == 0)
        def _():
            h_scr[0:8, :] = jnp.zeros((8, W_B), F32)

    def part(r0, n):
        rows_p = slice(r0, r0 + n)
        x = x_ref[rows_p, :]
        xb = x.astype(BF16)

        def proj(k):
            return jnp.dot(xb, w_in_ref[:, k * W_A:(k + 1) * W_A], preferred_element_type=F32)

        u = _gelu(proj(0))
        v = _layer_norm(_gelu(proj(1)), lvg_ref[...], lvb_ref[...])
        if sample:
            v_ref[...] = v
        vb = v.astype(BF16)
        for c in range(n // chunk):
            rows = slice(c * chunk, (c + 1) * chunk)
            for g in range(G_A):
                cols = slice(g * 128, (g + 1) * 128)
                y_scr[r0 + c * chunk:r0 + (c + 1) * chunk, cols] = (
                    jnp.dot(wmix_ref[g], vb[rows, cols], preferred_element_type=F32) + bmix_ref[:, cols])
        a_out = (u * y_scr[rows_p, :]).astype(BF16)

        hc = proj(3) * proj(4)
        if sample:
            hc_ref[...] = hc
            pos = lax.broadcasted_iota(jnp.int32, hc.shape, 0) & 7
            sh1 = jnp.where(pos == 0, s1_ref[...], pltpu.roll(hc, 1, 0))
            sh2 = jnp.where(pos < 2, s2_ref[...], pltpu.roll(hc, 2, 0))
        else:
            h_scr[8 + r0:8 + r0 + n, :] = hc
            sh1 = h_scr[7 + r0:7 + r0 + n, :]
            sh2 = h_scr[6 + r0:6 + r0 + n, :]
        conv = cw_ref[0:1, :] * sh2 + cw_ref[1:2, :] * sh1 + cw_ref[2:3, :] * hc
        b_out = (proj(2) * conv).astype(BF16)

        mix = (jnp.dot(a_out, w_out_ref[0:W_A, :], preferred_element_type=F32)
               + jnp.dot(b_out, w_out_ref[W_A:W_A + W_B, :], preferred_element_type=F32))
        o_ref[rows_p, :] = _layer_norm(ALPHA * x + mix, g1_ref[...], b1_ref[...])

    for p in range(n_parts):
        part(p * (tm // n_parts), tm // n_parts)
    if not sample:
        cb_ref[...] = h_scr[tm + 6:tm + 8, :]
        h_scr[0:8, :] = h_scr[tm:tm + 8, :]


def _ab_layer_prompt(x, batch, seq, weights):
    tm = ROW_TILE
    nt = seq // tm
    row_spec = pl.BlockSpec((tm, D_MODEL), lambda b, t: (b * nt + t, 0))
    weights, w_specs = _resident_all(weights)
    return pl.pallas_call(
        functools.partial(_ab_kernel, tm=tm, chunk=CHUNK, sample=False, n_parts=AB_PARTS),
        grid=(batch, nt),
        in_specs=[row_spec] + w_specs,
        out_specs=[row_spec, pl.BlockSpec((None, 2, W_B), lambda b, t: (b, 0, 0))],
        out_shape=[jax.ShapeDtypeStruct((batch * seq, D_MODEL), F32),
                   jax.ShapeDtypeStruct((batch, 2, W_B), F32)],
        scratch_shapes=[pltpu.VMEM((tm, W_A), F32), pltpu.VMEM((tm + 8, W_B), F32)],
        compiler_params=_params(2),
        name="ab_layer_prompt",
    )(x, *weights)


def _ab_layer_sample(x, s1, s2, weights):
    tm = x.shape[0]
    full = lambda cols: pl.BlockSpec((tm, cols), lambda i: (0, 0))
    weights, w_specs = _resident_all(weights)
    return pl.pallas_call(
        functools.partial(_ab_kernel, tm=tm, chunk=tm, sample=True, n_parts=1),
        grid=(1,),
        in_specs=[full(D_MODEL), full(W_B), full(W_B)] + w_specs,
        out_specs=[full(D_MODEL), full(W_B), full(W_A)],
        out_shape=[jax.ShapeDtypeStruct((tm, D_MODEL), F32),
                   jax.ShapeDtypeStruct((tm, W_B), F32),
                   jax.ShapeDtypeStruct((tm, W_A), F32)],
        scratch_shapes=[pltpu.VMEM((tm, W_A), F32)],
        compiler_params=_params(1),
        name="ab_layer_sample",
    )(x, s1, s2, *weights)


def _mlp_kernel(x_ref, w_up_ref, w_down_ref, g_ref, b_ref, o_ref):
    x = x_ref[...]
    xb = x.astype(BF16)
    acc = None
    for c in range(D_FF // D_MODEL):
        cols = slice(c * D_MODEL, (c + 1) * D_MODEL)
        h = jnp.maximum(jnp.dot(xb, w_up_ref[:, cols], preferred_element_type=F32), 0.0)
        part = jnp.dot((h * h).astype(BF16), w_down_ref[cols, :], preferred_element_type=F32)
        acc = part if acc is None else acc + part
    o_ref[...] = _layer_norm(ALPHA * x + acc, g_ref[...], b_ref[...])


def _mlp_layer(x, w_up, w_down, g, b):
    rows = x.shape[0]
    tm = min(ROW_TILE, rows)
    row_spec = pl.BlockSpec((tm, D_MODEL), lambda t: (t, 0))
    weights, w_specs = _resident_all((w_up, w_down, g, b))
    return pl.pallas_call(
        _mlp_kernel,
        grid=(rows // tm,),
        in_specs=[row_spec] + w_specs,
        out_specs=row_spec,
        out_shape=jax.ShapeDtypeStruct((rows, D_MODEL), F32),
        compiler_params=_params(1),
        name="mlp_layer",
    )(x, *weights)


def _qkv_kernel(x_ref, w_ref, o_ref):
    xb = x_ref[...].astype(BF16)
    for c in range(QKV_BLOCKS):
        cols = slice(c * C_WIDTH, (c + 1) * C_WIDTH)
        o_ref[:, cols] = jnp.dot(xb, w_ref[:, cols], preferred_element_type=F32)


def _qkv_proj(x, w):
    rows = x.shape[0]
    tm = min(256, rows)
    w, w_spec = _resident(w)
    return pl.pallas_call(
        _qkv_kernel,
        grid=(rows // tm,),
        in_specs=[pl.BlockSpec((tm, D_MODEL), lambda t: (t, 0)), w_spec],
        out_specs=pl.BlockSpec((tm, 3 * QKV_COLS), lambda t: (t, 0)),
        out_shape=jax.ShapeDtypeStruct((rows, 3 * QKV_COLS), F32),
        compiler_params=_params(1),
        name="qkv_proj",
    )(x, w)


def _qkv_perm_kernel(*refs):
    x_refs, (wk_ref, wqv_t_ref, k_ref, qt_ref, vt_ref, xp_scr) = refs[:X_SLABS], refs[X_SLABS:]
    nt_dims = (((1,), (1,)), ((), ()))

    def step(d):
        n = PERM_TILE // d
        for c in range(PERM_TILE // ROW_TILE):
            rows = slice(c * ROW_TILE, (c + 1) * ROW_TILE)
            for r in range(d):
                lo, hi = max(r * n, rows.start), min((r + 1) * n, rows.stop)
                if lo >= hi:
                    continue
                for s in range(X_SLABS):
                    src = x_refs[s]
                    vals = (src[lo:hi, :] if d == 1 else src[pl.ds(r + (lo - r * n) * d, hi - lo, stride=d), :])
                    xp_scr[lo:hi, s * 128:(s + 1) * 128] = vals.astype(BF16)
            xs = xp_scr[rows, :]
            k_ref[rows, :] = jnp.dot(xs, wk_ref[...], preferred_element_type=F32).astype(BF16)
            qv_t = lax.dot_general(wqv_t_ref[...], xs, nt_dims, preferred_element_type=F32)
            qt_ref[:, rows] = (qv_t[0:C_WIDTH] * (QK_SCALE * LOG2E)).astype(BF16)
            vt_ref[:, rows] = qv_t[C_WIDTH:2 * C_WIDTH].astype(BF16)

    for g in range(N_GROUPS):
        pl.when(pl.program_id(2) == g)(functools.partial(step, DILATIONS[g]))


def _qkv_proj_perm(x, wk_groups, wqv_t_groups, layer_idx, batch, seq):
    nt = seq // PERM_TILE
    slab = lambda s: pl.BlockSpec((PERM_TILE, 128), lambda b, t, g: (b * nt + t, s))
    chan_spec = pl.BlockSpec((None, C_WIDTH, PERM_TILE), lambda b, t, g: (g, 0, b * nt + t))
    chan_shape = jax.ShapeDtypeStruct((N_GROUPS, C_WIDTH, batch * seq), BF16)
    return pl.pallas_call(
        _qkv_perm_kernel,
        grid=(batch, nt, N_GROUPS),
        in_specs=[slab(s) for s in range(X_SLABS)]
                 + [pl.BlockSpec((None, None, D_MODEL, C_WIDTH), lambda b, t, g: (layer_idx, g, 0, 0)),
                    pl.BlockSpec((None, None, 2 * C_WIDTH, D_MODEL), lambda b, t, g: (layer_idx, g, 0, 0))],
        out_specs=[pl.BlockSpec((None, PERM_TILE, C_WIDTH), lambda b, t, g: (g, b * nt + t, 0)),
                   chan_spec, chan_spec],
        out_shape=[jax.ShapeDtypeStruct((N_GROUPS, batch * seq, C_WIDTH), BF16), chan_shape, chan_shape],
        scratch_shapes=[pltpu.VMEM((PERM_TILE, D_MODEL), BF16)],
        compiler_params=_params(3),
        name="qkv_proj_perm",
    )(*([x] * X_SLABS), wk_groups, wqv_t_groups)


def _kv_tail_kernel(x_ref, wt_ref, o0_ref, o1_ref, o2_ref):
    xb = x_ref[...].astype(BF16)
    nt = (((1,), (1,)), ((), ()))
    o2_ref[...] = lax.dot_general(wt_ref[2], xb, nt, preferred_element_type=F32)

    @pl.when(pl.program_id(1) == pl.num_programs(1) - 1)
    def _():
        o1_ref[...] = lax.dot_general(wt_ref[1], xb, nt, preferred_element_type=F32)
        o0_ref[...] = lax.dot_general(wt_ref[0], xb[ROW_TILE - WINDOWS[0]:, :], nt, preferred_element_type=F32)


def _kv_tail(x, wt_groups, batch, seq):
    nj = WINDOWS[2] // ROW_TILE
    first = (seq - WINDOWS[2]) // ROW_TILE
    nblk = seq // ROW_TILE
    last = lambda n: pl.BlockSpec((None, 2 * C_WIDTH, n), lambda b, j: (b, 0, 0))
    wt_groups, wt_spec = _resident(wt_groups)
    return pl.pallas_call(
        _kv_tail_kernel,
        grid=(batch, nj),
        in_specs=[pl.BlockSpec((ROW_TILE, D_MODEL), lambda b, j: (b * nblk + first + j, 0)), wt_spec],
        out_specs=[last(WINDOWS[0]), last(WINDOWS[1]),
                   pl.BlockSpec((None, 2 * C_WIDTH, ROW_TILE), lambda b, j: (b, 0, j))],
        out_shape=[jax.ShapeDtypeStruct((batch, 2 * C_WIDTH, n), F32) for n in WINDOWS],
        compiler_params=_params(2),
        name="kv_tail",
    )(x, wt_groups)


def _attn_kernel(qt_ref, kc_ref, kp_ref, vtc_ref, vtp_ref, o_ref, lse_ref, bias_scr, *, dil, chain):
    first_step = (pl.program_id(0) == 0) & (pl.program_id(1) == 0) & (pl.program_id(2) == 0)

    @pl.when(first_step)
    def _():
        kj = lax.broadcasted_iota(jnp.int32, (CHUNK, CHUNK), 0)
        qi = lax.broadcasted_iota(jnp.int32, (CHUNK, CHUNK), 1)
        back_own = qi - kj
        back_prev = back_own + CHUNK
        for h in range(H_G):
            slope = SLOPES[h] * dil * LOG2E
            bias_scr[0, h] = jnp.where(back_prev <= N_BACK, -slope * back_prev.astype(F32), NEG_INF)
            bias_scr[1, h] = jnp.full((CHUNK, CHUNK), NEG_INF, F32)
            bias_scr[2, h] = jnp.where(back_own >= 0, -slope * back_own.astype(F32), NEG_INF)

    n_sub = ROW_TILE // CHUNK
    if chain:
        run_steps = PERM_TILE // dil // ROW_TILE
        at_start = (pl.program_id(1) == 0) & (pl.program_id(2) % run_steps == 0)
    else:
        at_start = pl.program_id(1) == 0
    zeros_half = jnp.zeros((HEAD_DIM, CHUNK), BF16)
    lse_pad = jnp.zeros((CHUNK - H_G, CHUNK), F32)

    def half_tile(k_rows, qt_h, vt_cols, bias):
        st = jnp.dot(k_rows, qt_h, preferred_element_type=F32) + bias
        m = jnp.max(st, axis=0, keepdims=True)
        p = jnp.exp2(st - m)
        l = jnp.sum(p, axis=0, keepdims=True)
        return m, l, jnp.dot(vt_cols, p.astype(BF16), preferred_element_type=F32)

    for s in range(n_sub):
        own = slice(s * CHUNK, (s + 1) * CHUNK)
        if chain and s > 0:
            prev_k, prev_vt = kc_ref, vtc_ref
            prv = slice((s - 1) * CHUNK, s * CHUNK)
        else:
            prev_k, prev_vt = kp_ref, vtp_ref
            prv = own if not chain else slice(0, CHUNK)
        masked = at_start if (s == 0 or not chain) else None
        lses = []
        for pair in range(H_G // 2):
            lanes = slice(pair * 2 * HEAD_DIM, (pair + 1) * 2 * HEAD_DIM)
            qt_pair = qt_ref[lanes, own]
            outs = []
            for half in range(2):
                h = 2 * pair + half
                chans = slice(h * HEAD_DIM, (h + 1) * HEAD_DIM)
                qt_h = (jnp.concatenate([qt_pair[0:HEAD_DIM], zeros_half], axis=0) if half == 0
                        else jnp.concatenate([zeros_half, qt_pair[HEAD_DIM:]], axis=0))
                bias_prev = bias_scr[0, h] if masked is None else bias_scr[jnp.where(masked, 1, 0), h]
                m_a, l_a, o_a = half_tile(prev_k[prv, lanes], qt_h, prev_vt[chans, prv], bias_prev)
                m_b, l_b, o_b = half_tile(kc_ref[own, lanes], qt_h, vtc_ref[chans, own], bias_scr[2, h])
                m = jnp.maximum(m_a, m_b)
                w_a, w_b = jnp.exp2(m_a - m), jnp.exp2(m_b - m)
                l = w_a * l_a + w_b * l_b
                outs.append((w_a * o_a + w_b * o_b) * (1.0 / l))
                lses.append((m + jnp.log2(l)) * LN2)
            o_ref[own, lanes] = jnp.concatenate(outs, axis=0).T
        lse_ref[own, :] = jnp.concatenate(lses + [lse_pad], axis=0).T


def _attn_prompt(k_all, qt_all, vt_all, batch, seq, g):
    d = DILATIONS[g]
    nt = seq // PERM_TILE
    nj = PERM_TILE // ROW_TILE
    sub_per_step = ROW_TILE // CHUNK
    run = PERM_TILE // d
    chain = run > CHUNK
    assert run % ROW_TILE == 0 if chain else run == CHUNK
    cur = lambda b, t, j: (b * nt + t) * nj + j
    if chain:
        back = (PERM_TILE - run) // CHUNK + 1
        prev_rows = CHUNK
        prev = lambda b, t, j: jnp.maximum(cur(b, t, j) * sub_per_step - back, 0)
    else:
        prev_rows = ROW_TILE
        prev = lambda b, t, j: jnp.maximum(cur(b, t, j) - nj, 0)
    tok = lambda rows, idx: pl.BlockSpec((None, rows, C_WIDTH), lambda b, t, j: (g, idx(b, t, j), 0))
    chn = lambda cols, idx: pl.BlockSpec((None, C_WIDTH, cols), lambda b, t, j: (g, 0, idx(b, t, j)))
    return pl.pallas_call(
        functools.partial(_attn_kernel, dil=d, chain=chain),
        grid=(batch, nt, nj),
        in_specs=[chn(ROW_TILE, cur), tok(ROW_TILE, cur), tok(prev_rows, prev),
                  chn(ROW_TILE, cur), chn(prev_rows, prev)],
        out_specs=[pl.BlockSpec((ROW_TILE, C_WIDTH), lambda b, t, j: (cur(b, t, j), 0)),
                   pl.BlockSpec((ROW_TILE, LSE_LANES), lambda b, t, j: (cur(b, t, j), 0))],
        out_shape=[jax.ShapeDtypeStruct((batch * seq, C_WIDTH), F32),
                   jax.ShapeDtypeStruct((batch * seq, LSE_LANES), F32)],
        scratch_shapes=[pltpu.VMEM((3, H_G, CHUNK, CHUNK), F32)],
        compiler_params=_params(3),
        name=f"attn_prompt_d{d}",
    )(qt_all, k_all, k_all, vt_all, vt_all)


def _sattn_kernel(*refs, buf_len, dil, t_new):
    q_ref, kn_ref, vn_ref, cache_ref = refs[:4]
    o_ref, lse_ref, newc_ref, bias_c, bias_n = refs[-5:]
    new_lane0 = CHUNK - t_new

    @pl.when(pl.program_id(0) == 0)
    def _():
        def bias(dist, ok, h):
            b0 = jnp.where((dist & (dil - 1)) == 0, -SLOPES[h] * dist.astype(F32), NEG_INF)
            b0 = jnp.where(dist >= 0, b0, NEG_INF)
            b0 = jnp.where(dist <= N_BACK * dil, b0, NEG_INF)
            return b0 if ok is None else jnp.where(ok, b0, NEG_INF)
        t_c = lax.broadcasted_iota(jnp.int32, (t_new, buf_len), 0)
        p_c = lax.broadcasted_iota(jnp.int32, (t_new, buf_len), 1)
        t_n = lax.broadcasted_iota(jnp.int32, (t_new, CHUNK), 0)
        j_n = lax.broadcasted_iota(jnp.int32, (t_new, CHUNK), 1)
        for h in range(H_G):
            bias_c[h] = bias(buf_len + t_c - p_c, None, h)
            bias_n[h] = bias(t_n - (j_n - new_lane0), j_n >= new_lane0, h)

    pad = jnp.zeros((new_lane0, C_WIDTH), F32)
    lane = lax.broadcasted_iota(jnp.int32, (HEAD_DIM, CHUNK), 1)
    nt = (((1,), (1,)), ((), ()))

    for sb in range(cache_ref.shape[0]):
        rows = slice(sb * t_new, (sb + 1) * t_new)
        kn_t = jnp.concatenate([pad, kn_ref[rows, :]], axis=0).T
        vn_t = jnp.concatenate([pad, vn_ref[rows, :]], axis=0).T
        qb = (q_ref[rows, :] * QK_SCALE).astype(BF16)
        outs, lses = [], []
        for h in range(H_G):
            cols = slice(h * HEAD_DIM, (h + 1) * HEAD_DIM)
            k_t, v_t = cache_ref[sb, 0, h], cache_ref[sb, 1, h]
            knh_t, vnh_t = kn_t[cols], vn_t[cols]
            qh = qb[:, cols]
            lc = jnp.dot(qh, k_t.astype(BF16), preferred_element_type=F32) + bias_c[h]
            ln = jnp.dot(qh, knh_t.astype(BF16), preferred_element_type=F32) + bias_n[h]
            m = jnp.maximum(jnp.max(lc, axis=-1, keepdims=True), jnp.max(ln, axis=-1, keepdims=True))
            pc = jnp.exp(lc - m)
            pn = jnp.exp(ln - m)
            l = jnp.sum(pc, axis=-1, keepdims=True) + jnp.sum(pn, axis=-1, keepdims=True)
            o = (lax.dot_general(pc.astype(BF16), v_t.astype(BF16), nt, preferred_element_type=F32)
                 + lax.dot_general(pn.astype(BF16), vnh_t.astype(BF16), nt, preferred_element_type=F32))
            outs.append(o / l)
            lses.append(jnp.broadcast_to(m + jnp.log(l), (t_new, HEAD_DIM)))
            for kv, old, new in ((0, k_t, knh_t), (1, v_t, vnh_t)):
                rolled = pltpu.roll(old, buf_len - t_new, 1)
                if buf_len > CHUNK:
                    newc_ref[sb, kv, h, :, 0:buf_len - CHUNK] = rolled[:, 0:buf_len - CHUNK]
                newc_ref[sb, kv, h, :, buf_len - CHUNK:buf_len] = jnp.where(
                    lane >= new_lane0, new, rolled[:, buf_len - CHUNK:buf_len])
        o_ref[rows, :] = jnp.concatenate(outs, axis=-1)
        lse_ref[rows, :] = jnp.concatenate(lses, axis=-1)


def _attn_sample(qkv, cache, prev_out, layer_idx, g, t_new):
    n_layers, n_seq, _, _, _, buf_len = cache.shape
    d = DILATIONS[g]
    n_sb = max(1, min(n_seq, SAMPLE_STEP_POSITIONS // buf_len))
    assert n_seq % n_sb == 0
    row = lambda sec: pl.BlockSpec((n_sb * t_new, C_WIDTH), lambda b: (b, sec * N_GROUPS + g))
    cache_spec = pl.BlockSpec((None, n_sb, 2, H_G, HEAD_DIM, buf_len), lambda b: (layer_idx, b, 0, 0, 0, 0))
    out_row = pl.BlockSpec((n_sb * t_new, C_WIDTH), lambda b: (b, 0))
    in_specs = [row(0), row(1), row(2), cache_spec]
    args = [qkv, qkv, qkv, cache]
    aliases = {}
    if prev_out is not None:
        in_specs.append(pl.BlockSpec(memory_space=pl.ANY))
        args.append(prev_out)
        aliases = {4: 2}
    return pl.pallas_call(
        functools.partial(_sattn_kernel, buf_len=buf_len, dil=d, t_new=t_new),
        grid=(n_seq // n_sb,),
        in_specs=in_specs,
        out_specs=[out_row, out_row, cache_spec],
        out_shape=[jax.ShapeDtypeStruct((n_seq * t_new, C_WIDTH), F32)] * 2
                  + [jax.ShapeDtypeStruct(cache.shape, F32)],
        scratch_shapes=[pltpu.VMEM((H_G, t_new, buf_len), F32), pltpu.VMEM((H_G, t_new, CHUNK), F32)],
        input_output_aliases=aliases,
        compiler_params=_params(1),
        name=f"attn_sample_d{d}",
    )(*args)


def _merge_kernel(*refs, tm, dils):
    x_ref = refs[0]
    ol_refs = refs[1:1 + 2 * N_GROUPS]
    w_ref, g_ref, b_ref, out_ref = refs[1 + 2 * N_GROUPS:5 + 2 * N_GROUPS]
    scrs = list(refs[5 + 2 * N_GROUPS:])

    def natural(ref, d):
        if d == 1:
            return ref[...]
        scr = scrs.pop()
        n = tm // d
        slabs = ref.shape[-1] // 128
        for r in range(d):
            for s in range(slabs):
                scr[s, pl.ds(r, n, stride=d), :] = ref[r, :, s * 128:(s + 1) * 128]
        return jnp.concatenate([scr[s] for s in range(slabs)], axis=-1)

    vals = [natural(ol_refs[k], dils[k % N_GROUPS]) for k in range(2 * N_GROUPS)]
    (o0, o1, o2), (l0, l1, l2) = vals[:N_GROUPS], vals[N_GROUPS:]
    m = jnp.maximum(jnp.maximum(l0, l1), l2)
    e0, e1, e2 = jnp.exp(l0 - m), jnp.exp(l1 - m), jnp.exp(l2 - m)
    inv = 1.0 / (e0 + e1 + e2)
    alphas = [e0 * inv, e1 * inv, e2 * inv]
    if l0.shape[-1] != C_WIDTH:
        lane = lax.broadcasted_iota(jnp.int32, (2 * LSE_LANES, C_WIDTH), 0) & (LSE_LANES - 1)
        chan = lax.broadcasted_iota(jnp.int32, (2 * LSE_LANES, C_WIDTH), 1)
        spread = jnp.where((chan >> (HEAD_DIM.bit_length() - 1)) == lane, 1.0, 0.0).astype(BF16)

        def expand(a):
            hi = a.astype(BF16)
            lo = (a - hi.astype(F32)).astype(BF16)
            return jnp.dot(jnp.concatenate([hi, lo], axis=-1), spread, preferred_element_type=F32)
        alphas = [expand(a) for a in alphas]
    o = alphas[0] * o0 + alphas[1] * o1 + alphas[2] * o2
    x = x_ref[...]
    mix = jnp.dot(o.astype(BF16), w_ref[...], preferred_element_type=F32)
    out_ref[...] = _layer_norm(ALPHA * x + mix, g_ref[...], b_ref[...])


def _merge_layer(x, outs, lses, w_out, g, b, dils):
    rows = x.shape[0]
    tm = min(MERGE_TILE, rows)
    per_tile = PERM_TILE // tm
    row_spec = lambda cols: pl.BlockSpec((tm, cols), lambda t: (t, 0))

    def group_arg(a, d):
        cols = a.shape[-1]
        if d == 1:
            return a, row_spec(cols), None
        run = PERM_TILE // d
        view = a.reshape(rows // PERM_TILE, d, run, cols)
        spec = pl.BlockSpec((None, d, tm // d, cols), lambda t: (t // per_tile, 0, t % per_tile, 0))
        return view, spec, pltpu.VMEM((cols // 128, tm, 128), F32)

    args, specs, scratch = zip(*[group_arg(a, dils[k % N_GROUPS]) for k, a in enumerate(list(outs) + list(lses))])
    scratch = [s for s in scratch if s is not None][::-1]
    weights, w_specs = _resident_all((w_out, g, b))
    return pl.pallas_call(
        functools.partial(_merge_kernel, tm=tm, dils=dils),
        grid=(rows // tm,),
        in_specs=[row_spec(D_MODEL)] + list(specs) + w_specs,
        out_specs=row_spec(D_MODEL),
        out_shape=jax.ShapeDtypeStruct((rows, D_MODEL), F32),
        scratch_shapes=scratch,
        compiler_params=_params(1),
        name="merge_layer",
    )(x, *args, *weights)


def kernel(x_prompt, x_sample, state_conv, cache_kv_w128, cache_kv_w512, cache_kv_w2048, w_in_ab, ln_v_g, ln_v_b, w_spatial, b_spatial, conv_w, w_out_ab, w_qkv_c, w_out_c, ln1_g, ln1_b, ln2_g, ln2_b, w_mlp_up, w_mlp_down):
    batch, seq, _ = x_prompt.shape
    n_seq, t_new, _ = x_sample.shape
    n_tok_s = n_seq * t_new
    xp = x_prompt.reshape(batch * seq, D_MODEL)
    xs = x_sample.reshape(n_tok_s, D_MODEL)
    caches = [jnp.transpose(c, (0, 1, 3, 4, 5, 2)) for c in (cache_kv_w128, cache_kv_w512, cache_kv_w2048)]
    new_caches = [None] * N_GROUPS
    rows3 = lambda a: a.reshape(a.shape[0], 1, a.shape[1])
    ln_v_g3, ln_v_b3, ln1_g3, ln1_b3, ln2_g3, ln2_b3 = map(rows3, (ln_v_g, ln_v_b, ln1_g, ln1_b, ln2_g, ln2_b))

    causal = jnp.tril(jnp.ones((CHUNK, CHUNK), dtype=bool))
    w_in_b, w_out_ab_b = w_in_ab.astype(BF16), w_out_ab.astype(BF16)
    w_up_b, w_down_b = w_mlp_up.astype(BF16), w_mlp_down.astype(BF16)
    w_qkv_b, w_out_c_b = w_qkv_c.astype(BF16), w_out_c.astype(BF16)
    n_c = w_qkv_b.shape[0]
    wq5 = w_qkv_b.reshape(n_c, D_MODEL, 3, N_GROUPS, C_WIDTH)
    wk_groups = jnp.transpose(wq5[:, :, 1], (0, 2, 1, 3))
    wqv_t_groups = jnp.transpose(wq5[:, :, 0::2], (0, 3, 2, 4, 1)).reshape(n_c, N_GROUPS, 2 * C_WIDTH, D_MODEL)
    wkv_t_groups = jnp.transpose(wq5[:, :, 1:], (0, 3, 2, 4, 1)).reshape(n_c, N_GROUPS, 2 * C_WIDTH, D_MODEL)
    conv_p, conv_s, chunk_v_s = [], [], []
    kv_p = [[] for _ in range(N_GROUPS)]

    for layer in range(DEPTH):
        i = layer // 2
        if layer % 2 == 0:
            w_tril = jnp.where(causal[None], w_spatial[i], 0.0)
            bmix = jnp.repeat(b_spatial[i].T, CHUNK, axis=1)
            shared = (_Layer(w_in_b, i), _Layer(ln_v_g3, i), _Layer(ln_v_b3, i))
            tail = (_Layer(conv_w, i), _Layer(w_out_ab_b, i), _Layer(ln1_g3, layer), _Layer(ln1_b3, layer))
            xp, buf_p = _ab_layer_prompt(xp, batch, seq,
                                         shared + (w_tril.astype(BF16), bmix) + tail)
            eye = jnp.eye(n_seq, dtype=F32)
            w_blk = jnp.stack([jnp.kron(eye, w_tril[g, :t_new, :t_new]) for g in range(G_A)])
            bmix_s = jnp.tile(bmix[:t_new], (n_seq, 1))
            st = state_conv[i]
            s1 = jnp.concatenate([st[:, 1:2], jnp.zeros((n_seq, t_new - 1, W_B), F32)], axis=1)
            s2 = jnp.concatenate([st, jnp.zeros((n_seq, t_new - 2, W_B), F32)], axis=1)
            xs, hc_s, v_s = _ab_layer_sample(xs, s1.reshape(n_tok_s, W_B), s2.reshape(n_tok_s, W_B),
                                             shared + (w_blk.astype(BF16), bmix_s) + tail)
            conv_p.append(buf_p)
            conv_s.append(hc_s.reshape(n_seq, t_new, W_B)[:, t_new - 2:])
            chunk_v_s.append(v_s.reshape(n_seq, t_new, W_A))
        else:
            wo = _Layer(w_out_c_b, i)
            g1, b1 = _Layer(ln1_g3, layer), _Layer(ln1_b3, layer)
            tails = _kv_tail(xp, _Layer(wkv_t_groups, i), batch, seq)
            for g in range(N_GROUPS):
                kv_p[g].append(tails[g].reshape(batch, 2, H_G, HEAD_DIM, WINDOWS[g]))
            k_all, qt_all, vt_all = _qkv_proj_perm(xp, wk_groups, wqv_t_groups, i, batch, seq)
            outs, lses = zip(*[_attn_prompt(k_all, qt_all, vt_all, batch, seq, g) for g in range(N_GROUPS)])
            xp = _merge_layer(xp, outs, lses, wo, g1, b1, DILATIONS)
            qkv_s = _qkv_proj(xs, _Layer(w_qkv_b, i))
            outs, lses = [], []
            for g in range(N_GROUPS):
                o, lse, new_caches[g] = _attn_sample(qkv_s, caches[g], new_caches[g], i, g, t_new)
                outs.append(o)
                lses.append(lse)
            xs = _merge_layer(xs, outs, lses, wo, g1, b1, (1,) * N_GROUPS)
        w_up, w_down = _Layer(w_up_b, layer), _Layer(w_down_b, layer)
        g2, b2 = _Layer(ln2_g3, layer), _Layer(ln2_b3, layer)
        xp = _mlp_layer(xp, w_up, w_down, g2, b2)
        xs = _mlp_layer(xs, w_up, w_down, g2, b2)

    kv_s = [jnp.transpose(nc, (0, 1, 5, 2, 3, 4)) for nc in new_caches]
    return (xp.reshape(batch, seq, D_MODEL), xs.reshape(n_seq, t_new, D_MODEL),
            jnp.stack(conv_p), jnp.stack(conv_s), jnp.stack(chunk_v_s),
            *[jnp.transpose(jnp.stack(kv_p[g]), (0, 1, 5, 2, 3, 4)) for g in range(N_GROUPS)],
            kv_s[0], kv_s[1], kv_s[2])
```

```python
import functools
import math
from typing import NamedTuple

import jax
import jax.numpy as jnp
from jax import lax
from jax.experimental import pallas as pl
from jax.experimental.pallas import tpu as pltpu

F32 = jnp.float32
BF16 = jnp.bfloat16

D_MODEL = 1024
DEPTH = 4
CHUNK = 128
W_A = 512
G_A = 4
W_B = 512
N_GROUPS = 3
WINDOWS = (128, 512, 2048)
DILATIONS = (1, 4, 16)
N_BACK = 128
H_G = 8
HEAD_DIM = 64
C_WIDTH = H_G * HEAD_DIM
QKV_COLS = N_GROUPS * C_WIDTH
QKV_BLOCKS = 3 * N_GROUPS
D_FF = 4 * D_MODEL
ALPHA = (2.0 * DEPTH) ** 0.25
LN_EPS = 1e-5
NEG_INF = -1e30
SLOPES = tuple(2.0 ** (-(8.0 / H_G) * j) for j in range(1, H_G + 1))
QK_SCALE = HEAD_DIM ** -0.5
LOG2E = math.log2(math.e)
LN2 = math.log(2.0)

VMEM_LIMIT_BYTES = 52 * 1024 * 1024
ROW_TILE = 512
PERM_TILE = CHUNK * max(DILATIONS)
X_SLABS = D_MODEL // 128
MERGE_TILE = 512
AB_PARTS = 1
SAMPLE_STEP_POSITIONS = 1024
LSE_LANES = 128


def _layer_norm(x, g, b):
    mu = jnp.mean(x, axis=-1, keepdims=True)
    xc = x - mu
    var = jnp.mean(xc * xc, axis=-1, keepdims=True)
    return xc * lax.rsqrt(var + LN_EPS) * g + b


def _gelu(x):
    c = math.sqrt(2.0 / math.pi)
    return x * (0.5 * (1.0 + jnp.tanh(c * (x + 0.044715 * (x * x * x)))))


class _Layer(NamedTuple):
    stacked: jax.Array
    index: int


def _resident(w):
    if isinstance(w, _Layer):
        shape = w.stacked.shape[1:]
        return w.stacked, pl.BlockSpec((None,) + shape, lambda *_: (w.index,) + (0,) * len(shape),
                                       pipeline_mode=pl.Buffered(1))
    return w, pl.BlockSpec(w.shape, lambda *_: (0,) * w.ndim, pipeline_mode=pl.Buffered(1))


def _resident_all(ws):
    arrays, specs = zip(*[_resident(w) for w in ws])
    return list(arrays), list(specs)


def _params(n_grid):
    return pltpu.CompilerParams(dimension_semantics=("arbitrary",) * n_grid,
                                vmem_limit_bytes=VMEM_LIMIT_BYTES)


def _ab_kernel(*refs, tm, chunk, sample, n_parts):
    if sample:
        (x_ref, s1_ref, s2_ref, w_in_ref, lvg_ref, lvb_ref, wmix_ref, bmix_ref, cw_ref, w_out_ref,
         g1_ref, b1_ref, o_ref, hc_ref, v_ref, y_scr) = refs
    else:
        (x_ref, w_in_ref, lvg_ref, lvb_ref, wmix_ref, bmix_ref, cw_ref, w_out_ref,
         g1_ref, b1_ref, o_ref, cb_ref, y_scr, h_scr) = refs

    if not sample:
        @pl.when(pl.program_id(1) == 0)
        def _():
            h_scr[0:8, :] = jnp.zeros((8, W_B), F32)

    def part(r0, n):
        rows_p = slice(r0, r0 + n)
        x = x_ref[rows_p, :]
        xb = x.astype(BF16)

        def proj(k):
            return jnp.dot(xb, w_in_ref[:, k * W_A:(k + 1) * W_A], preferred_element_type=F32)

        u = _gelu(proj(0))
        v = _layer_norm(_gelu(proj(1)), lvg_ref[...], lvb_ref[...])
        if sample:
            v_ref[...] = v
        vb = v.astype(BF16)
        for c in range(n // chunk):
            rows = slice(c * chunk, (c + 1) * chunk)
            for g in range(G_A):
                cols = slice(g * 128, (g + 1) * 128)
                y_scr[r0 + c * chunk:r0 + (c + 1) * chunk, cols] = (
                    jnp.dot(wmix_ref[g], vb[rows, cols], preferred_element_type=F32) + bmix_ref[:, cols])
        a_out = (u * y_scr[rows_p, :]).astype(BF16)

        hc = proj(3) * proj(4)
        if sample:
            hc_ref[...] = hc
            pos = lax.broadcasted_iota(jnp.int32, hc.shape, 0) & 7
            sh1 = jnp.where(pos == 0, s1_ref[...], pltpu.roll(hc, 1, 0))
            sh2 = jnp.where(pos < 2, s2_ref[...], pltpu.roll(hc, 2, 0))
        else:
            h_scr[8 + r0:8 + r0 + n, :] = hc
            sh1 = h_scr[7 + r0:7 + r0 + n, :]
            sh2 = h_scr[6 + r0:6 + r0 + n, :]
        conv = cw_ref[0:1, :] * sh2 + cw_ref[1:2, :] * sh1 + cw_ref[2:3, :] * hc
        b_out = (proj(2) * conv).astype(BF16)

        mix = (jnp.dot(a_out, w_out_ref[0:W_A, :], preferred_element_type=F32)
               + jnp.dot(b_out, w_out_ref[W_A:W_A + W_B, :], preferred_element_type=F32))
        o_ref[rows_p, :] = _layer_norm(ALPHA * x + mix, g1_ref[...], b1_ref[...])

    for p in range(n_parts):
        part(p * (tm // n_parts), tm // n_parts)
    if not sample:
        cb_ref[...] = h_scr[tm + 6:tm + 8, :]
        h_scr[0:8, :] = h_scr[tm:tm + 8, :]


def _ab_layer_prompt(x, batch, seq, weights):
    tm = ROW_TILE
    nt = seq // tm
    row_spec = pl.BlockSpec((tm, D_MODEL), lambda b, t: (b * nt + t, 0))
    weights, w_specs = _resident_all(weights)
    return pl.pallas_call(
        functools.partial(_ab_kernel, tm=tm, chunk=CHUNK, sample=False, n_parts=AB_PARTS),
        grid=(batch, nt),
        in_specs=[row_spec] + w_specs,
        out_specs=[row_spec, pl.BlockSpec((None, 2, W_B), lambda b, t: (b, 0, 0))],
        out_shape=[jax.ShapeDtypeStruct((batch * seq, D_MODEL), F32),
                   jax.ShapeDtypeStruct((batch, 2, W_B), F32)],
        scratch_shapes=[pltpu.VMEM((tm, W_A), F32), pltpu.VMEM((tm + 8, W_B), F32)],
        compiler_params=_params(2),
        name="ab_layer_prompt",
    )(x, *weights)


def _ab_layer_sample(x, s1, s2, weights):
    tm = x.shape[0]
    full = lambda cols: pl.BlockSpec((tm, cols), lambda i: (0, 0))
    weights, w_specs = _resident_all(weights)
    return pl.pallas_call(
        functools.partial(_ab_kernel, tm=tm, chunk=tm, sample=True, n_parts=1),
        grid=(1,),
        in_specs=[full(D_MODEL), full(W_B), full(W_B)] + w_specs,
        out_specs=[full(D_MODEL), full(W_B), full(W_A)],
        out_shape=[jax.ShapeDtypeStruct((tm, D_MODEL), F32),
                   jax.ShapeDtypeStruct((tm, W_B), F32),
                   jax.ShapeDtypeStruct((tm, W_A), F32)],
        scratch_shapes=[pltpu.VMEM((tm, W_A), F32)],
        compiler_params=_params(1),
        name="ab_layer_sample",
    )(x, s1, s2, *weights)


def _mlp_kernel(x_ref, w_up_ref, w_down_ref, g_ref, b_ref, *rest, n_roll=0, roll_by=0):
    old_refs, o_ref, new_refs = rest[:n_roll], rest[-n_roll - 1], rest[len(rest) - n_roll:]
    for old, new in zip(old_refs, new_refs):
        buf_len = old.shape[-1]
        for h in range(H_G):
            new[h] = pltpu.roll(old[h], buf_len - roll_by, 1)
    x = x_ref[...]
    xb = x.astype(BF16)
    acc = None
    for c in range(D_FF // D_MODEL):
        cols = slice(c * D_MODEL, (c + 1) * D_MODEL)
        h = jnp.maximum(jnp.dot(xb, w_up_ref[:, cols], preferred_element_type=F32), 0.0)
        part = jnp.dot((h * h).astype(BF16), w_down_ref[cols, :], preferred_element_type=F32)
        acc = part if acc is None else acc + part
    o_ref[...] = _layer_norm(ALPHA * x + acc, g_ref[...], b_ref[...])


def _mlp_layer(x, w_up, w_down, g, b, roll=None):
    rows = x.shape[0]
    tm = min(ROW_TILE, rows)
    steps = rows // tm
    row_spec = pl.BlockSpec((tm, D_MODEL), lambda t: (t, 0))
    weights, w_specs = _resident_all((w_up, w_down, g, b))
    args, in_specs = [x] + weights, [row_spec] + w_specs
    out_specs, out_shape = [row_spec], [jax.ShapeDtypeStruct((rows, D_MODEL), F32)]
    aliases, n_roll, roll_by = {}, 0, 0
    if roll is not None:
        bufs, rolled, first, roll_by = roll
        n_roll = len(bufs)
        n_seq = bufs[0].shape[1]

        def block_spec(buf):
            def index(t):
                blk = first + t
                return (blk // (2 * n_seq), (blk % (2 * n_seq)) // 2, blk % 2, 0, 0, 0)
            return pl.BlockSpec((None, None, None) + buf.shape[3:], index)
        args += list(bufs)
        in_specs += [block_spec(buf) for buf in bufs]
        if rolled is not None:
            aliases = {len(args) + k: 1 + k for k in range(n_roll)}
            args += list(rolled)
            in_specs += [pl.BlockSpec(memory_space=pl.ANY)] * n_roll
        out_specs += [block_spec(buf) for buf in bufs]
        out_shape += [jax.ShapeDtypeStruct(buf.shape, buf.dtype) for buf in bufs]
    outs = pl.pallas_call(
        functools.partial(_mlp_kernel, n_roll=n_roll, roll_by=roll_by),
        grid=(steps,),
        in_specs=in_specs,
        out_specs=out_specs,
        out_shape=out_shape,
        input_output_aliases=aliases,
        compiler_params=_params(1),
        name="mlp_layer",
    )(*args)
    return outs[0] if roll is None else (outs[0], list(outs[1:]))


def _qkv_kernel(x_ref, w_ref, o_ref):
    xb = x_ref[...].astype(BF16)
    for c in range(QKV_BLOCKS):
        cols = slice(c * C_WIDTH, (c + 1) * C_WIDTH)
        o_ref[:, cols] = jnp.dot(xb, w_ref[:, cols], preferred_element_type=F32)


def _qkv_proj(x, w):
    rows = x.shape[0]
    tm = min(256, rows)
    w, w_spec = _resident(w)
    return pl.pallas_call(
        _qkv_kernel,
        grid=(rows // tm,),
        in_specs=[pl.BlockSpec((tm, D_MODEL), lambda t: (t, 0)), w_spec],
        out_specs=pl.BlockSpec((tm, 3 * QKV_COLS), lambda t: (t, 0)),
        out_shape=jax.ShapeDtypeStruct((rows, 3 * QKV_COLS), F32),
        compiler_params=_params(1),
        name="qkv_proj",
    )(x, w)


def _qkv_perm_kernel(*refs):
    x_refs, (wk_ref, wqv_t_ref, k_ref, qt_ref, vt_ref, xp_scr) = refs[:X_SLABS], refs[X_SLABS:]
    nt_dims = (((1,), (1,)), ((), ()))

    def step(d):
        n = PERM_TILE // d
        for c in range(PERM_TILE // ROW_TILE):
            rows = slice(c * ROW_TILE, (c + 1) * ROW_TILE)
            for r in range(d):
                lo, hi = max(r * n, rows.start), min((r + 1) * n, rows.stop)
                if lo >= hi:
                    continue
                for s in range(X_SLABS):
                    src = x_refs[s]
                    vals = (src[lo:hi, :] if d == 1 else src[pl.ds(r + (lo - r * n) * d, hi - lo, stride=d), :])
                    xp_scr[lo:hi, s * 128:(s + 1) * 128] = vals.astype(BF16)
            xs = xp_scr[rows, :]
            k_ref[rows, :] = jnp.dot(xs, wk_ref[...], preferred_element_type=F32).astype(BF16)
            qv_t = lax.dot_general(wqv_t_ref[...], xs, nt_dims, preferred_element_type=F32)
            qt_ref[:, rows] = (qv_t[0:C_WIDTH] * (QK_SCALE * LOG2E)).astype(BF16)
            vt_ref[:, rows] = qv_t[C_WIDTH:2 * C_WIDTH].astype(BF16)

    for g in range(N_GROUPS):
        pl.when(pl.program_id(2) == g)(functools.partial(step, DILATIONS[g]))


def _qkv_proj_perm(x, wk_groups, wqv_t_groups, layer_idx, batch, seq):
    nt = seq // PERM_TILE
    slab = lambda s: pl.BlockSpec((PERM_TILE, 128), lambda b, t, g: (b * nt + t, s))
    chan_spec = pl.BlockSpec((None, C_WIDTH, PERM_TILE), lambda b, t, g: (g, 0, b * nt + t))
    chan_shape = jax.ShapeDtypeStruct((N_GROUPS, C_WIDTH, batch * seq), BF16)
    return pl.pallas_call(
        _qkv_perm_kernel,
        grid=(batch, nt, N_GROUPS),
        in_specs=[slab(s) for s in range(X_SLABS)]
                 + [pl.BlockSpec((None, None, D_MODEL, C_WIDTH), lambda b, t, g: (layer_idx, g, 0, 0)),
                    pl.BlockSpec((None, None, 2 * C_WIDTH, D_MODEL), lambda b, t, g: (layer_idx, g, 0, 0))],
        out_specs=[pl.BlockSpec((None, PERM_TILE, C_WIDTH), lambda b, t, g: (g, b * nt + t, 0)),
                   chan_spec, chan_spec],
        out_shape=[jax.ShapeDtypeStruct((N_GROUPS, batch * seq, C_WIDTH), BF16), chan_shape, chan_shape],
        scratch_shapes=[pltpu.VMEM((PERM_TILE, D_MODEL), BF16)],
        compiler_params=_params(3),
        name="qkv_proj_perm",
    )(*([x] * X_SLABS), wk_groups, wqv_t_groups)


def _kv_tail_kernel(x_ref, wt_ref, o0_ref, o1_ref, o2_ref):
    xb = x_ref[...].astype(BF16)
    nt = (((1,), (1,)), ((), ()))
    o2_ref[...] = lax.dot_general(wt_ref[2], xb, nt, preferred_element_type=F32)

    @pl.when(pl.program_id(1) == pl.num_programs(1) - 1)
    def _():
        o1_ref[...] = lax.dot_general(wt_ref[1], xb, nt, preferred_element_type=F32)
        o0_ref[...] = lax.dot_general(wt_ref[0], xb[ROW_TILE - WINDOWS[0]:, :], nt, preferred_element_type=F32)


def _kv_tail(x, wt_groups, batch, seq):
    nj = WINDOWS[2] // ROW_TILE
    first = (seq - WINDOWS[2]) // ROW_TILE
    nblk = seq // ROW_TILE
    last = lambda n: pl.BlockSpec((None, 2 * C_WIDTH, n), lambda b, j: (b, 0, 0))
    wt_groups, wt_spec = _resident(wt_groups)
    return pl.pallas_call(
        _kv_tail_kernel,
        grid=(batch, nj),
        in_specs=[pl.BlockSpec((ROW_TILE, D_MODEL), lambda b, j: (b * nblk + first + j, 0)), wt_spec],
        out_specs=[last(WINDOWS[0]), last(WINDOWS[1]),
                   pl.BlockSpec((None, 2 * C_WIDTH, ROW_TILE), lambda b, j: (b, 0, j))],
        out_shape=[jax.ShapeDtypeStruct((batch, 2 * C_WIDTH, n), F32) for n in WINDOWS],
        compiler_params=_params(2),
        name="kv_tail",
    )(x, wt_groups)


def _attn_kernel(qt_ref, kc_ref, kp_ref, vtc_ref, vtp_ref, o_ref, lse_ref, bias_scr, *, dil, chain):
    first_step = (pl.program_id(0) == 0) & (pl.program_id(1) == 0) & (pl.program_id(2) == 0)

    @pl.when(first_step)
    def _():
        kj = lax.broadcasted_iota(jnp.int32, (CHUNK, CHUNK), 0)
        qi = lax.broadcasted_iota(jnp.int32, (CHUNK, CHUNK), 1)
        back_own = qi - kj
        back_prev = back_own + CHUNK
        for h in range(H_G):
            slope = SLOPES[h] * dil * LOG2E
            bias_scr[0, h] = jnp.where(back_prev <= N_BACK, -slope * back_prev.astype(F32), NEG_INF)
            bias_scr[1, h] = jnp.full((CHUNK, CHUNK), NEG_INF, F32)
            bias_scr[2, h] = jnp.where(back_own >= 0, -slope * back_own.astype(F32), NEG_INF)

    n_sub = ROW_TILE // CHUNK
    if chain:
        run_steps = PERM_TILE // dil // ROW_TILE
        at_start = (pl.program_id(1) == 0) & (pl.program_id(2) % run_steps == 0)
    else:
        at_start = pl.program_id(1) == 0
    zeros_half = jnp.zeros((HEAD_DIM, CHUNK), BF16)
    lse_pad = jnp.zeros((CHUNK - H_G, CHUNK), F32)

    def half_tile(k_rows, qt_h, vt_cols, bias):
        st = jnp.dot(k_rows, qt_h, preferred_element_type=F32) + bias
        m = jnp.max(st, axis=0, keepdims=True)
        p = jnp.exp2(st - m)
        l = jnp.sum(p, axis=0, keepdims=True)
        return m, l, jnp.dot(vt_cols, p.astype(BF16), preferred_element_type=F32)

    for s in range(n_sub):
        own = slice(s * CHUNK, (s + 1) * CHUNK)
        if chain and s > 0:
            prev_k, prev_vt = kc_ref, vtc_ref
            prv = slice((s - 1) * CHUNK, s * CHUNK)
        else:
            prev_k, prev_vt = kp_ref, vtp_ref
            prv = own if not chain else slice(0, CHUNK)
        masked = at_start if (s == 0 or not chain) else None
        lses = []
        for pair in range(H_G // 2):
            lanes = slice(pair * 2 * HEAD_DIM, (pair + 1) * 2 * HEAD_DIM)
            qt_pair = qt_ref[lanes, own]
            outs = []
            for half in range(2):
                h = 2 * pair + half
                chans = slice(h * HEAD_DIM, (h + 1) * HEAD_DIM)
                qt_h = (jnp.concatenate([qt_pair[0:HEAD_DIM], zeros_half], axis=0) if half == 0
                        else jnp.concatenate([zeros_half, qt_pair[HEAD_DIM:]], axis=0))
                bias_prev = bias_scr[0, h] if masked is None else bias_scr[jnp.where(masked, 1, 0), h]
                m_a, l_a, o_a = half_tile(prev_k[prv, lanes], qt_h, prev_vt[chans, prv], bias_prev)
                m_b, l_b, o_b = half_tile(kc_ref[own, lanes], qt_h, vtc_ref[chans, own], bias_scr[2, h])
                m = jnp.maximum(m_a, m_b)
                w_a, w_b = jnp.exp2(m_a - m), jnp.exp2(m_b - m)
                l = w_a * l_a + w_b * l_b
                outs.append((w_a * o_a + w_b * o_b) * (1.0 / l))
                lses.append((m + jnp.log2(l)) * LN2)
            o_ref[own, lanes] = jnp.concatenate(outs, axis=0).T
        lse_ref[own, :] = jnp.concatenate(lses + [lse_pad], axis=0).T


def _attn_prompt(k_all, qt_all, vt_all, batch, seq, g):
    d = DILATIONS[g]
    nt = seq // PERM_TILE
    nj = PERM_TILE // ROW_TILE
    sub_per_step = ROW_TILE // CHUNK
    run = PERM_TILE // d
    chain = run > CHUNK
    assert run % ROW_TILE == 0 if chain else run == CHUNK
    cur = lambda b, t, j: (b * nt + t) * nj + j
    if chain:
        back = (PERM_TILE - run) // CHUNK + 1
        prev_rows = CHUNK
        prev = lambda b, t, j: jnp.maximum(cur(b, t, j) * sub_per_step - back, 0)
    else:
        prev_rows = ROW_TILE
        prev = lambda b, t, j: jnp.maximum(cur(b, t, j) - nj, 0)
    tok = lambda rows, idx: pl.BlockSpec((None, rows, C_WIDTH), lambda b, t, j: (g, idx(b, t, j), 0))
    chn = lambda cols, idx: pl.BlockSpec((None, C_WIDTH, cols), lambda b, t, j: (g, 0, idx(b, t, j)))
    return pl.pallas_call(
        functools.partial(_attn_kernel, dil=d, chain=chain),
        grid=(batch, nt, nj),
        in_specs=[chn(ROW_TILE, cur), tok(ROW_TILE, cur), tok(prev_rows, prev),
                  chn(ROW_TILE, cur), chn(prev_rows, prev)],
        out_specs=[pl.BlockSpec((ROW_TILE, C_WIDTH), lambda b, t, j: (cur(b, t, j), 0)),
                   pl.BlockSpec((ROW_TILE, LSE_LANES), lambda b, t, j: (cur(b, t, j), 0))],
        out_shape=[jax.ShapeDtypeStruct((batch * seq, C_WIDTH), F32),
                   jax.ShapeDtypeStruct((batch * seq, LSE_LANES), F32)],
        scratch_shapes=[pltpu.VMEM((3, H_G, CHUNK, CHUNK), F32)],
        compiler_params=_params(3),
        name=f"attn_prompt_d{d}",
    )(qt_all, k_all, k_all, vt_all, vt_all)


def _sattn_kernel(q_ref, kn_ref, vn_ref, cache_ref, o_ref, lse_ref, bias_c, bias_n, *, buf_len, dil, t_new):
    new_lane0 = CHUNK - t_new

    @pl.when(pl.program_id(0) == 0)
    def _():
        def bias(dist, ok, h):
            b0 = jnp.where((dist & (dil - 1)) == 0, -SLOPES[h] * dist.astype(F32), NEG_INF)
            b0 = jnp.where(dist >= 0, b0, NEG_INF)
            b0 = jnp.where(dist <= N_BACK * dil, b0, NEG_INF)
            return b0 if ok is None else jnp.where(ok, b0, NEG_INF)
        t_c = lax.broadcasted_iota(jnp.int32, (t_new, buf_len), 0)
        p_c = lax.broadcasted_iota(jnp.int32, (t_new, buf_len), 1)
        t_n = lax.broadcasted_iota(jnp.int32, (t_new, CHUNK), 0)
        j_n = lax.broadcasted_iota(jnp.int32, (t_new, CHUNK), 1)
        for h in range(H_G):
            bias_c[h] = bias(buf_len + t_c - p_c, None, h)
            bias_n[h] = bias(t_n - (j_n - new_lane0), j_n >= new_lane0, h)

    pad = jnp.zeros((new_lane0, C_WIDTH), F32)
    nt = (((1,), (1,)), ((), ()))

    for sb in range(cache_ref.shape[0]):
        rows = slice(sb * t_new, (sb + 1) * t_new)
        kn_t = jnp.concatenate([pad, kn_ref[rows, :]], axis=0).T
        vn_t = jnp.concatenate([pad, vn_ref[rows, :]], axis=0).T
        qb = (q_ref[rows, :] * QK_SCALE).astype(BF16)
        outs, lses = [], []
        for h in range(H_G):
            cols = slice(h * HEAD_DIM, (h + 1) * HEAD_DIM)
            k_t, v_t = cache_ref[sb, 0, h], cache_ref[sb, 1, h]
            knh_t, vnh_t = kn_t[cols], vn_t[cols]
            qh = qb[:, cols]
            lc = jnp.dot(qh, k_t.astype(BF16), preferred_element_type=F32) + bias_c[h]
            ln = jnp.dot(qh, knh_t.astype(BF16), preferred_element_type=F32) + bias_n[h]
            m = jnp.maximum(jnp.max(lc, axis=-1, keepdims=True), jnp.max(ln, axis=-1, keepdims=True))
            pc = jnp.exp(lc - m)
            pn = jnp.exp(ln - m)
            l = jnp.sum(pc, axis=-1, keepdims=True) + jnp.sum(pn, axis=-1, keepdims=True)
            o = (lax.dot_general(pc.astype(BF16), v_t.astype(BF16), nt, preferred_element_type=F32)
                 + lax.dot_general(pn.astype(BF16), vnh_t.astype(BF16), nt, preferred_element_type=F32))
            outs.append(o / l)
            lses.append(jnp.broadcast_to(m + jnp.log(l), (t_new, HEAD_DIM)))
        o_ref[rows, :] = jnp.concatenate(outs, axis=-1)
        lse_ref[rows, :] = jnp.concatenate(lses, axis=-1)


def _attn_sample(qkv, cache, layer_idx, g, t_new):
    n_layers, n_seq, _, _, _, buf_len = cache.shape
    d = DILATIONS[g]
    n_sb = max(1, min(n_seq, SAMPLE_STEP_POSITIONS // buf_len))
    assert n_seq % n_sb == 0
    row = lambda sec: pl.BlockSpec((n_sb * t_new, C_WIDTH), lambda b: (b, sec * N_GROUPS + g))
    cache_spec = pl.BlockSpec((None, n_sb, 2, H_G, HEAD_DIM, buf_len), lambda b: (layer_idx, b, 0, 0, 0, 0))
    out_row = pl.BlockSpec((n_sb * t_new, C_WIDTH), lambda b: (b, 0))
    return pl.pallas_call(
        functools.partial(_sattn_kernel, buf_len=buf_len, dil=d, t_new=t_new),
        grid=(n_seq // n_sb,),
        in_specs=[row(0), row(1), row(2), cache_spec],
        out_specs=[out_row, out_row],
        out_shape=[jax.ShapeDtypeStruct((n_seq * t_new, C_WIDTH), F32)] * 2,
        scratch_shapes=[pltpu.VMEM((H_G, t_new, buf_len), F32), pltpu.VMEM((H_G, t_new, CHUNK), F32)],
        compiler_params=_params(1),
        name=f"attn_sample_d{d}",
    )(qkv, qkv, qkv, cache)


def _patch_kernel(kn_ref, vn_ref, tail_ref, out_ref, *, t_new):
    new_lane0 = CHUNK - t_new
    pad = jnp.zeros((new_lane0, C_WIDTH), F32)
    lane = lax.broadcasted_iota(jnp.int32, (HEAD_DIM, CHUNK), 1)
    for sb in range(tail_ref.shape[0]):
        rows = slice(sb * t_new, (sb + 1) * t_new)
        for kv, new_ref in ((0, kn_ref), (1, vn_ref)):
            new_t = jnp.concatenate([pad, new_ref[rows, :]], axis=0).T
            for h in range(H_G):
                out_ref[sb, kv, h] = jnp.where(lane >= new_lane0, new_t[h * HEAD_DIM:(h + 1) * HEAD_DIM],
                                               tail_ref[sb, kv, h])


def _patch_rolled(rolled, qkv, layer_idx, g, t_new):
    n_layers, n_seq, _, _, _, buf_len = rolled.shape
    n_sb = min(n_seq, 8)
    assert n_seq % n_sb == 0
    row = lambda sec: pl.BlockSpec((n_sb * t_new, C_WIDTH), lambda b: (b, sec * N_GROUPS + g))
    tail_spec = pl.BlockSpec((None, n_sb, 2, H_G, HEAD_DIM, CHUNK),
                             lambda b: (layer_idx, b, 0, 0, 0, buf_len // CHUNK - 1))
    return pl.pallas_call(
        functools.partial(_patch_kernel, t_new=t_new),
        grid=(n_seq // n_sb,),
        in_specs=[row(1), row(2), tail_spec],
        out_specs=tail_spec,
        out_shape=jax.ShapeDtypeStruct(rolled.shape, rolled.dtype),
        input_output_aliases={2: 0},
        compiler_params=_params(1),
        name=f"patch_rolled_d{DILATIONS[g]}",
    )(qkv, qkv, rolled)


def _merge_kernel(*refs, tm, dils):
    x_ref = refs[0]
    ol_refs = refs[1:1 + 2 * N_GROUPS]
    w_ref, g_ref, b_ref, out_ref = refs[1 + 2 * N_GROUPS:5 + 2 * N_GROUPS]
    scrs = list(refs[5 + 2 * N_GROUPS:])

    def natural(ref, d):
        if d == 1:
            return ref[...]
        scr = scrs.pop()
        n = tm // d
        slabs = ref.shape[-1] // 128
        for r in range(d):
            for s in range(slabs):
                scr[s, pl.ds(r, n, stride=d), :] = ref[r, :, s * 128:(s + 1) * 128]
        return jnp.concatenate([scr[s] for s in range(slabs)], axis=-1)

    vals = [natural(ol_refs[k], dils[k % N_GROUPS]) for k in range(2 * N_GROUPS)]
    (o0, o1, o2), (l0, l1, l2) = vals[:N_GROUPS], vals[N_GROUPS:]
    m = jnp.maximum(jnp.maximum(l0, l1), l2)
    e0, e1, e2 = jnp.exp(l0 - m), jnp.exp(l1 - m), jnp.exp(l2 - m)
    inv = 1.0 / (e0 + e1 + e2)
    alphas = [e0 * inv, e1 * inv, e2 * inv]
    if l0.shape[-1] != C_WIDTH:
        lane = lax.broadcasted_iota(jnp.int32, (2 * LSE_LANES, C_WIDTH), 0) & (LSE_LANES - 1)
        chan = lax.broadcasted_iota(jnp.int32, (2 * LSE_LANES, C_WIDTH), 1)
        spread = jnp.where((chan >> (HEAD_DIM.bit_length() - 1)) == lane, 1.0, 0.0).astype(BF16)

        def expand(a):
            hi = a.astype(BF16)
            lo = (a - hi.astype(F32)).astype(BF16)
            return jnp.dot(jnp.concatenate([hi, lo], axis=-1), spread, preferred_element_type=F32)
        alphas = [expand(a) for a in alphas]
    o = alphas[0] * o0 + alphas[1] * o1 + alphas[2] * o2
    x = x_ref[...]
    mix = jnp.dot(o.astype(BF16), w_ref[...], preferred_element_type=F32)
    out_ref[...] = _layer_norm(ALPHA * x + mix, g_ref[...], b_ref[...])


def _merge_layer(x, outs, lses, w_out, g, b, dils):
    rows = x.shape[0]
    tm = min(MERGE_TILE, rows)
    per_tile = PERM_TILE // tm
    row_spec = lambda cols: pl.BlockSpec((tm, cols), lambda t: (t, 0))

    def group_arg(a, d):
        cols = a.shape[-1]
        if d == 1:
            return a, row_spec(cols), None
        run = PERM_TILE // d
        view = a.reshape(rows // PERM_TILE, d, run, cols)
        spec = pl.BlockSpec((None, d, tm // d, cols), lambda t: (t // per_tile, 0, t % per_tile, 0))
        return view, spec, pltpu.VMEM((cols // 128, tm, 128), F32)

    args, specs, scratch = zip(*[group_arg(a, dils[k % N_GROUPS]) for k, a in enumerate(list(outs) + list(lses))])
    scratch = [s for s in scratch if s is not None][::-1]
    weights, w_specs = _resident_all((w_out, g, b))
    return pl.pallas_call(
        functools.partial(_merge_kernel, tm=tm, dils=dils),
        grid=(rows // tm,),
        in_specs=[row_spec(D_MODEL)] + list(specs) + w_specs,
        out_specs=row_spec(D_MODEL),
        out_shape=jax.ShapeDtypeStruct((rows, D_MODEL), F32),
        scratch_shapes=scratch,
        compiler_params=_params(1),
        name="merge_layer",
    )(x, *args, *weights)


def kernel(x_prompt, x_sample, state_conv, cache_kv_w128, cache_kv_w512, cache_kv_w2048, w_in_ab, ln_v_g, ln_v_b, w_spatial, b_spatial, conv_w, w_out_ab, w_qkv_c, w_out_c, ln1_g, ln1_b, ln2_g, ln2_b, w_mlp_up, w_mlp_down):
    batch, seq, _ = x_prompt.shape
    n_seq, t_new, _ = x_sample.shape
    n_tok_s = n_seq * t_new
    xp = x_prompt.reshape(batch * seq, D_MODEL)
    xs = x_sample.reshape(n_tok_s, D_MODEL)
    caches = [jnp.transpose(c, (0, 1, 3, 4, 5, 2)) for c in (cache_kv_w128, cache_kv_w512, cache_kv_w2048)]
    rows3 = lambda a: a.reshape(a.shape[0], 1, a.shape[1])
    ln_v_g3, ln_v_b3, ln1_g3, ln1_b3, ln2_g3, ln2_b3 = map(rows3, (ln_v_g, ln_v_b, ln1_g, ln1_b, ln2_g, ln2_b))

    causal = jnp.tril(jnp.ones((CHUNK, CHUNK), dtype=bool))
    w_in_b, w_out_ab_b = w_in_ab.astype(BF16), w_out_ab.astype(BF16)
    w_up_b, w_down_b = w_mlp_up.astype(BF16), w_mlp_down.astype(BF16)
    w_qkv_b, w_out_c_b = w_qkv_c.astype(BF16), w_out_c.astype(BF16)
    n_c = w_qkv_b.shape[0]
    wq5 = w_qkv_b.reshape(n_c, D_MODEL, 3, N_GROUPS, C_WIDTH)
    wk_groups = jnp.transpose(wq5[:, :, 1], (0, 2, 1, 3))
    wqv_t_groups = jnp.transpose(wq5[:, :, 0::2], (0, 3, 2, 4, 1)).reshape(n_c, N_GROUPS, 2 * C_WIDTH, D_MODEL)
    wkv_t_groups = jnp.transpose(wq5[:, :, 1:], (0, 3, 2, 4, 1)).reshape(n_c, N_GROUPS, 2 * C_WIDTH, D_MODEL)
    conv_p, conv_s, chunk_v_s = [], [], []
    kv_p = [[] for _ in range(N_GROUPS)]
    qkv_s_layers, rolled = [], None
    mlp_steps = batch * seq // ROW_TILE
    assert DEPTH * mlp_steps == caches[0].shape[0] * n_seq * 2

    for layer in range(DEPTH):
        i = layer // 2
        if layer % 2 == 0:
            w_tril = jnp.where(causal[None], w_spatial[i], 0.0)
            bmix = jnp.repeat(b_spatial[i].T, CHUNK, axis=1)
            shared = (_Layer(w_in_b, i), _Layer(ln_v_g3, i), _Layer(ln_v_b3, i))
            tail = (_Layer(conv_w, i), _Layer(w_out_ab_b, i), _Layer(ln1_g3, layer), _Layer(ln1_b3, layer))
            xp, buf_p = _ab_layer_prompt(xp, batch, seq,
                                         shared + (w_tril.astype(BF16), bmix) + tail)
            eye = jnp.eye(n_seq, dtype=F32)
            w_blk = jnp.stack([jnp.kron(eye, w_tril[g, :t_new, :t_new]) for g in range(G_A)])
            bmix_s = jnp.tile(bmix[:t_new], (n_seq, 1))
            st = state_conv[i]
            s1 = jnp.concatenate([st[:, 1:2], jnp.zeros((n_seq, t_new - 1, W_B), F32)], axis=1)
            s2 = jnp.concatenate([st, jnp.zeros((n_seq, t_new - 2, W_B), F32)], axis=1)
            xs, hc_s, v_s = _ab_layer_sample(xs, s1.reshape(n_tok_s, W_B), s2.reshape(n_tok_s, W_B),
                                             shared + (w_blk.astype(BF16), bmix_s) + tail)
            conv_p.append(buf_p)
            conv_s.append(hc_s.reshape(n_seq, t_new, W_B)[:, t_new - 2:])
            chunk_v_s.append(v_s.reshape(n_seq, t_new, W_A))
        else:
            wo = _Layer(w_out_c_b, i)
            g1, b1 = _Layer(ln1_g3, layer), _Layer(ln1_b3, layer)
            tails = _kv_tail(xp, _Layer(wkv_t_groups, i), batch, seq)
            for g in range(N_GROUPS):
                kv_p[g].append(tails[g].reshape(batch, 2, H_G, HEAD_DIM, WINDOWS[g]))
            k_all, qt_all, vt_all = _qkv_proj_perm(xp, wk_groups, wqv_t_groups, i, batch, seq)
            outs, lses = zip(*[_attn_prompt(k_all, qt_all, vt_all, batch, seq, g) for g in range(N_GROUPS)])
            xp = _merge_layer(xp, outs, lses, wo, g1, b1, DILATIONS)
            qkv_s = _qkv_proj(xs, _Layer(w_qkv_b, i))
            qkv_s_layers.append(qkv_s)
            outs, lses = zip(*[_attn_sample(qkv_s, caches[g], i, g, t_new) for g in range(N_GROUPS)])
            xs = _merge_layer(xs, outs, lses, wo, g1, b1, (1,) * N_GROUPS)
        w_up, w_down = _Layer(w_up_b, layer), _Layer(w_down_b, layer)
        g2, b2 = _Layer(ln2_g3, layer), _Layer(ln2_b3, layer)
        xp, rolled = _mlp_layer(xp, w_up, w_down, g2, b2, roll=(caches, rolled, layer * mlp_steps, t_new))
        xs = _mlp_layer(xs, w_up, w_down, g2, b2)

    for i, qkv_s in enumerate(qkv_s_layers):
        rolled = [_patch_rolled(rolled[g], qkv_s, i, g, t_new) for g in range(N_GROUPS)]
    kv_s = [jnp.transpose(nc, (0, 1, 5, 2, 3, 4)) for nc in rolled]
    return (xp.reshape(batch, seq, D_MODEL), xs.reshape(n_seq, t_new, D_MODEL),
            jnp.stack(conv_p), jnp.stack(conv_s), jnp.stack(chunk_v_s),
            *[jnp.transpose(jnp.stack(kv_p[g]), (0, 1, 5, 2, 3, 4)) for g in range(N_GROUPS)],
            kv_s[0], kv_s[1], kv_s[2])
```

```python
import functools
import math
from typing import NamedTuple

import jax
import jax.numpy as jnp
from jax import lax
from jax.experimental import pallas as pl
from jax.experimental.pallas import tpu as pltpu

F32 = jnp.float32
BF16 = jnp.bfloat16

D_MODEL = 1024
DEPTH = 4
CHUNK = 128
W_A = 512
G_A = 4
W_B = 512
N_GROUPS = 3
WINDOWS = (128, 512, 2048)
DILATIONS = (1, 4, 16)
N_BACK = 128
H_G = 8
HEAD_DIM = 64
C_WIDTH = H_G * HEAD_DIM
QKV_COLS = N_GROUPS * C_WIDTH
QKV_BLOCKS = 3 * N_GROUPS
D_FF = 4 * D_MODEL
ALPHA = (2.0 * DEPTH) ** 0.25
LN_EPS = 1e-5
NEG_INF = -1e30
SLOPES = tuple(2.0 ** (-(8.0 / H_G) * j) for j in range(1, H_G + 1))
QK_SCALE = HEAD_DIM ** -0.5
LOG2E = math.log2(math.e)
LN2 = math.log(2.0)

VMEM_LIMIT_BYTES = 52 * 1024 * 1024
ROW_TILE = 512
PERM_TILE = CHUNK * max(DILATIONS)
X_SLABS = D_MODEL // 128
MERGE_TILE = 512
AB_PARTS = 1
SAMPLE_STEP_POSITIONS = 1024
LSE_LANES = 128


def _layer_norm(x, g, b):
    mu = jnp.mean(x, axis=-1, keepdims=True)
    xc = x - mu
    var = jnp.mean(xc * xc, axis=-1, keepdims=True)
    return xc * lax.rsqrt(var + LN_EPS) * g + b


def _gelu(x):
    c = math.sqrt(2.0 / math.pi)
    return x * (0.5 * (1.0 + jnp.tanh(c * (x + 0.044715 * (x * x * x)))))


class _Layer(NamedTuple):
    stacked: jax.Array
    index: int


def _resident(w):
    if isinstance(w, _Layer):
        shape = w.stacked.shape[1:]
        return w.stacked, pl.BlockSpec((None,) + shape, lambda *_: (w.index,) + (0,) * len(shape),
                                       pipeline_mode=pl.Buffered(1))
    return w, pl.BlockSpec(w.shape, lambda *_: (0,) * w.ndim, pipeline_mode=pl.Buffered(1))


def _resident_all(ws):
    arrays, specs = zip(*[_resident(w) for w in ws])
    return list(arrays), list(specs)


def _params(n_grid):
    return pltpu.CompilerParams(dimension_semantics=("arbitrary",) * n_grid,
                                vmem_limit_bytes=VMEM_LIMIT_BYTES)


def _ab_kernel(*refs, tm, chunk, sample, n_parts):
    if sample:
        (x_ref, s1_ref, s2_ref, w_in_ref, lvg_ref, lvb_ref, wmix_ref, bmix_ref, cw_ref, w_out_ref,
         g1_ref, b1_ref, o_ref, hc_ref, v_ref, y_scr) = refs
    else:
        (x_ref, w_in_ref, lvg_ref, lvb_ref, wmix_ref, bmix_ref, cw_ref, w_out_ref,
         g1_ref, b1_ref, o_ref, cb_ref, y_scr, h_scr) = refs

    if not sample:
        @pl.when(pl.program_id(1) == 0)
        def _():
            h_scr[0:8, :] = jnp.zeros((8, W_B), F32)

    def part(r0, n):
        rows_p = slice(r0, r0 + n)
        x = x_ref[rows_p, :]
        xb = x.astype(BF16)

        def proj(k):
            return jnp.dot(xb, w_in_ref[:, k * W_A:(k + 1) * W_A], preferred_element_type=F32)

        u = _gelu(proj(0))
        v = _layer_norm(_gelu(proj(1)), lvg_ref[...], lvb_ref[...])
        if sample:
            v_ref[...] = v
        vb = v.astype(BF16)
        for c in range(n // chunk):
            rows = slice(c * chunk, (c + 1) * chunk)
            for g in range(G_A):
                cols = slice(g * 128, (g + 1) * 128)
                y_scr[r0 + c * chunk:r0 + (c + 1) * chunk, cols] = (
                    jnp.dot(wmix_ref[g], vb[rows, cols], preferred_element_type=F32) + bmix_ref[:, cols])
        a_out = (u * y_scr[rows_p, :]).astype(BF16)

        hc = proj(3) * proj(4)
        if sample:
            hc_ref[...] = hc
            pos = lax.broadcasted_iota(jnp.int32, hc.shape, 0) & 7
            sh1 = jnp.where(pos == 0, s1_ref[...], pltpu.roll(hc, 1, 0))
            sh2 = jnp.where(pos < 2, s2_ref[...], pltpu.roll(hc, 2, 0))
        else:
            h_scr[8 + r0:8 + r0 + n, :] = hc
            sh1 = h_scr[7 + r0:7 + r0 + n, :]
            sh2 = h_scr[6 + r0:6 + r0 + n, :]
        conv = cw_ref[0:1, :] * sh2 + cw_ref[1:2, :] * sh1 + cw_ref[2:3, :] * hc
        b_out = (proj(2) * conv).astype(BF16)

        mix = (jnp.dot(a_out, w_out_ref[0:W_A, :], preferred_element_type=F32)
               + jnp.dot(b_out, w_out_ref[W_A:W_A + W_B, :], preferred_element_type=F32))
        o_ref[rows_p, :] = _layer_norm(ALPHA * x + mix, g1_ref[...], b1_ref[...])

    for p in range(n_parts):
        part(p * (tm // n_parts), tm // n_parts)
    if not sample:
        cb_ref[...] = h_scr[tm + 6:tm + 8, :]
        h_scr[0:8, :] = h_scr[tm:tm + 8, :]


def _ab_layer_prompt(x, batch, seq, weights):
    tm = ROW_TILE
    nt = seq // tm
    row_spec = pl.BlockSpec((tm, D_MODEL), lambda b, t: (b * nt + t, 0))
    weights, w_specs = _resident_all(weights)
    return pl.pallas_call(
        functools.partial(_ab_kernel, tm=tm, chunk=CHUNK, sample=False, n_parts=AB_PARTS),
        grid=(batch, nt),
        in_specs=[row_spec] + w_specs,
        out_specs=[row_spec, pl.BlockSpec((None, 2, W_B), lambda b, t: (b, 0, 0))],
        out_shape=[jax.ShapeDtypeStruct((batch * seq, D_MODEL), F32),
                   jax.ShapeDtypeStruct((batch, 2, W_B), F32)],
        scratch_shapes=[pltpu.VMEM((tm, W_A), F32), pltpu.VMEM((tm + 8, W_B), F32)],
        compiler_params=_params(2),
        name="ab_layer_prompt",
    )(x, *weights)


def _ab_layer_sample(x, s1, s2, weights):
    tm = x.shape[0]
    full = lambda cols: pl.BlockSpec((tm, cols), lambda i: (0, 0))
    weights, w_specs = _resident_all(weights)
    return pl.pallas_call(
        functools.partial(_ab_kernel, tm=tm, chunk=tm, sample=True, n_parts=1),
        grid=(1,),
        in_specs=[full(D_MODEL), full(W_B), full(W_B)] + w_specs,
        out_specs=[full(D_MODEL), full(W_B), full(W_A)],
        out_shape=[jax.ShapeDtypeStruct((tm, D_MODEL), F32),
                   jax.ShapeDtypeStruct((tm, W_B), F32),
                   jax.ShapeDtypeStruct((tm, W_A), F32)],
        scratch_shapes=[pltpu.VMEM((tm, W_A), F32)],
        compiler_params=_params(1),
        name="ab_layer_sample",
    )(x, s1, s2, *weights)


def _mlp_kernel(x_ref, xs_ref, w_up_ref, w_down_ref, g_ref, b_ref, *rest, n_roll, roll_by):
    old_refs, new_refs = rest[:n_roll], rest[len(rest) - n_roll:]
    o_ref, os_ref = rest[len(rest) - n_roll - 2], rest[len(rest) - n_roll - 1]
    for old, new in zip(old_refs, new_refs):
        buf_len = old.shape[-1]
        for h in range(H_G):
            new[h] = pltpu.roll(old[h], buf_len - roll_by, 1)

    def mlp_rows(x):
        xb = x.astype(BF16)
        acc = None
        for c in range(D_FF // D_MODEL):
            cols = slice(c * D_MODEL, (c + 1) * D_MODEL)
            h = jnp.maximum(jnp.dot(xb, w_up_ref[:, cols], preferred_element_type=F32), 0.0)
            part = jnp.dot((h * h).astype(BF16), w_down_ref[cols, :], preferred_element_type=F32)
            acc = part if acc is None else acc + part
        return _layer_norm(ALPHA * x + acc, g_ref[...], b_ref[...])

    o_ref[...] = mlp_rows(x_ref[...])

    @pl.when(pl.program_id(0) == pl.num_programs(0) - 1)
    def _():
        os_ref[...] = mlp_rows(xs_ref[...])


def _mlp_layer(x, x_small, w_up, w_down, g, b, roll):
    rows = x.shape[0]
    tm = min(ROW_TILE, rows)
    row_spec = pl.BlockSpec((tm, D_MODEL), lambda t: (t, 0))
    small_spec = pl.BlockSpec(x_small.shape, lambda t: (0, 0))
    weights, w_specs = _resident_all((w_up, w_down, g, b))
    bufs, rolled, first, roll_by = roll
    n_roll = len(bufs)
    n_seq = bufs[0].shape[1]

    def block_spec(buf):
        def index(t):
            blk = first + t
            return (blk // (2 * n_seq), (blk % (2 * n_seq)) // 2, blk % 2, 0, 0, 0)
        return pl.BlockSpec((None, None, None) + buf.shape[3:], index)

    args = [x, x_small] + weights + list(bufs)
    in_specs = [row_spec, small_spec] + w_specs + [block_spec(buf) for buf in bufs]
    aliases = {}
    if rolled is not None:
        aliases = {len(args) + k: 2 + k for k in range(n_roll)}
        args += list(rolled)
        in_specs += [pl.BlockSpec(memory_space=pl.ANY)] * n_roll
    outs = pl.pallas_call(
        functools.partial(_mlp_kernel, n_roll=n_roll, roll_by=roll_by),
        grid=(rows // tm,),
        in_specs=in_specs,
        out_specs=[row_spec, small_spec] + [block_spec(buf) for buf in bufs],
        out_shape=[jax.ShapeDtypeStruct(x.shape, F32), jax.ShapeDtypeStruct(x_small.shape, F32)]
                  + [jax.ShapeDtypeStruct(buf.shape, buf.dtype) for buf in bufs],
        input_output_aliases=aliases,
        compiler_params=_params(1),
        name="mlp_layer",
    )(*args)
    return outs[0], outs[1], list(outs[2:])


def _qkv_kernel(x_ref, w_ref, o_ref):
    xb = x_ref[...].astype(BF16)
    for c in range(QKV_BLOCKS):
        cols = slice(c * C_WIDTH, (c + 1) * C_WIDTH)
        o_ref[:, cols] = jnp.dot(xb, w_ref[:, cols], preferred_element_type=F32)


def _qkv_proj(x, w):
    rows = x.shape[0]
    tm = min(256, rows)
    w, w_spec = _resident(w)
    return pl.pallas_call(
        _qkv_kernel,
        grid=(rows // tm,),
        in_specs=[pl.BlockSpec((tm, D_MODEL), lambda t: (t, 0)), w_spec],
        out_specs=pl.BlockSpec((tm, 3 * QKV_COLS), lambda t: (t, 0)),
        out_shape=jax.ShapeDtypeStruct((rows, 3 * QKV_COLS), F32),
        compiler_params=_params(1),
        name="qkv_proj",
    )(x, w)


def _qkv_perm_kernel(*refs):
    x_refs, (wk_ref, wqv_t_ref, k_ref, qt_ref, vt_ref, xp_scr) = refs[:X_SLABS], refs[X_SLABS:]
    nt_dims = (((1,), (1,)), ((), ()))

    def step(d):
        n = PERM_TILE // d
        for c in range(PERM_TILE // ROW_TILE):
            rows = slice(c * ROW_TILE, (c + 1) * ROW_TILE)
            for r in range(d):
                lo, hi = max(r * n, rows.start), min((r + 1) * n, rows.stop)
                if lo >= hi:
                    continue
                for s in range(X_SLABS):
                    src = x_refs[s]
                    vals = (src[lo:hi, :] if d == 1 else src[pl.ds(r + (lo - r * n) * d, hi - lo, stride=d), :])
                    xp_scr[lo:hi, s * 128:(s + 1) * 128] = vals.astype(BF16)
            xs = xp_scr[rows, :]
            k_ref[rows, :] = jnp.dot(xs, wk_ref[...], preferred_element_type=F32).astype(BF16)
            qv_t = lax.dot_general(wqv_t_ref[...], xs, nt_dims, preferred_element_type=F32)
            qt_ref[:, rows] = (qv_t[0:C_WIDTH] * (QK_SCALE * LOG2E)).astype(BF16)
            vt_ref[:, rows] = qv_t[C_WIDTH:2 * C_WIDTH].astype(BF16)

    for g in range(N_GROUPS):
        pl.when(pl.program_id(2) == g)(functools.partial(step, DILATIONS[g]))


def _qkv_proj_perm(x, wk_groups, wqv_t_groups, layer_idx, batch, seq):
    nt = seq // PERM_TILE
    slab = lambda s: pl.BlockSpec((PERM_TILE, 128), lambda b, t, g: (b * nt + t, s))
    chan_spec = pl.BlockSpec((None, C_WIDTH, PERM_TILE), lambda b, t, g: (g, 0, b * nt + t))
    chan_shape = jax.ShapeDtypeStruct((N_GROUPS, C_WIDTH, batch * seq), BF16)
    return pl.pallas_call(
        _qkv_perm_kernel,
        grid=(batch, nt, N_GROUPS),
        in_specs=[slab(s) for s in range(X_SLABS)]
                 + [pl.BlockSpec((None, None, D_MODEL, C_WIDTH), lambda b, t, g: (layer_idx, g, 0, 0)),
                    pl.BlockSpec((None, None, 2 * C_WIDTH, D_MODEL), lambda b, t, g: (layer_idx, g, 0, 0))],
        out_specs=[pl.BlockSpec((None, PERM_TILE, C_WIDTH), lambda b, t, g: (g, b * nt + t, 0)),
                   chan_spec, chan_spec],
        out_shape=[jax.ShapeDtypeStruct((N_GROUPS, batch * seq, C_WIDTH), BF16), chan_shape, chan_shape],
        scratch_shapes=[pltpu.VMEM((PERM_TILE, D_MODEL), BF16)],
        compiler_params=_params(3),
        name="qkv_proj_perm",
    )(*([x] * X_SLABS), wk_groups, wqv_t_groups)


def _kv_tail_kernel(x_ref, wt_ref, o0_ref, o1_ref, o2_ref):
    xb = x_ref[...].astype(BF16)
    nt = (((1,), (1,)), ((), ()))
    o2_ref[...] = lax.dot_general(wt_ref[2], xb, nt, preferred_element_type=F32)

    @pl.when(pl.program_id(1) == pl.num_programs(1) - 1)
    def _():
        o1_ref[...] = lax.dot_general(wt_ref[1], xb, nt, preferred_element_type=F32)
        o0_ref[...] = lax.dot_general(wt_ref[0], xb[ROW_TILE - WINDOWS[0]:, :], nt, preferred_element_type=F32)


def _kv_tail(x, wt_groups, batch, seq):
    nj = WINDOWS[2] // ROW_TILE
    first = (seq - WINDOWS[2]) // ROW_TILE
    nblk = seq // ROW_TILE
    last = lambda n: pl.BlockSpec((None, 2 * C_WIDTH, n), lambda b, j: (b, 0, 0))
    wt_groups, wt_spec = _resident(wt_groups)
    return pl.pallas_call(
        _kv_tail_kernel,
        grid=(batch, nj),
        in_specs=[pl.BlockSpec((ROW_TILE, D_MODEL), lambda b, j: (b * nblk + first + j, 0)), wt_spec],
        out_specs=[last(WINDOWS[0]), last(WINDOWS[1]),
                   pl.BlockSpec((None, 2 * C_WIDTH, ROW_TILE), lambda b, j: (b, 0, j))],
        out_shape=[jax.ShapeDtypeStruct((batch, 2 * C_WIDTH, n), F32) for n in WINDOWS],
        compiler_params=_params(2),
        name="kv_tail",
    )(x, wt_groups)


def _attn_kernel(qt_ref, kc_ref, kp_ref, vtc_ref, vtp_ref, o_ref, lse_ref, bias_scr, *, dil, chain):
    first_step = (pl.program_id(0) == 0) & (pl.program_id(1) == 0) & (pl.program_id(2) == 0)

    @pl.when(first_step)
    def _():
        kj = lax.broadcasted_iota(jnp.int32, (CHUNK, CHUNK), 0)
        qi = lax.broadcasted_iota(jnp.int32, (CHUNK, CHUNK), 1)
        back_own = qi - kj
        back_prev = back_own + CHUNK
        for h in range(H_G):
            slope = SLOPES[h] * dil * LOG2E
            bias_scr[0, h] = jnp.where(back_prev <= N_BACK, -slope * back_prev.astype(F32), NEG_INF)
            bias_scr[1, h] = jnp.full((CHUNK, CHUNK), NEG_INF, F32)
            bias_scr[2, h] = jnp.where(back_own >= 0, -slope * back_own.astype(F32), NEG_INF)

    n_sub = ROW_TILE // CHUNK
    if chain:
        run_steps = PERM_TILE // dil // ROW_TILE
        at_start = (pl.program_id(1) == 0) & (pl.program_id(2) % run_steps == 0)
    else:
        at_start = pl.program_id(1) == 0
    zeros_half = jnp.zeros((HEAD_DIM, CHUNK), BF16)
    lse_pad = jnp.zeros((CHUNK - H_G, CHUNK), F32)

    def half_tile(k_rows, qt_h, vt_cols, bias):
        st = jnp.dot(k_rows, qt_h, preferred_element_type=F32) + bias
        m = jnp.max(st, axis=0, keepdims=True)
        p = jnp.exp2(st - m)
        l = jnp.sum(p, axis=0, keepdims=True)
        return m, l, jnp.dot(vt_cols, p.astype(BF16), preferred_element_type=F32)

    for s in range(n_sub):
        own = slice(s * CHUNK, (s + 1) * CHUNK)
        if chain and s > 0:
            prev_k, prev_vt = kc_ref, vtc_ref
            prv = slice((s - 1) * CHUNK, s * CHUNK)
        else:
            prev_k, prev_vt = kp_ref, vtp_ref
            prv = own if not chain else slice(0, CHUNK)
        masked = at_start if (s == 0 or not chain) else None
        lses = []
        for pair in range(H_G // 2):
            lanes = slice(pair * 2 * HEAD_DIM, (pair + 1) * 2 * HEAD_DIM)
            qt_pair = qt_ref[lanes, own]
            outs = []
            for half in range(2):
                h = 2 * pair + half
                chans = slice(h * HEAD_DIM, (h + 1) * HEAD_DIM)
                qt_h = (jnp.concatenate([qt_pair[0:HEAD_DIM], zeros_half], axis=0) if half == 0
                        else jnp.concatenate([zeros_half, qt_pair[HEAD_DIM:]], axis=0))
                bias_prev = bias_scr[0, h] if masked is None else bias_scr[jnp.where(masked, 1, 0), h]
                m_a, l_a, o_a = half_tile(prev_k[prv, lanes], qt_h, prev_vt[chans, prv], bias_prev)
                m_b, l_b, o_b = half_tile(kc_ref[own, lanes], qt_h, vtc_ref[chans, own], bias_scr[2, h])
                m = jnp.maximum(m_a, m_b)
                w_a, w_b = jnp.exp2(m_a - m), jnp.exp2(m_b - m)
                l = w_a * l_a + w_b * l_b
                outs.append((w_a * o_a + w_b * o_b) * (1.0 / l))
                lses.append((m + jnp.log2(l)) * LN2)
            o_ref[own, lanes] = jnp.concatenate(outs, axis=0).T
        lse_ref[own, :] = jnp.concatenate(lses + [lse_pad], axis=0).T


def _attn_prompt(k_all, qt_all, vt_all, batch, seq, g):
    d = DILATIONS[g]
    nt = seq // PERM_TILE
    nj = PERM_TILE // ROW_TILE
    sub_per_step = ROW_TILE // CHUNK
    run = PERM_TILE // d
    chain = run > CHUNK
    assert run % ROW_TILE == 0 if chain else run == CHUNK
    cur = lambda b, t, j: (b * nt + t) * nj + j
    if chain:
        back = (PERM_TILE - run) // CHUNK + 1
        prev_rows = CHUNK
        prev = lambda b, t, j: jnp.maximum(cur(b, t, j) * sub_per_step - back, 0)
    else:
        prev_rows = ROW_TILE
        prev = lambda b, t, j: jnp.maximum(cur(b, t, j) - nj, 0)
    tok = lambda rows, idx: pl.BlockSpec((None, rows, C_WIDTH), lambda b, t, j: (g, idx(b, t, j), 0))
    chn = lambda cols, idx: pl.BlockSpec((None, C_WIDTH, cols), lambda b, t, j: (g, 0, idx(b, t, j)))
    return pl.pallas_call(
        functools.partial(_attn_kernel, dil=d, chain=chain),
        grid=(batch, nt, nj),
        in_specs=[chn(ROW_TILE, cur), tok(ROW_TILE, cur), tok(prev_rows, prev),
                  chn(ROW_TILE, cur), chn(prev_rows, prev)],
        out_specs=[pl.BlockSpec((ROW_TILE, C_WIDTH), lambda b, t, j: (cur(b, t, j), 0)),
                   pl.BlockSpec((ROW_TILE, LSE_LANES), lambda b, t, j: (cur(b, t, j), 0))],
        out_shape=[jax.ShapeDtypeStruct((batch * seq, C_WIDTH), F32),
                   jax.ShapeDtypeStruct((batch * seq, LSE_LANES), F32)],
        scratch_shapes=[pltpu.VMEM((3, H_G, CHUNK, CHUNK), F32)],
        compiler_params=_params(3),
        name=f"attn_prompt_d{d}",
    )(qt_all, k_all, k_all, vt_all, vt_all)


def _sattn_kernel(*refs, buf_len, dil, t_new, roll):
    q_ref, kn_ref, vn_ref, cache_ref = refs[:4]
    bias_c, bias_n = refs[-2:]
    o_ref, lse_ref = refs[-5:-3] if roll else refs[-4:-2]
    newc_ref = refs[-3] if roll else None
    new_lane0 = CHUNK - t_new

    @pl.when(pl.program_id(0) == 0)
    def _():
        def bias(dist, ok, h):
            b0 = jnp.where((dist & (dil - 1)) == 0, -SLOPES[h] * dist.astype(F32), NEG_INF)
            b0 = jnp.where(dist >= 0, b0, NEG_INF)
            b0 = jnp.where(dist <= N_BACK * dil, b0, NEG_INF)
            return b0 if ok is None else jnp.where(ok, b0, NEG_INF)
        t_c = lax.broadcasted_iota(jnp.int32, (t_new, buf_len), 0)
        p_c = lax.broadcasted_iota(jnp.int32, (t_new, buf_len), 1)
        t_n = lax.broadcasted_iota(jnp.int32, (t_new, CHUNK), 0)
        j_n = lax.broadcasted_iota(jnp.int32, (t_new, CHUNK), 1)
        for h in range(H_G):
            bias_c[h] = bias(buf_len + t_c - p_c, None, h)
            bias_n[h] = bias(t_n - (j_n - new_lane0), j_n >= new_lane0, h)

    pad = jnp.zeros((new_lane0, C_WIDTH), F32)
    nt = (((1,), (1,)), ((), ()))

    for sb in range(cache_ref.shape[0]):
        rows = slice(sb * t_new, (sb + 1) * t_new)
        kn_t = jnp.concatenate([pad, kn_ref[rows, :]], axis=0).T
        vn_t = jnp.concatenate([pad, vn_ref[rows, :]], axis=0).T
        qb = (q_ref[rows, :] * QK_SCALE).astype(BF16)
        outs, lses = [], []
        for h in range(H_G):
            cols = slice(h * HEAD_DIM, (h + 1) * HEAD_DIM)
            k_t, v_t = cache_ref[sb, 0, h], cache_ref[sb, 1, h]
            knh_t, vnh_t = kn_t[cols], vn_t[cols]
            qh = qb[:, cols]
            lc = jnp.dot(qh, k_t.astype(BF16), preferred_element_type=F32) + bias_c[h]
            ln = jnp.dot(qh, knh_t.astype(BF16), preferred_element_type=F32) + bias_n[h]
            m = jnp.maximum(jnp.max(lc, axis=-1, keepdims=True), jnp.max(ln, axis=-1, keepdims=True))
            pc = jnp.exp(lc - m)
            pn = jnp.exp(ln - m)
            l = jnp.sum(pc, axis=-1, keepdims=True) + jnp.sum(pn, axis=-1, keepdims=True)
            o = (lax.dot_general(pc.astype(BF16), v_t.astype(BF16), nt, preferred_element_type=F32)
                 + lax.dot_general(pn.astype(BF16), vnh_t.astype(BF16), nt, preferred_element_type=F32))
            outs.append(o / l)
            lses.append(jnp.broadcast_to(m + jnp.log(l), (t_new, HEAD_DIM)))
            if roll:
                lane = lax.broadcasted_iota(jnp.int32, (HEAD_DIM, CHUNK), 1)
                for kv, old, new in ((0, k_t, knh_t), (1, v_t, vnh_t)):
                    rolled = pltpu.roll(old, buf_len - t_new, 1)
                    if buf_len > CHUNK:
                        newc_ref[sb, kv, h, :, 0:buf_len - CHUNK] = rolled[:, 0:buf_len - CHUNK]
                    newc_ref[sb, kv, h, :, buf_len - CHUNK:buf_len] = jnp.where(
                        lane >= new_lane0, new, rolled[:, buf_len - CHUNK:buf_len])
        o_ref[rows, :] = jnp.concatenate(outs, axis=-1)
        lse_ref[rows, :] = jnp.concatenate(lses, axis=-1)


def _attn_sample(qkv, cache, layer_idx, g, t_new, roll, prev_out=None):
    n_layers, n_seq, _, _, _, buf_len = cache.shape
    d = DILATIONS[g]
    n_sb = max(1, min(n_seq, SAMPLE_STEP_POSITIONS // buf_len))
    assert n_seq % n_sb == 0
    row = lambda sec: pl.BlockSpec((n_sb * t_new, C_WIDTH), lambda b: (b, sec * N_GROUPS + g))
    cache_spec = pl.BlockSpec((None, n_sb, 2, H_G, HEAD_DIM, buf_len), lambda b: (layer_idx, b, 0, 0, 0, 0))
    out_row = pl.BlockSpec((n_sb * t_new, C_WIDTH), lambda b: (b, 0))
    in_specs, args, aliases = [row(0), row(1), row(2), cache_spec], [qkv, qkv, qkv, cache], {}
    out_specs = [out_row, out_row]
    out_shape = [jax.ShapeDtypeStruct((n_seq * t_new, C_WIDTH), F32)] * 2
    if roll:
        out_specs.append(cache_spec)
        out_shape.append(jax.ShapeDtypeStruct(cache.shape, F32))
        if prev_out is not None:
            in_specs.append(pl.BlockSpec(memory_space=pl.ANY))
            args.append(prev_out)
            aliases = {4: 2}
    return pl.pallas_call(
        functools.partial(_sattn_kernel, buf_len=buf_len, dil=d, t_new=t_new, roll=roll),
        grid=(n_seq // n_sb,),
        in_specs=in_specs,
        out_specs=out_specs,
        out_shape=out_shape,
        scratch_shapes=[pltpu.VMEM((H_G, t_new, buf_len), F32), pltpu.VMEM((H_G, t_new, CHUNK), F32)],
        input_output_aliases=aliases,
        compiler_params=_params(1),
        name=f"attn_sample_d{d}",
    )(*args)


def _patch_kernel(kn_ref, vn_ref, tail_ref, out_ref, *, t_new):
    new_lane0 = CHUNK - t_new
    pad = jnp.zeros((new_lane0, C_WIDTH), F32)
    lane = lax.broadcasted_iota(jnp.int32, (HEAD_DIM, CHUNK), 1)
    for sb in range(tail_ref.shape[0]):
        rows = slice(sb * t_new, (sb + 1) * t_new)
        for kv, new_ref in ((0, kn_ref), (1, vn_ref)):
            new_t = jnp.concatenate([pad, new_ref[rows, :]], axis=0).T
            for h in range(H_G):
                out_ref[sb, kv, h] = jnp.where(lane >= new_lane0, new_t[h * HEAD_DIM:(h + 1) * HEAD_DIM],
                                               tail_ref[sb, kv, h])


def _patch_rolled(rolled, qkv, layer_idx, g, t_new):
    n_layers, n_seq, _, _, _, buf_len = rolled.shape
    n_sb = min(n_seq, 8)
    assert n_seq % n_sb == 0
    row = lambda sec: pl.BlockSpec((n_sb * t_new, C_WIDTH), lambda b: (b, sec * N_GROUPS + g))
    tail_spec = pl.BlockSpec((None, n_sb, 2, H_G, HEAD_DIM, CHUNK),
                             lambda b: (layer_idx, b, 0, 0, 0, buf_len // CHUNK - 1))
    return pl.pallas_call(
        functools.partial(_patch_kernel, t_new=t_new),
        grid=(n_seq // n_sb,),
        in_specs=[row(1), row(2), tail_spec],
        out_specs=tail_spec,
        out_shape=jax.ShapeDtypeStruct(rolled.shape, rolled.dtype),
        input_output_aliases={2: 0},
        compiler_params=_params(1),
        name=f"patch_rolled_d{DILATIONS[g]}",
    )(qkv, qkv, rolled)


def _merge_kernel(*refs, tm, dils):
    x_ref = refs[0]
    ol_refs = refs[1:1 + 2 * N_GROUPS]
    w_ref, g_ref, b_ref, out_ref = refs[1 + 2 * N_GROUPS:5 + 2 * N_GROUPS]
    scrs = list(refs[5 + 2 * N_GROUPS:])

    def natural(ref, d):
        if d == 1:
            return lambda rows: ref[rows, :]
        scr = scrs.pop()
        n = tm // d
        slabs = ref.shape[-1] // 128
        for r in range(d):
            for s in range(slabs):
                scr[s, pl.ds(r, n, stride=d), :] = ref[r, :, s * 128:(s + 1) * 128]
        return lambda rows: jnp.concatenate([scr[s, rows, :] for s in range(slabs)], axis=-1)

    loaders = [natural(ol_refs[k], dils[k % N_GROUPS]) for k in range(2 * N_GROUPS)]
    compact = ol_refs[N_GROUPS].shape[-1] != C_WIDTH
    if compact:
        lane = lax.broadcasted_iota(jnp.int32, (2 * LSE_LANES, C_WIDTH), 0) & (LSE_LANES - 1)
        chan = lax.broadcasted_iota(jnp.int32, (2 * LSE_LANES, C_WIDTH), 1)
        spread = jnp.where((chan >> (HEAD_DIM.bit_length() - 1)) == lane, 1.0, 0.0).astype(BF16)

        def expand(a):
            hi = a.astype(BF16)
            lo = (a - hi.astype(F32)).astype(BF16)
            return jnp.dot(jnp.concatenate([hi, lo], axis=-1), spread, preferred_element_type=F32)

    o0, o1, o2, l0, l1, l2 = [load(slice(0, tm)) for load in loaders]
    m = jnp.maximum(jnp.maximum(l0, l1), l2)
    e0, e1, e2 = jnp.exp(l0 - m), jnp.exp(l1 - m), jnp.exp(l2 - m)
    inv = 1.0 / (e0 + e1 + e2)
    alphas = [e0 * inv, e1 * inv, e2 * inv]
    if compact:
        alphas = [expand(a) for a in alphas]
    o = alphas[0] * o0 + alphas[1] * o1 + alphas[2] * o2
    mix = jnp.dot(o.astype(BF16), w_ref[...], preferred_element_type=F32)
    out_ref[...] = _layer_norm(ALPHA * x_ref[...] + mix, g_ref[...], b_ref[...])


def _merge_layer(x, outs, lses, w_out, g, b, dils):
    rows = x.shape[0]
    tm = min(MERGE_TILE, rows)
    per_tile = PERM_TILE // tm
    row_spec = lambda cols: pl.BlockSpec((tm, cols), lambda t: (t, 0))

    def group_arg(a, d):
        cols = a.shape[-1]
        if d == 1:
            return a, row_spec(cols), None
        run = PERM_TILE // d
        view = a.reshape(rows // PERM_TILE, d, run, cols)
        spec = pl.BlockSpec((None, d, tm // d, cols), lambda t: (t // per_tile, 0, t % per_tile, 0))
        return view, spec, pltpu.VMEM((cols // 128, tm, 128), F32)

    args, specs, scratch = zip(*[group_arg(a, dils[k % N_GROUPS]) for k, a in enumerate(list(outs) + list(lses))])
    scratch = [s for s in scratch if s is not None][::-1]
    weights, w_specs = _resident_all((w_out, g, b))
    return pl.pallas_call(
        functools.partial(_merge_kernel, tm=tm, dils=dils),
        grid=(rows // tm,),
        in_specs=[row_spec(D_MODEL)] + list(specs) + w_specs,
        out_specs=row_spec(D_MODEL),
        out_shape=jax.ShapeDtypeStruct((rows, D_MODEL), F32),
        scratch_shapes=scratch,
        compiler_params=_params(1),
        name="merge_layer",
    )(x, *args, *weights)


def kernel(x_prompt, x_sample, state_conv, cache_kv_w128, cache_kv_w512, cache_kv_w2048, w_in_ab, ln_v_g, ln_v_b, w_spatial, b_spatial, conv_w, w_out_ab, w_qkv_c, w_out_c, ln1_g, ln1_b, ln2_g, ln2_b, w_mlp_up, w_mlp_down):
    batch, seq, _ = x_prompt.shape
    n_seq, t_new, _ = x_sample.shape
    n_tok_s = n_seq * t_new
    xp = x_prompt.reshape(batch * seq, D_MODEL)
    xs = x_sample.reshape(n_tok_s, D_MODEL)
    caches = [jnp.transpose(c, (0, 1, 3, 4, 5, 2)) for c in (cache_kv_w128, cache_kv_w512, cache_kv_w2048)]
    rows3 = lambda a: a.reshape(a.shape[0], 1, a.shape[1])
    ln_v_g3, ln_v_b3, ln1_g3, ln1_b3, ln2_g3, ln2_b3 = map(rows3, (ln_v_g, ln_v_b, ln1_g, ln1_b, ln2_g, ln2_b))

    causal = jnp.tril(jnp.ones((CHUNK, CHUNK), dtype=bool))
    w_in_b, w_out_ab_b = w_in_ab.astype(BF16), w_out_ab.astype(BF16)
    w_up_b, w_down_b = w_mlp_up.astype(BF16), w_mlp_down.astype(BF16)
    w_qkv_b, w_out_c_b = w_qkv_c.astype(BF16), w_out_c.astype(BF16)
    n_c = w_qkv_b.shape[0]
    wq5 = w_qkv_b.reshape(n_c, D_MODEL, 3, N_GROUPS, C_WIDTH)
    wk_groups = jnp.transpose(wq5[:, :, 1], (0, 2, 1, 3))
    wqv_t_groups = jnp.transpose(wq5[:, :, 0::2], (0, 3, 2, 4, 1)).reshape(n_c, N_GROUPS, 2 * C_WIDTH, D_MODEL)
    wkv_t_groups = jnp.transpose(wq5[:, :, 1:], (0, 3, 2, 4, 1)).reshape(n_c, N_GROUPS, 2 * C_WIDTH, D_MODEL)
    conv_p, conv_s, chunk_v_s = [], [], []
    kv_p = [[] for _ in range(N_GROUPS)]
    qkv_s_layers, rolled = [], None
    new_caches = [None] * N_GROUPS
    hosted = [g for g in range(N_GROUPS) if caches[g].shape[-1] > SAMPLE_STEP_POSITIONS]
    mlp_steps = batch * seq // ROW_TILE
    assert DEPTH * mlp_steps == caches[0].shape[0] * n_seq * 2

    for layer in range(DEPTH):
        i = layer // 2
        if layer % 2 == 0:
            w_tril = jnp.where(causal[None], w_spatial[i], 0.0)
            bmix = jnp.repeat(b_spatial[i].T, CHUNK, axis=1)
            shared = (_Layer(w_in_b, i), _Layer(ln_v_g3, i), _Layer(ln_v_b3, i))
            tail = (_Layer(conv_w, i), _Layer(w_out_ab_b, i), _Layer(ln1_g3, layer), _Layer(ln1_b3, layer))
            xp, buf_p = _ab_layer_prompt(xp, batch, seq,
                                         shared + (w_tril.astype(BF16), bmix) + tail)
            eye = jnp.eye(n_seq, dtype=F32)
            w_blk = jnp.stack([jnp.kron(eye, w_tril[g, :t_new, :t_new]) for g in range(G_A)])
            bmix_s = jnp.tile(bmix[:t_new], (n_seq, 1))
            st = state_conv[i]
            s1 = jnp.concatenate([st[:, 1:2], jnp.zeros((n_seq, t_new - 1, W_B), F32)], axis=1)
            s2 = jnp.concatenate([st, jnp.zeros((n_seq, t_new - 2, W_B), F32)], axis=1)
            xs, hc_s, v_s = _ab_layer_sample(xs, s1.reshape(n_tok_s, W_B), s2.reshape(n_tok_s, W_B),
                                             shared + (w_blk.astype(BF16), bmix_s) + tail)
            conv_p.append(buf_p)
            conv_s.append(hc_s.reshape(n_seq, t_new, W_B)[:, t_new - 2:])
            chunk_v_s.append(v_s.reshape(n_seq, t_new, W_A))
        else:
            wo = _Layer(w_out_c_b, i)
            g1, b1 = _Layer(ln1_g3, layer), _Layer(ln1_b3, layer)
            tails = _kv_tail(xp, _Layer(wkv_t_groups, i), batch, seq)
            for g in range(N_GROUPS):
                kv_p[g].append(tails[g].reshape(batch, 2, H_G, HEAD_DIM, WINDOWS[g]))
            k_all, qt_all, vt_all = _qkv_proj_perm(xp, wk_groups, wqv_t_groups, i, batch, seq)
            outs, lses = zip(*[_attn_prompt(k_all, qt_all, vt_all, batch, seq, g) for g in range(N_GROUPS)])
            xp = _merge_layer(xp, outs, lses, wo, g1, b1, DILATIONS)
            qkv_s = _qkv_proj(xs, _Layer(w_qkv_b, i))
            qkv_s_layers.append(qkv_s)
            outs, lses = [], []
            for g in range(N_GROUPS):
                res = _attn_sample(qkv_s, caches[g], i, g, t_new, g not in hosted, new_caches[g])
                outs.append(res[0])
                lses.append(res[1])
                if g not in hosted:
                    new_caches[g] = res[2]
            xs = _merge_layer(xs, outs, lses, wo, g1, b1, (1,) * N_GROUPS)
        w_up, w_down = _Layer(w_up_b, layer), _Layer(w_down_b, layer)
        g2, b2 = _Layer(ln2_g3, layer), _Layer(ln2_b3, layer)
        xp, xs, rolled = _mlp_layer(xp, xs, w_up, w_down, g2, b2,
                                    roll=([caches[g] for g in hosted], rolled, layer * mlp_steps, t_new))

    for i, qkv_s in enumerate(qkv_s_layers):
        rolled = [_patch_rolled(rolled[k], qkv_s, i, g, t_new) for k, g in enumerate(hosted)]
    for k, g in enumerate(hosted):
        new_caches[g] = rolled[k]
    kv_s = [jnp.transpose(nc, (0, 1, 5, 2, 3, 4)) for nc in new_caches]
    return (xp.reshape(batch, seq, D_MODEL), xs.reshape(n_seq, t_new, D_MODEL),
            jnp.stack(conv_p), jnp.stack(conv_s), jnp.stack(chunk_v_s),
            *[jnp.transpose(jnp.stack(kv_p[g]), (0, 1, 5, 2, 3, 4)) for g in range(N_GROUPS)],
            kv_s[0], kv_s[1], kv_s[2])
```

```python
import functools
import math
from typing import NamedTuple

import jax
import jax.numpy as jnp
from jax import lax
from jax.experimental import pallas as pl
from jax.experimental.pallas import tpu as pltpu

F32 = jnp.float32
BF16 = jnp.bfloat16

D_MODEL = 1024
DEPTH = 4
CHUNK = 128
W_A = 512
G_A = 4
W_B = 512
N_GROUPS = 3
WINDOWS = (128, 512, 2048)
DILATIONS = (1, 4, 16)
N_BACK = 128
H_G = 8
HEAD_DIM = 64
C_WIDTH = H_G * HEAD_DIM
QKV_COLS = N_GROUPS * C_WIDTH
QKV_BLOCKS = 3 * N_GROUPS
D_FF = 4 * D_MODEL
ALPHA = (2.0 * DEPTH) ** 0.25
LN_EPS = 1e-5
NEG_INF = -1e30
SLOPES = tuple(2.0 ** (-(8.0 / H_G) * j) for j in range(1, H_G + 1))
QK_SCALE = HEAD_DIM ** -0.5
LOG2E = math.log2(math.e)
LN2 = math.log(2.0)

VMEM_LIMIT_BYTES = 52 * 1024 * 1024
ROW_TILE = 512
PERM_TILE = CHUNK * max(DILATIONS)
X_SLABS = D_MODEL // 128
MERGE_TILE = 512
AB_PARTS = 1
SAMPLE_STEP_POSITIONS = 1024
LSE_LANES = 128


def _layer_norm(x, g, b):
    mu = jnp.mean(x, axis=-1, keepdims=True)
    xc = x - mu
    var = jnp.mean(xc * xc, axis=-1, keepdims=True)
    return xc * lax.rsqrt(var + LN_EPS) * g + b


def _gelu(x):
    c = math.sqrt(2.0 / math.pi)
    return x * (0.5 * (1.0 + jnp.tanh(c * (x + 0.044715 * (x * x * x)))))


class _Layer(NamedTuple):
    stacked: jax.Array
    index: int


def _resident(w):
    if isinstance(w, _Layer):
        shape = w.stacked.shape[1:]
        return w.stacked, pl.BlockSpec((None,) + shape, lambda *_: (w.index,) + (0,) * len(shape),
                                       pipeline_mode=pl.Buffered(1))
    return w, pl.BlockSpec(w.shape, lambda *_: (0,) * w.ndim, pipeline_mode=pl.Buffered(1))


def _resident_all(ws):
    arrays, specs = zip(*[_resident(w) for w in ws])
    return list(arrays), list(specs)


def _params(n_grid):
    return pltpu.CompilerParams(dimension_semantics=("arbitrary",) * n_grid,
                                vmem_limit_bytes=VMEM_LIMIT_BYTES)


def _ab_kernel(*refs, tm, chunk, sample, n_parts):
    if sample:
        (x_ref, s1_ref, s2_ref, w_in_ref, lvg_ref, lvb_ref, wmix_ref, bmix_ref, cw_ref, w_out_ref,
         g1_ref, b1_ref, o_ref, hc_ref, v_ref, y_scr) = refs
    else:
        (x_ref, w_in_ref, lvg_ref, lvb_ref, wmix_ref, bmix_ref, cw_ref, w_out_ref, g1_ref, b1_ref) = refs[:10]
        n_cast = (len(refs) - 14) // 2
        cast_in, cast_out = refs[10:10 + n_cast], refs[12 + n_cast:12 + 2 * n_cast]
        o_ref, cb_ref = refs[10 + n_cast:12 + n_cast]
        y_scr, h_scr = refs[-2:]
        for src, dst in zip(cast_in, cast_out):
            dst[...] = src[...].astype(BF16)

    if not sample:
        @pl.when(pl.program_id(1) == 0)
        def _():
            h_scr[0:8, :] = jnp.zeros((8, W_B), F32)

    def part(r0, n):
        rows_p = slice(r0, r0 + n)
        x = x_ref[rows_p, :]
        xb = x.astype(BF16)

        def proj(k):
            return jnp.dot(xb, w_in_ref[:, k * W_A:(k + 1) * W_A], preferred_element_type=F32)

        u = _gelu(proj(0))
        v = _layer_norm(_gelu(proj(1)), lvg_ref[...], lvb_ref[...])
        if sample:
            v_ref[...] = v
        vb = v.astype(BF16)
        for c in range(n // chunk):
            rows = slice(c * chunk, (c + 1) * chunk)
            for g in range(G_A):
                cols = slice(g * 128, (g + 1) * 128)
                y_scr[r0 + c * chunk:r0 + (c + 1) * chunk, cols] = (
                    jnp.dot(wmix_ref[g], vb[rows, cols], preferred_element_type=F32) + bmix_ref[:, cols])
        a_out = (u * y_scr[rows_p, :]).astype(BF16)

        hc = proj(3) * proj(4)
        if sample:
            hc_ref[...] = hc
            pos = lax.broadcasted_iota(jnp.int32, hc.shape, 0) & 7
            sh1 = jnp.where(pos == 0, s1_ref[...], pltpu.roll(hc, 1, 0))
            sh2 = jnp.where(pos < 2, s2_ref[...], pltpu.roll(hc, 2, 0))
        else:
            h_scr[8 + r0:8 + r0 + n, :] = hc
            sh1 = h_scr[7 + r0:7 + r0 + n, :]
            sh2 = h_scr[6 + r0:6 + r0 + n, :]
        conv = cw_ref[0:1, :] * sh2 + cw_ref[1:2, :] * sh1 + cw_ref[2:3, :] * hc
        b_out = (proj(2) * conv).astype(BF16)

        mix = (jnp.dot(a_out, w_out_ref[0:W_A, :], preferred_element_type=F32)
               + jnp.dot(b_out, w_out_ref[W_A:W_A + W_B, :], preferred_element_type=F32))
        o_ref[rows_p, :] = _layer_norm(ALPHA * x + mix, g1_ref[...], b1_ref[...])

    for p in range(n_parts):
        part(p * (tm // n_parts), tm // n_parts)
    if not sample:
        cb_ref[...] = h_scr[tm + 6:tm + 8, :]
        h_scr[0:8, :] = h_scr[tm:tm + 8, :]


def _ab_layer_prompt(x, batch, seq, weights, cast=()):
    tm = ROW_TILE
    nt = seq // tm
    steps = batch * nt
    row_spec = pl.BlockSpec((tm, D_MODEL), lambda b, t: (b * nt + t, 0))
    weights, w_specs = _resident_all(weights)
    cast_specs = [pl.BlockSpec((a.shape[0] // steps, a.shape[1]), lambda b, t: (b * nt + t, 0)) for a in cast]
    assert all(a.shape[0] % (16 * steps) == 0 for a in cast)
    outs = pl.pallas_call(
        functools.partial(_ab_kernel, tm=tm, chunk=CHUNK, sample=False, n_parts=AB_PARTS),
        grid=(batch, nt),
        in_specs=[row_spec] + w_specs + cast_specs,
        out_specs=[row_spec, pl.BlockSpec((None, 2, W_B), lambda b, t: (b, 0, 0))] + cast_specs,
        out_shape=[jax.ShapeDtypeStruct((batch * seq, D_MODEL), F32),
                   jax.ShapeDtypeStruct((batch, 2, W_B), F32)]
                  + [jax.ShapeDtypeStruct(a.shape, BF16) for a in cast],
        scratch_shapes=[pltpu.VMEM((tm, W_A), F32), pltpu.VMEM((tm + 8, W_B), F32)],
        compiler_params=_params(2),
        name="ab_layer_prompt",
    )(x, *weights, *cast)
    return outs[0], outs[1], list(outs[2:])


def _ab_layer_sample(x, s1, s2, weights):
    tm = x.shape[0]
    full = lambda cols: pl.BlockSpec((tm, cols), lambda i: (0, 0))
    weights, w_specs = _resident_all(weights)
    return pl.pallas_call(
        functools.partial(_ab_kernel, tm=tm, chunk=tm, sample=True, n_parts=1),
        grid=(1,),
        in_specs=[full(D_MODEL), full(W_B), full(W_B)] + w_specs,
        out_specs=[full(D_MODEL), full(W_B), full(W_A)],
        out_shape=[jax.ShapeDtypeStruct((tm, D_MODEL), F32),
                   jax.ShapeDtypeStruct((tm, W_B), F32),
                   jax.ShapeDtypeStruct((tm, W_A), F32)],
        scratch_shapes=[pltpu.VMEM((tm, W_A), F32)],
        compiler_params=_params(1),
        name="ab_layer_sample",
    )(x, s1, s2, *weights)


def _mlp_kernel(x_ref, xs_ref, w_up_ref, w_down_ref, g_ref, b_ref, *rest, n_roll, roll_by):
    old_refs, new_refs = rest[:n_roll], rest[len(rest) - n_roll:]
    o_ref, os_ref = rest[len(rest) - n_roll - 2], rest[len(rest) - n_roll - 1]
    for old, new in zip(old_refs, new_refs):
        buf_len = old.shape[-1]
        for h in range(H_G):
            new[h] = pltpu.roll(old[h], buf_len - roll_by, 1)

    def mlp_rows(x):
        xb = x.astype(BF16)
        acc = None
        for c in range(D_FF // D_MODEL):
            cols = slice(c * D_MODEL, (c + 1) * D_MODEL)
            h = jnp.maximum(jnp.dot(xb, w_up_ref[:, cols], preferred_element_type=F32), 0.0)
            part = jnp.dot((h * h).astype(BF16), w_down_ref[cols, :], preferred_element_type=F32)
            acc = part if acc is None else acc + part
        return _layer_norm(ALPHA * x + acc, g_ref[...], b_ref[...])

    o_ref[...] = mlp_rows(x_ref[...])

    @pl.when(pl.program_id(0) == pl.num_programs(0) - 1)
    def _():
        os_ref[...] = mlp_rows(xs_ref[...])


def _mlp_layer(x, x_small, w_up, w_down, g, b, roll):
    rows = x.shape[0]
    tm = min(ROW_TILE, rows)
    row_spec = pl.BlockSpec((tm, D_MODEL), lambda t: (t, 0))
    small_spec = pl.BlockSpec(x_small.shape, lambda t: (0, 0))
    weights, w_specs = _resident_all((w_up, w_down, g, b))
    bufs, rolled, first, roll_by = roll
    n_roll = len(bufs)
    n_seq = bufs[0].shape[1]

    def block_spec(buf):
        def index(t):
            blk = first + t
            return (blk // (2 * n_seq), (blk % (2 * n_seq)) // 2, blk % 2, 0, 0, 0)
        return pl.BlockSpec((None, None, None) + buf.shape[3:], index)

    args = [x, x_small] + weights + list(bufs)
    in_specs = [row_spec, small_spec] + w_specs + [block_spec(buf) for buf in bufs]
    aliases = {}
    if rolled is not None:
        aliases = {len(args) + k: 2 + k for k in range(n_roll)}
        args += list(rolled)
        in_specs += [pl.BlockSpec(memory_space=pl.ANY)] * n_roll
    outs = pl.pallas_call(
        functools.partial(_mlp_kernel, n_roll=n_roll, roll_by=roll_by),
        grid=(rows // tm,),
        in_specs=in_specs,
        out_specs=[row_spec, small_spec] + [block_spec(buf) for buf in bufs],
        out_shape=[jax.ShapeDtypeStruct(x.shape, F32), jax.ShapeDtypeStruct(x_small.shape, F32)]
                  + [jax.ShapeDtypeStruct(buf.shape, buf.dtype) for buf in bufs],
        input_output_aliases=aliases,
        compiler_params=_params(1),
        name="mlp_layer",
    )(*args)
    return outs[0], outs[1], list(outs[2:])


def _qkv_kernel(x_ref, w_ref, o_ref):
    xb = x_ref[...].astype(BF16)
    for c in range(QKV_BLOCKS):
        cols = slice(c * C_WIDTH, (c + 1) * C_WIDTH)
        o_ref[:, cols] = jnp.dot(xb, w_ref[:, cols], preferred_element_type=F32)


def _qkv_proj(x, w):
    rows = x.shape[0]
    tm = min(256, rows)
    w, w_spec = _resident(w)
    return pl.pallas_call(
        _qkv_kernel,
        grid=(rows // tm,),
        in_specs=[pl.BlockSpec((tm, D_MODEL), lambda t: (t, 0)), w_spec],
        out_specs=pl.BlockSpec((tm, 3 * QKV_COLS), lambda t: (t, 0)),
        out_shape=jax.ShapeDtypeStruct((rows, 3 * QKV_COLS), F32),
        compiler_params=_params(1),
        name="qkv_proj",
    )(x, w)


def _qkv_perm_kernel(*refs):
    x_refs, (wk_ref, wqv_t_ref, k_ref, qt_ref, vt_ref, xp_scr) = refs[:X_SLABS], refs[X_SLABS:]
    nt_dims = (((1,), (1,)), ((), ()))

    def step(d):
        n = PERM_TILE // d
        for c in range(PERM_TILE // ROW_TILE):
            rows = slice(c * ROW_TILE, (c + 1) * ROW_TILE)
            for r in range(d):
                lo, hi = max(r * n, rows.start), min((r + 1) * n, rows.stop)
                if lo >= hi:
                    continue
                for s in range(X_SLABS):
                    src = x_refs[s]
                    vals = (src[lo:hi, :] if d == 1 else src[pl.ds(r + (lo - r * n) * d, hi - lo, stride=d), :])
                    xp_scr[lo:hi, s * 128:(s + 1) * 128] = vals.astype(BF16)
            xs = xp_scr[rows, :]
            k_ref[rows, :] = jnp.dot(xs, wk_ref[...], preferred_element_type=F32).astype(BF16)
            qv_t = lax.dot_general(wqv_t_ref[...], xs, nt_dims, preferred_element_type=F32)
            qt_ref[:, rows] = (qv_t[0:C_WIDTH] * (QK_SCALE * LOG2E)).astype(BF16)
            vt_ref[:, rows] = qv_t[C_WIDTH:2 * C_WIDTH].astype(BF16)

    for g in range(N_GROUPS):
        pl.when(pl.program_id(2) == g)(functools.partial(step, DILATIONS[g]))


def _qkv_proj_perm(x, wk_groups, wqv_t_groups, layer_idx, batch, seq):
    nt = seq // PERM_TILE
    slab = lambda s: pl.BlockSpec((PERM_TILE, 128), lambda b, t, g: (b * nt + t, s))
    chan_spec = pl.BlockSpec((None, C_WIDTH, PERM_TILE), lambda b, t, g: (g, 0, b * nt + t))
    chan_shape = jax.ShapeDtypeStruct((N_GROUPS, C_WIDTH, batch * seq), BF16)
    return pl.pallas_call(
        _qkv_perm_kernel,
        grid=(batch, nt, N_GROUPS),
        in_specs=[slab(s) for s in range(X_SLABS)]
                 + [pl.BlockSpec((None, None, D_MODEL, C_WIDTH), lambda b, t, g: (layer_idx, g, 0, 0)),
                    pl.BlockSpec((None, None, 2 * C_WIDTH, D_MODEL), lambda b, t, g: (layer_idx, g, 0, 0))],
        out_specs=[pl.BlockSpec((None, PERM_TILE, C_WIDTH), lambda b, t, g: (g, b * nt + t, 0)),
                   chan_spec, chan_spec],
        out_shape=[jax.ShapeDtypeStruct((N_GROUPS, batch * seq, C_WIDTH), BF16), chan_shape, chan_shape],
        scratch_shapes=[pltpu.VMEM((PERM_TILE, D_MODEL), BF16)],
        compiler_params=_params(3),
        name="qkv_proj_perm",
    )(*([x] * X_SLABS), wk_groups, wqv_t_groups)


def _kv_tail_kernel(x_ref, wt_ref, o0_ref, o1_ref, o2_ref):
    xb = x_ref[...].astype(BF16)
    nt = (((1,), (1,)), ((), ()))
    o2_ref[...] = lax.dot_general(wt_ref[2], xb, nt, preferred_element_type=F32)

    @pl.when(pl.program_id(1) == pl.num_programs(1) - 1)
    def _():
        o1_ref[...] = lax.dot_general(wt_ref[1], xb, nt, preferred_element_type=F32)
        o0_ref[...] = lax.dot_general(wt_ref[0], xb[ROW_TILE - WINDOWS[0]:, :], nt, preferred_element_type=F32)


def _kv_tail(x, wt_groups, batch, seq):
    nj = WINDOWS[2] // ROW_TILE
    first = (seq - WINDOWS[2]) // ROW_TILE
    nblk = seq // ROW_TILE
    last = lambda n: pl.BlockSpec((None, 2 * C_WIDTH, n), lambda b, j: (b, 0, 0))
    wt_groups, wt_spec = _resident(wt_groups)
    return pl.pallas_call(
        _kv_tail_kernel,
        grid=(batch, nj),
        in_specs=[pl.BlockSpec((ROW_TILE, D_MODEL), lambda b, j: (b * nblk + first + j, 0)), wt_spec],
        out_specs=[last(WINDOWS[0]), last(WINDOWS[1]),
                   pl.BlockSpec((None, 2 * C_WIDTH, ROW_TILE), lambda b, j: (b, 0, j))],
        out_shape=[jax.ShapeDtypeStruct((batch, 2 * C_WIDTH, n), F32) for n in WINDOWS],
        compiler_params=_params(2),
        name="kv_tail",
    )(x, wt_groups)


def _attn_kernel(qt_ref, kc_ref, kp_ref, vtc_ref, vtp_ref, o_ref, lse_ref, bias_scr, *, dil, chain):
    first_step = (pl.program_id(0) == 0) & (pl.program_id(1) == 0) & (pl.program_id(2) == 0)

    @pl.when(first_step)
    def _():
        kj = lax.broadcasted_iota(jnp.int32, (CHUNK, CHUNK), 0)
        qi = lax.broadcasted_iota(jnp.int32, (CHUNK, CHUNK), 1)
        back_own = qi - kj
        back_prev = back_own + CHUNK
        for h in range(H_G):
            slope = SLOPES[h] * dil * LOG2E
            bias_scr[0, h] = jnp.where(back_prev <= N_BACK, -slope * back_prev.astype(F32), NEG_INF)
            bias_scr[1, h] = jnp.full((CHUNK, CHUNK), NEG_INF, F32)
            bias_scr[2, h] = jnp.where(back_own >= 0, -slope * back_own.astype(F32), NEG_INF)

    n_sub = ROW_TILE // CHUNK
    if chain:
        run_steps = PERM_TILE // dil // ROW_TILE
        at_start = (pl.program_id(1) == 0) & (pl.program_id(2) % run_steps == 0)
    else:
        at_start = pl.program_id(1) == 0
    zeros_half = jnp.zeros((HEAD_DIM, CHUNK), BF16)
    lse_pad = jnp.zeros((CHUNK - H_G, CHUNK), F32)

    def half_tile(k_rows, qt_h, vt_cols, bias):
        st = jnp.dot(k_rows, qt_h, preferred_element_type=F32) + bias
        m = jnp.max(st, axis=0, keepdims=True)
        p = jnp.exp2(st - m)
        l = jnp.sum(p, axis=0, keepdims=True)
        return m, l, jnp.dot(vt_cols, p.astype(BF16), preferred_element_type=F32)

    for s in range(n_sub):
        own = slice(s * CHUNK, (s + 1) * CHUNK)
        if chain and s > 0:
            prev_k, prev_vt = kc_ref, vtc_ref
            prv = slice((s - 1) * CHUNK, s * CHUNK)
        else:
            prev_k, prev_vt = kp_ref, vtp_ref
            prv = own if not chain else slice(0, CHUNK)
        masked = at_start if (s == 0 or not chain) else None
        lses = []
        for pair in range(H_G // 2):
            lanes = slice(pair * 2 * HEAD_DIM, (pair + 1) * 2 * HEAD_DIM)
            qt_pair = qt_ref[lanes, own]
            outs = []
            for half in range(2):
                h = 2 * pair + half
                chans = slice(h * HEAD_DIM, (h + 1) * HEAD_DIM)
                qt_h = (jnp.concatenate([qt_pair[0:HEAD_DIM], zeros_half], axis=0) if half == 0
                        else jnp.concatenate([zeros_half, qt_pair[HEAD_DIM:]], axis=0))
                bias_prev = bias_scr[0, h] if masked is None else bias_scr[jnp.where(masked, 1, 0), h]
                m_a, l_a, o_a = half_tile(prev_k[prv, lanes], qt_h, prev_vt[chans, prv], bias_prev)
                m_b, l_b, o_b = half_tile(kc_ref[own, lanes], qt_h, vtc_ref[chans, own], bias_scr[2, h])
                m = jnp.maximum(m_a, m_b)
                w_a, w_b = jnp.exp2(m_a - m), jnp.exp2(m_b - m)
                l = w_a * l_a + w_b * l_b
                outs.append((w_a * o_a + w_b * o_b) * (1.0 / l))
                lses.append((m + jnp.log2(l)) * LN2)
            o_ref[own, lanes] = jnp.concatenate(outs, axis=0).T
        lse_ref[own, :] = jnp.concatenate(lses + [lse_pad], axis=0).T


def _attn_prompt(k_all, qt_all, vt_all, batch, seq, g):
    d = DILATIONS[g]
    nt = seq // PERM_TILE
    nj = PERM_TILE // ROW_TILE
    sub_per_step = ROW_TILE // CHUNK
    run = PERM_TILE // d
    chain = run > CHUNK
    assert run % ROW_TILE == 0 if chain else run == CHUNK
    cur = lambda b, t, j: (b * nt + t) * nj + j
    if chain:
        back = (PERM_TILE - run) // CHUNK + 1
        prev_rows = CHUNK
        prev = lambda b, t, j: jnp.maximum(cur(b, t, j) * sub_per_step - back, 0)
    else:
        prev_rows = ROW_TILE
        prev = lambda b, t, j: jnp.maximum(cur(b, t, j) - nj, 0)
    tok = lambda rows, idx: pl.BlockSpec((None, rows, C_WIDTH), lambda b, t, j: (g, idx(b, t, j), 0))
    chn = lambda cols, idx: pl.BlockSpec((None, C_WIDTH, cols), lambda b, t, j: (g, 0, idx(b, t, j)))
    return pl.pallas_call(
        functools.partial(_attn_kernel, dil=d, chain=chain),
        grid=(batch, nt, nj),
        in_specs=[chn(ROW_TILE, cur), tok(ROW_TILE, cur), tok(prev_rows, prev),
                  chn(ROW_TILE, cur), chn(prev_rows, prev)],
        out_specs=[pl.BlockSpec((ROW_TILE, C_WIDTH), lambda b, t, j: (cur(b, t, j), 0)),
                   pl.BlockSpec((ROW_TILE, LSE_LANES), lambda b, t, j: (cur(b, t, j), 0))],
        out_shape=[jax.ShapeDtypeStruct((batch * seq, C_WIDTH), F32),
                   jax.ShapeDtypeStruct((batch * seq, LSE_LANES), F32)],
        scratch_shapes=[pltpu.VMEM((3, H_G, CHUNK, CHUNK), F32)],
        compiler_params=_params(3),
        name=f"attn_prompt_d{d}",
    )(qt_all, k_all, k_all, vt_all, vt_all)


def _sattn_kernel(*refs, buf_len, dil, t_new, roll):
    q_ref, kn_ref, vn_ref, cache_ref = refs[:4]
    bias_c, bias_n = refs[-2:]
    o_ref, lse_ref = refs[-5:-3] if roll else refs[-4:-2]
    newc_ref = refs[-3] if roll else None
    new_lane0 = CHUNK - t_new

    @pl.when(pl.program_id(0) == 0)
    def _():
        def bias(dist, ok, h):
            b0 = jnp.where((dist & (dil - 1)) == 0, -SLOPES[h] * dist.astype(F32), NEG_INF)
            b0 = jnp.where(dist >= 0, b0, NEG_INF)
            b0 = jnp.where(dist <= N_BACK * dil, b0, NEG_INF)
            return b0 if ok is None else jnp.where(ok, b0, NEG_INF)
        t_c = lax.broadcasted_iota(jnp.int32, (t_new, buf_len), 0)
        p_c = lax.broadcasted_iota(jnp.int32, (t_new, buf_len), 1)
        t_n = lax.broadcasted_iota(jnp.int32, (t_new, CHUNK), 0)
        j_n = lax.broadcasted_iota(jnp.int32, (t_new, CHUNK), 1)
        for h in range(H_G):
            bias_c[h * t_new:(h + 1) * t_new, :] = bias(buf_len + t_c - p_c, None, h)
            bias_n[h * t_new:(h + 1) * t_new, :] = bias(t_n - (j_n - new_lane0), j_n >= new_lane0, h)

    pad = jnp.zeros((new_lane0, C_WIDTH), F32)
    nt = (((1,), (1,)), ((), ()))
    n_rows = H_G * t_new
    assert t_new & (t_new - 1) == 0 and HEAD_DIM & (HEAD_DIM - 1) == 0
    row_head = lax.broadcasted_iota(jnp.int32, (n_rows, C_WIDTH), 0) >> (t_new.bit_length() - 1)
    col_head = lax.broadcasted_iota(jnp.int32, (n_rows, C_WIDTH), 1) >> (HEAD_DIM.bit_length() - 1)
    own_head = row_head == col_head
    lane = lax.broadcasted_iota(jnp.int32, (HEAD_DIM, CHUNK), 1)

    def per_query(a):
        return jnp.sum(jnp.where(own_head, a, 0.0).reshape(H_G, t_new, C_WIDTH), axis=0)

    for sb in range(cache_ref.shape[0]):
        rows = slice(sb * t_new, (sb + 1) * t_new)
        kn_t = jnp.concatenate([pad, kn_ref[rows, :]], axis=0).T
        vn_t = jnp.concatenate([pad, vn_ref[rows, :]], axis=0).T
        q = q_ref[rows, :] * QK_SCALE
        q_bd = jnp.where(own_head, jnp.concatenate([q] * H_G, axis=0), 0.0).astype(BF16)
        k_t = cache_ref[sb, 0].reshape(C_WIDTH, buf_len)
        v_t = cache_ref[sb, 1].reshape(C_WIDTH, buf_len)
        lc = jnp.dot(q_bd, k_t.astype(BF16), preferred_element_type=F32) + bias_c[...]
        ln = jnp.dot(q_bd, kn_t.astype(BF16), preferred_element_type=F32) + bias_n[...]
        m = jnp.maximum(jnp.max(lc, axis=-1, keepdims=True), jnp.max(ln, axis=-1, keepdims=True))
        pc = jnp.exp(lc - m)
        pn = jnp.exp(ln - m)
        l = jnp.sum(pc, axis=-1, keepdims=True) + jnp.sum(pn, axis=-1, keepdims=True)
        o = (lax.dot_general(pc.astype(BF16), v_t.astype(BF16), nt, preferred_element_type=F32)
             + lax.dot_general(pn.astype(BF16), vn_t.astype(BF16), nt, preferred_element_type=F32))
        o_ref[rows, :] = per_query(o / l)
        lse_ref[rows, :] = per_query(jnp.broadcast_to(m + jnp.log(l), (n_rows, C_WIDTH)))
        if roll:
            for kv, old_t, new_t in ((0, k_t, kn_t), (1, v_t, vn_t)):
                for h in range(H_G):
                    chans = slice(h * HEAD_DIM, (h + 1) * HEAD_DIM)
                    rolled = pltpu.roll(old_t[chans], buf_len - t_new, 1)
                    if buf_len > CHUNK:
                        newc_ref[sb, kv, h, :, 0:buf_len - CHUNK] = rolled[:, 0:buf_len - CHUNK]
                    newc_ref[sb, kv, h, :, buf_len - CHUNK:buf_len] = jnp.where(
                        lane >= new_lane0, new_t[chans], rolled[:, buf_len - CHUNK:buf_len])


def _attn_sample(qkv, cache, layer_idx, g, t_new, roll, prev_out=None):
    n_layers, n_seq, _, _, _, buf_len = cache.shape
    d = DILATIONS[g]
    n_sb = max(1, min(n_seq, SAMPLE_STEP_POSITIONS // buf_len))
    assert n_seq % n_sb == 0
    row = lambda sec: pl.BlockSpec((n_sb * t_new, C_WIDTH), lambda b: (b, sec * N_GROUPS + g))
    cache_spec = pl.BlockSpec((None, n_sb, 2, H_G, HEAD_DIM, buf_len), lambda b: (layer_idx, b, 0, 0, 0, 0))
    out_row = pl.BlockSpec((n_sb * t_new, C_WIDTH), lambda b: (b, 0))
    in_specs, args, aliases = [row(0), row(1), row(2), cache_spec], [qkv, qkv, qkv, cache], {}
    out_specs = [out_row, out_row]
    out_shape = [jax.ShapeDtypeStruct((n_seq * t_new, C_WIDTH), F32)] * 2
    if roll:
        out_specs.append(cache_spec)
        out_shape.append(jax.ShapeDtypeStruct(cache.shape, F32))
        if prev_out is not None:
            in_specs.append(pl.BlockSpec(memory_space=pl.ANY))
            args.append(prev_out)
            aliases = {4: 2}
    return pl.pallas_call(
        functools.partial(_sattn_kernel, buf_len=buf_len, dil=d, t_new=t_new, roll=roll),
        grid=(n_seq // n_sb,),
        in_specs=in_specs,
        out_specs=out_specs,
        out_shape=out_shape,
        scratch_shapes=[pltpu.VMEM((H_G * t_new, buf_len), F32), pltpu.VMEM((H_G * t_new, CHUNK), F32)],
        input_output_aliases=aliases,
        compiler_params=_params(1),
        name=f"attn_sample_d{d}",
    )(*args)


def _patch_kernel(kn_ref, vn_ref, tail_ref, out_ref, *, t_new):
    new_lane0 = CHUNK - t_new
    pad = jnp.zeros((new_lane0, C_WIDTH), F32)
    lane = lax.broadcasted_iota(jnp.int32, (HEAD_DIM, CHUNK), 1)
    for sb in range(tail_ref.shape[0]):
        rows = slice(sb * t_new, (sb + 1) * t_new)
        for kv, new_ref in ((0, kn_ref), (1, vn_ref)):
            new_t = jnp.concatenate([pad, new_ref[rows, :]], axis=0).T
            for h in range(H_G):
                out_ref[sb, kv, h] = jnp.where(lane >= new_lane0, new_t[h * HEAD_DIM:(h + 1) * HEAD_DIM],
                                               tail_ref[sb, kv, h])


def _patch_rolled(rolled, qkv, layer_idx, g, t_new):
    n_layers, n_seq, _, _, _, buf_len = rolled.shape
    n_sb = min(n_seq, 8)
    assert n_seq % n_sb == 0
    row = lambda sec: pl.BlockSpec((n_sb * t_new, C_WIDTH), lambda b: (b, sec * N_GROUPS + g))
    tail_spec = pl.BlockSpec((None, n_sb, 2, H_G, HEAD_DIM, CHUNK),
                             lambda b: (layer_idx, b, 0, 0, 0, buf_len // CHUNK - 1))
    return pl.pallas_call(
        functools.partial(_patch_kernel, t_new=t_new),
        grid=(n_seq // n_sb,),
        in_specs=[row(1), row(2), tail_spec],
        out_specs=tail_spec,
        out_shape=jax.ShapeDtypeStruct(rolled.shape, rolled.dtype),
        input_output_aliases={2: 0},
        compiler_params=_params(1),
        name=f"patch_rolled_d{DILATIONS[g]}",
    )(qkv, qkv, rolled)


def _merge_kernel(*refs, tm, dils):
    x_ref = refs[0]
    ol_refs = refs[1:1 + 2 * N_GROUPS]
    w_ref, g_ref, b_ref, out_ref = refs[1 + 2 * N_GROUPS:5 + 2 * N_GROUPS]
    scrs = list(refs[5 + 2 * N_GROUPS:])

    def natural(ref, d):
        if d == 1:
            return lambda rows: ref[rows, :]
        scr = scrs.pop()
        n = tm // d
        slabs = ref.shape[-1] // 128
        for r in range(d):
            for s in range(slabs):
                scr[s, pl.ds(r, n, stride=d), :] = ref[r, :, s * 128:(s + 1) * 128]
        return lambda rows: jnp.concatenate([scr[s, rows, :] for s in range(slabs)], axis=-1)

    loaders = [natural(ol_refs[k], dils[k % N_GROUPS]) for k in range(2 * N_GROUPS)]
    compact = ol_refs[N_GROUPS].shape[-1] != C_WIDTH
    if compact:
        lane = lax.broadcasted_iota(jnp.int32, (2 * LSE_LANES, C_WIDTH), 0) & (LSE_LANES - 1)
        chan = lax.broadcasted_iota(jnp.int32, (2 * LSE_LANES, C_WIDTH), 1)
        spread = jnp.where((chan >> (HEAD_DIM.bit_length() - 1)) == lane, 1.0, 0.0).astype(BF16)

        def expand(a):
            hi = a.astype(BF16)
            lo = (a - hi.astype(F32)).astype(BF16)
            return jnp.dot(jnp.concatenate([hi, lo], axis=-1), spread, preferred_element_type=F32)

    o0, o1, o2, l0, l1, l2 = [load(slice(0, tm)) for load in loaders]
    m = jnp.maximum(jnp.maximum(l0, l1), l2)
    e0, e1, e2 = jnp.exp(l0 - m), jnp.exp(l1 - m), jnp.exp(l2 - m)
    inv = 1.0 / (e0 + e1 + e2)
    alphas = [e0 * inv, e1 * inv, e2 * inv]
    if compact:
        alphas = [expand(a) for a in alphas]
    o = alphas[0] * o0 + alphas[1] * o1 + alphas[2] * o2
    mix = jnp.dot(o.astype(BF16), w_ref[...], preferred_element_type=F32)
    out_ref[...] = _layer_norm(ALPHA * x_ref[...] + mix, g_ref[...], b_ref[...])


def _merge_layer(x, outs, lses, w_out, g, b, dils):
    rows = x.shape[0]
    tm = min(MERGE_TILE, rows)
    per_tile = PERM_TILE // tm
    row_spec = lambda cols: pl.BlockSpec((tm, cols), lambda t: (t, 0))

    def group_arg(a, d):
        cols = a.shape[-1]
        if d == 1:
            return a, row_spec(cols), None
        run = PERM_TILE // d
        view = a.reshape(rows // PERM_TILE, d, run, cols)
        spec = pl.BlockSpec((None, d, tm // d, cols), lambda t: (t // per_tile, 0, t % per_tile, 0))
        return view, spec, pltpu.VMEM((cols // 128, tm, 128), F32)

    args, specs, scratch = zip(*[group_arg(a, dils[k % N_GROUPS]) for k, a in enumerate(list(outs) + list(lses))])
    scratch = [s for s in scratch if s is not None][::-1]
    weights, w_specs = _resident_all((w_out, g, b))
    return pl.pallas_call(
        functools.partial(_merge_kernel, tm=tm, dils=dils),
        grid=(rows // tm,),
        in_specs=[row_spec(D_MODEL)] + list(specs) + w_specs,
        out_specs=row_spec(D_MODEL),
        out_shape=jax.ShapeDtypeStruct((rows, D_MODEL), F32),
        scratch_shapes=scratch,
        compiler_params=_params(1),
        name="merge_layer",
    )(x, *args, *weights)


def kernel(x_prompt, x_sample, state_conv, cache_kv_w128, cache_kv_w512, cache_kv_w2048, w_in_ab, ln_v_g, ln_v_b, w_spatial, b_spatial, conv_w, w_out_ab, w_qkv_c, w_out_c, ln1_g, ln1_b, ln2_g, ln2_b, w_mlp_up, w_mlp_down):
    batch, seq, _ = x_prompt.shape
    n_seq, t_new, _ = x_sample.shape
    n_tok_s = n_seq * t_new
    xp = x_prompt.reshape(batch * seq, D_MODEL)
    xs = x_sample.reshape(n_tok_s, D_MODEL)
    caches = [jnp.transpose(c, (0, 1, 3, 4, 5, 2)) for c in (cache_kv_w128, cache_kv_w512, cache_kv_w2048)]
    rows3 = lambda a: a.reshape(a.shape[0], 1, a.shape[1])
    ln_v_g3, ln_v_b3, ln1_g3, ln1_b3, ln2_g3, ln2_b3 = map(rows3, (ln_v_g, ln_v_b, ln1_g, ln1_b, ln2_g, ln2_b))

    causal = jnp.tril(jnp.ones((CHUNK, CHUNK), dtype=bool))
    w_in_b, w_out_ab_b, w_out_c_b = w_in_ab.astype(BF16), w_out_ab.astype(BF16), w_out_c.astype(BF16)
    cast_later = (w_mlp_up.reshape(-1, D_FF), w_mlp_down.reshape(-1, D_MODEL), w_qkv_c.reshape(-1, 3 * QKV_COLS))
    conv_p, conv_s, chunk_v_s = [], [], []
    kv_p = [[] for _ in range(N_GROUPS)]
    qkv_s_layers, rolled = [], None
    new_caches = [None] * N_GROUPS
    hosted = [g for g in range(N_GROUPS) if caches[g].shape[-1] > SAMPLE_STEP_POSITIONS]
    mlp_steps = batch * seq // ROW_TILE
    assert DEPTH * mlp_steps == caches[0].shape[0] * n_seq * 2

    for layer in range(DEPTH):
        i = layer // 2
        if layer % 2 == 0:
            w_tril = jnp.where(causal[None], w_spatial[i], 0.0)
            bmix = jnp.repeat(b_spatial[i].T, CHUNK, axis=1)
            shared = (_Layer(w_in_b, i), _Layer(ln_v_g3, i), _Layer(ln_v_b3, i))
            tail = (_Layer(conv_w, i), _Layer(w_out_ab_b, i), _Layer(ln1_g3, layer), _Layer(ln1_b3, layer))
            xp, buf_p, cast_done = _ab_layer_prompt(xp, batch, seq, shared + (w_tril.astype(BF16), bmix) + tail,
                                                    cast_later if layer == 0 else ())
            if layer == 0:
                w_up_b = cast_done[0].reshape(w_mlp_up.shape)
                w_down_b = cast_done[1].reshape(w_mlp_down.shape)
                w_qkv_b = cast_done[2].reshape(w_qkv_c.shape)
                n_c = w_qkv_b.shape[0]
                wq5 = w_qkv_b.reshape(n_c, D_MODEL, 3, N_GROUPS, C_WIDTH)
                wk_groups = jnp.transpose(wq5[:, :, 1], (0, 2, 1, 3))
                wqv_t_groups = jnp.transpose(wq5[:, :, 0::2], (0, 3, 2, 4, 1)).reshape(
                    n_c, N_GROUPS, 2 * C_WIDTH, D_MODEL)
                wkv_t_groups = jnp.transpose(wq5[:, :, 1:], (0, 3, 2, 4, 1)).reshape(
                    n_c, N_GROUPS, 2 * C_WIDTH, D_MODEL)
            eye = jnp.eye(n_seq, dtype=F32)
            w_blk = jnp.stack([jnp.kron(eye, w_tril[g, :t_new, :t_new]) for g in range(G_A)])
            bmix_s = jnp.tile(bmix[:t_new], (n_seq, 1))
            st = state_conv[i]
            s1 = jnp.concatenate([st[:, 1:2], jnp.zeros((n_seq, t_new - 1, W_B), F32)], axis=1)
            s2 = jnp.concatenate([st, jnp.zeros((n_seq, t_new - 2, W_B), F32)], axis=1)
            xs, hc_s, v_s = _ab_layer_sample(xs, s1.reshape(n_tok_s, W_B), s2.reshape(n_tok_s, W_B),
                                             shared + (w_blk.astype(BF16), bmix_s) + tail)
            conv_p.append(buf_p)
            conv_s.append(hc_s.reshape(n_seq, t_new, W_B)[:, t_new - 2:])
            chunk_v_s.append(v_s.reshape(n_seq, t_new, W_A))
        else:
            wo = _Layer(w_out_c_b, i)
            g1, b1 = _Layer(ln1_g3, layer), _Layer(ln1_b3, layer)
            tails = _kv_tail(xp, _Layer(wkv_t_groups, i), batch, seq)
            for g in range(N_GROUPS):
                kv_p[g].append(tails[g].reshape(batch, 2, H_G, HEAD_DIM, WINDOWS[g]))
            k_all, qt_all, vt_all = _qkv_proj_perm(xp, wk_groups, wqv_t_groups, i, batch, seq)
            outs, lses = zip(*[_attn_prompt(k_all, qt_all, vt_all, batch, seq, g) for g in range(N_GROUPS)])
            xp = _merge_layer(xp, outs, lses, wo, g1, b1, DILATIONS)
            qkv_s = _qkv_proj(xs, _Layer(w_qkv_b, i))
            qkv_s_layers.append(qkv_s)
            outs, lses = [], []
            for g in range(N_GROUPS):
                res = _attn_sample(qkv_s, caches[g], i, g, t_new, g not in hosted, new_caches[g])
                outs.append(res[0])
                lses.append(res[1])
                if g not in hosted:
                    new_caches[g] = res[2]
            xs = _merge_layer(xs, outs, lses, wo, g1, b1, (1,) * N_GROUPS)
        w_up, w_down = _Layer(w_up_b, layer), _Layer(w_down_b, layer)
        g2, b2 = _Layer(ln2_g3, layer), _Layer(ln2_b3, layer)
        xp, xs, rolled = _mlp_layer(xp, xs, w_up, w_down, g2, b2,
                                    roll=([caches[g] for g in hosted], rolled, layer * mlp_steps, t_new))

    for i, qkv_s in enumerate(qkv_s_layers):
        rolled = [_patch_rolled(rolled[k], qkv_s, i, g, t_new) for k, g in enumerate(hosted)]
    for k, g in enumerate(hosted):
        new_caches[g] = rolled[k]
    kv_s = [jnp.transpose(nc, (0, 1, 5, 2, 3, 4)) for nc in new_caches]
    return (xp.reshape(batch, seq, D_MODEL), xs.reshape(n_seq, t_new, D_MODEL),
            jnp.stack(conv_p), jnp.stack(conv_s), jnp.stack(chunk_v_s),
            *[jnp.transpose(jnp.stack(kv_p[g]), (0, 1, 5, 2, 3, 4)) for g in range(N_GROUPS)],
            kv_s[0], kv_s[1], kv_s[2])
```

```python
import functools
import math
from typing import NamedTuple

import jax
import jax.numpy as jnp
from jax import lax
from jax.experimental import pallas as pl
from jax.experimental.pallas import tpu as pltpu

F32 = jnp.float32
BF16 = jnp.bfloat16

D_MODEL = 1024
DEPTH = 4
CHUNK = 128
W_A = 512
G_A = 4
W_B = 512
N_GROUPS = 3
WINDOWS = (128, 512, 2048)
DILATIONS = (1, 4, 16)
N_BACK = 128
H_G = 8
HEAD_DIM = 64
C_WIDTH = H_G * HEAD_DIM
QKV_COLS = N_GROUPS * C_WIDTH
QKV_BLOCKS = 3 * N_GROUPS
D_FF = 4 * D_MODEL
ALPHA = (2.0 * DEPTH) ** 0.25
LN_EPS = 1e-5
NEG_INF = -1e30
SLOPES = tuple(2.0 ** (-(8.0 / H_G) * j) for j in range(1, H_G + 1))
QK_SCALE = HEAD_DIM ** -0.5
LOG2E = math.log2(math.e)
LN2 = math.log(2.0)

VMEM_LIMIT_BYTES = 52 * 1024 * 1024
ROW_TILE = 512
PERM_TILE = CHUNK * max(DILATIONS)
X_SLABS = D_MODEL // 128
MERGE_TILE = 512
AB_PARTS = 1
SAMPLE_STEP_POSITIONS = 1024
LSE_LANES = 128


def _layer_norm(x, g, b):
    mu = jnp.mean(x, axis=-1, keepdims=True)
    xc = x - mu
    var = jnp.mean(xc * xc, axis=-1, keepdims=True)
    return xc * lax.rsqrt(var + LN_EPS) * g + b


def _gelu(x):
    c = math.sqrt(2.0 / math.pi)
    return x * (0.5 * (1.0 + jnp.tanh(c * (x + 0.044715 * (x * x * x)))))


class _Layer(NamedTuple):
    stacked: jax.Array
    index: int


def _resident(w):
    if isinstance(w, _Layer):
        shape = w.stacked.shape[1:]
        return w.stacked, pl.BlockSpec((None,) + shape, lambda *_: (w.index,) + (0,) * len(shape),
                                       pipeline_mode=pl.Buffered(1))
    return w, pl.BlockSpec(w.shape, lambda *_: (0,) * w.ndim, pipeline_mode=pl.Buffered(1))


def _resident_all(ws):
    arrays, specs = zip(*[_resident(w) for w in ws])
    return list(arrays), list(specs)


def _params(n_grid):
    return pltpu.CompilerParams(dimension_semantics=("arbitrary",) * n_grid,
                                vmem_limit_bytes=VMEM_LIMIT_BYTES)


def _ab_kernel(*refs, tm, chunk, sample, n_parts):
    if sample:
        (x_ref, s1_ref, s2_ref, w_in_ref, lvg_ref, lvb_ref, wmix_ref, bmix_ref, cw_ref, w_out_ref,
         g1_ref, b1_ref, o_ref, hc_ref, v_ref, y_scr) = refs
    else:
        (x_ref, w_in_ref, lvg_ref, lvb_ref, wmix_ref, bmix_ref, cw_ref, w_out_ref, g1_ref, b1_ref) = refs[:10]
        n_cast = (len(refs) - 14) // 2
        cast_in, cast_out = refs[10:10 + n_cast], refs[12 + n_cast:12 + 2 * n_cast]
        o_ref, cb_ref = refs[10 + n_cast:12 + n_cast]
        y_scr, h_scr = refs[-2:]
        for src, dst in zip(cast_in, cast_out):
            dst[...] = src[...].astype(BF16)

    if not sample:
        @pl.when(pl.program_id(1) == 0)
        def _():
            h_scr[0:8, :] = jnp.zeros((8, W_B), F32)

    def part(r0, n):
        rows_p = slice(r0, r0 + n)
        x = x_ref[rows_p, :]
        xb = x.astype(BF16)

        def proj(k):
            return jnp.dot(xb, w_in_ref[:, k * W_A:(k + 1) * W_A], preferred_element_type=F32)

        u = _gelu(proj(0))
        v = _layer_norm(_gelu(proj(1)), lvg_ref[...], lvb_ref[...])
        if sample:
            v_ref[...] = v
        vb = v.astype(BF16)
        for c in range(n // chunk):
            rows = slice(c * chunk, (c + 1) * chunk)
            for g in range(G_A):
                cols = slice(g * 128, (g + 1) * 128)
                y_scr[r0 + c * chunk:r0 + (c + 1) * chunk, cols] = (
                    jnp.dot(wmix_ref[g], vb[rows, cols], preferred_element_type=F32) + bmix_ref[:, cols])
        a_out = (u * y_scr[rows_p, :]).astype(BF16)

        hc = proj(3) * proj(4)
        if sample:
            hc_ref[...] = hc
            pos = lax.broadcasted_iota(jnp.int32, hc.shape, 0) & 7
            sh1 = jnp.where(pos == 0, s1_ref[...], pltpu.roll(hc, 1, 0))
            sh2 = jnp.where(pos < 2, s2_ref[...], pltpu.roll(hc, 2, 0))
        else:
            h_scr[8 + r0:8 + r0 + n, :] = hc
            sh1 = h_scr[7 + r0:7 + r0 + n, :]
            sh2 = h_scr[6 + r0:6 + r0 + n, :]
        conv = cw_ref[0:1, :] * sh2 + cw_ref[1:2, :] * sh1 + cw_ref[2:3, :] * hc
        b_out = (proj(2) * conv).astype(BF16)

        mix = (jnp.dot(a_out, w_out_ref[0:W_A, :], preferred_element_type=F32)
               + jnp.dot(b_out, w_out_ref[W_A:W_A + W_B, :], preferred_element_type=F32))
        o_ref[rows_p, :] = _layer_norm(ALPHA * x + mix, g1_ref[...], b1_ref[...])

    for p in range(n_parts):
        part(p * (tm // n_parts), tm // n_parts)
    if not sample:
        cb_ref[...] = h_scr[tm + 6:tm + 8, :]
        h_scr[0:8, :] = h_scr[tm:tm + 8, :]


def _ab_layer_prompt(x, batch, seq, weights, cast=()):
    tm = ROW_TILE
    nt = seq // tm
    steps = batch * nt
    row_spec = pl.BlockSpec((tm, D_MODEL), lambda b, t: (b * nt + t, 0))
    weights, w_specs = _resident_all(weights)
    cast_specs = [pl.BlockSpec((a.shape[0] // steps, a.shape[1]), lambda b, t: (b * nt + t, 0)) for a in cast]
    assert all(a.shape[0] % (16 * steps) == 0 for a in cast)
    outs = pl.pallas_call(
        functools.partial(_ab_kernel, tm=tm, chunk=CHUNK, sample=False, n_parts=AB_PARTS),
        grid=(batch, nt),
        in_specs=[row_spec] + w_specs + cast_specs,
        out_specs=[row_spec, pl.BlockSpec((None, 2, W_B), lambda b, t: (b, 0, 0))] + cast_specs,
        out_shape=[jax.ShapeDtypeStruct((batch * seq, D_MODEL), F32),
                   jax.ShapeDtypeStruct((batch, 2, W_B), F32)]
                  + [jax.ShapeDtypeStruct(a.shape, BF16) for a in cast],
        scratch_shapes=[pltpu.VMEM((tm, W_A), F32), pltpu.VMEM((tm + 8, W_B), F32)],
        compiler_params=_params(2),
        name="ab_layer_prompt",
    )(x, *weights, *cast)
    return outs[0], outs[1], list(outs[2:])


def _ab_layer_sample(x, s1, s2, weights):
    tm = x.shape[0]
    full = lambda cols: pl.BlockSpec((tm, cols), lambda i: (0, 0))
    weights, w_specs = _resident_all(weights)
    return pl.pallas_call(
        functools.partial(_ab_kernel, tm=tm, chunk=tm, sample=True, n_parts=1),
        grid=(1,),
        in_specs=[full(D_MODEL), full(W_B), full(W_B)] + w_specs,
        out_specs=[full(D_MODEL), full(W_B), full(W_A)],
        out_shape=[jax.ShapeDtypeStruct((tm, D_MODEL), F32),
                   jax.ShapeDtypeStruct((tm, W_B), F32),
                   jax.ShapeDtypeStruct((tm, W_A), F32)],
        scratch_shapes=[pltpu.VMEM((tm, W_A), F32)],
        compiler_params=_params(1),
        name="ab_layer_sample",
    )(x, s1, s2, *weights)


def _mlp_kernel(x_ref, xs_ref, w_up_ref, w_down_ref, g_ref, b_ref, *rest, n_roll, roll_by):
    old_refs, new_refs = rest[:n_roll], rest[len(rest) - n_roll:]
    o_ref, os_ref = rest[len(rest) - n_roll - 2], rest[len(rest) - n_roll - 1]
    for old, new in zip(old_refs, new_refs):
        buf_len = old.shape[-1]
        for h in range(H_G):
            new[h] = pltpu.roll(old[h], buf_len - roll_by, 1)

    def mlp_rows(x):
        xb = x.astype(BF16)
        acc = None
        for c in range(D_FF // D_MODEL):
            cols = slice(c * D_MODEL, (c + 1) * D_MODEL)
            h = jnp.maximum(jnp.dot(xb, w_up_ref[:, cols], preferred_element_type=F32), 0.0)
            part = jnp.dot((h * h).astype(BF16), w_down_ref[cols, :], preferred_element_type=F32)
            acc = part if acc is None else acc + part
        return _layer_norm(ALPHA * x + acc, g_ref[...], b_ref[...])

    o_ref[...] = mlp_rows(x_ref[...])

    @pl.when(pl.program_id(0) == pl.num_programs(0) - 1)
    def _():
        os_ref[...] = mlp_rows(xs_ref[...])


def _mlp_layer(x, x_small, w_up, w_down, g, b, roll):
    rows = x.shape[0]
    tm = min(ROW_TILE, rows)
    row_spec = pl.BlockSpec((tm, D_MODEL), lambda t: (t, 0))
    small_spec = pl.BlockSpec(x_small.shape, lambda t: (0, 0))
    weights, w_specs = _resident_all((w_up, w_down, g, b))
    bufs, rolled, first, roll_by = roll
    n_roll = len(bufs)
    n_seq = bufs[0].shape[1]

    def block_spec(buf):
        def index(t):
            blk = first + t
            return (blk // (2 * n_seq), (blk % (2 * n_seq)) // 2, blk % 2, 0, 0, 0)
        return pl.BlockSpec((None, None, None) + buf.shape[3:], index)

    args = [x, x_small] + weights + list(bufs)
    in_specs = [row_spec, small_spec] + w_specs + [block_spec(buf) for buf in bufs]
    aliases = {}
    if rolled is not None:
        aliases = {len(args) + k: 2 + k for k in range(n_roll)}
        args += list(rolled)
        in_specs += [pl.BlockSpec(memory_space=pl.ANY)] * n_roll
    outs = pl.pallas_call(
        functools.partial(_mlp_kernel, n_roll=n_roll, roll_by=roll_by),
        grid=(rows // tm,),
        in_specs=in_specs,
        out_specs=[row_spec, small_spec] + [block_spec(buf) for buf in bufs],
        out_shape=[jax.ShapeDtypeStruct(x.shape, F32), jax.ShapeDtypeStruct(x_small.shape, F32)]
                  + [jax.ShapeDtypeStruct(buf.shape, buf.dtype) for buf in bufs],
        input_output_aliases=aliases,
        compiler_params=_params(1),
        name="mlp_layer",
    )(*args)
    return outs[0], outs[1], list(outs[2:])


def _qkv_kernel(x_ref, w_ref, o_ref):
    xb = x_ref[...].astype(BF16)
    for c in range(QKV_BLOCKS):
        cols = slice(c * C_WIDTH, (c + 1) * C_WIDTH)
        o_ref[:, cols] = jnp.dot(xb, w_ref[:, cols], preferred_element_type=F32)


def _qkv_proj(x, w):
    rows = x.shape[0]
    tm = min(256, rows)
    w, w_spec = _resident(w)
    return pl.pallas_call(
        _qkv_kernel,
        grid=(rows // tm,),
        in_specs=[pl.BlockSpec((tm, D_MODEL), lambda t: (t, 0)), w_spec],
        out_specs=pl.BlockSpec((tm, 3 * QKV_COLS), lambda t: (t, 0)),
        out_shape=jax.ShapeDtypeStruct((rows, 3 * QKV_COLS), F32),
        compiler_params=_params(1),
        name="qkv_proj",
    )(x, w)


def _qkv_perm_kernel(*refs):
    x_refs, (wk_ref, wqv_t_ref, k_ref, qt_ref, vt_ref, xp_scr) = refs[:X_SLABS], refs[X_SLABS:]
    nt_dims = (((1,), (1,)), ((), ()))

    def step(d):
        n = PERM_TILE // d
        for c in range(PERM_TILE // ROW_TILE):
            rows = slice(c * ROW_TILE, (c + 1) * ROW_TILE)
            for r in range(d):
                lo, hi = max(r * n, rows.start), min((r + 1) * n, rows.stop)
                if lo >= hi:
                    continue
                for s in range(X_SLABS):
                    src = x_refs[s]
                    vals = (src[lo:hi, :] if d == 1 else src[pl.ds(r + (lo - r * n) * d, hi - lo, stride=d), :])
                    xp_scr[lo:hi, s * 128:(s + 1) * 128] = vals.astype(BF16)
            xs = xp_scr[rows, :]
            k_ref[rows, :] = jnp.dot(xs, wk_ref[...], preferred_element_type=F32).astype(BF16)
            qv_t = lax.dot_general(wqv_t_ref[...], xs, nt_dims, preferred_element_type=F32)
            qt_ref[:, rows] = (qv_t[0:C_WIDTH] * (QK_SCALE * LOG2E)).astype(BF16)
            vt_ref[:, rows] = qv_t[C_WIDTH:2 * C_WIDTH].astype(BF16)

    for g in range(N_GROUPS):
        pl.when(pl.program_id(2) == g)(functools.partial(step, DILATIONS[g]))


def _qkv_proj_perm(x, wk_groups, wqv_t_groups, layer_idx, batch, seq):
    nt = seq // PERM_TILE
    slab = lambda s: pl.BlockSpec((PERM_TILE, 128), lambda b, t, g: (b * nt + t, s))
    chan_spec = pl.BlockSpec((None, C_WIDTH, PERM_TILE), lambda b, t, g: (g, 0, b * nt + t))
    chan_shape = jax.ShapeDtypeStruct((N_GROUPS, C_WIDTH, batch * seq), BF16)
    return pl.pallas_call(
        _qkv_perm_kernel,
        grid=(batch, nt, N_GROUPS),
        in_specs=[slab(s) for s in range(X_SLABS)]
                 + [pl.BlockSpec((None, None, D_MODEL, C_WIDTH), lambda b, t, g: (layer_idx, g, 0, 0)),
                    pl.BlockSpec((None, None, 2 * C_WIDTH, D_MODEL), lambda b, t, g: (layer_idx, g, 0, 0))],
        out_specs=[pl.BlockSpec((None, PERM_TILE, C_WIDTH), lambda b, t, g: (g, b * nt + t, 0)),
                   chan_spec, chan_spec],
        out_shape=[jax.ShapeDtypeStruct((N_GROUPS, batch * seq, C_WIDTH), BF16), chan_shape, chan_shape],
        scratch_shapes=[pltpu.VMEM((PERM_TILE, D_MODEL), BF16)],
        compiler_params=_params(3),
        name="qkv_proj_perm",
    )(*([x] * X_SLABS), wk_groups, wqv_t_groups)


def _kv_tail_kernel(*refs):
    n_layers = len(refs) - 4
    x_refs, wt_ref, (o0_ref, o1_ref, o2_ref) = refs[:n_layers], refs[n_layers], refs[n_layers + 1:]
    nt = (((1,), (1,)), ((), ()))
    for layer in range(n_layers):
        @pl.when(pl.program_id(0) == layer)
        def _(x_ref=x_refs[layer]):
            xb = x_ref[...].astype(BF16)
            o2_ref[...] = lax.dot_general(wt_ref[2], xb, nt, preferred_element_type=F32)

            @pl.when(pl.program_id(2) == pl.num_programs(2) - 1)
            def _():
                o1_ref[...] = lax.dot_general(wt_ref[1], xb, nt, preferred_element_type=F32)
                o0_ref[...] = lax.dot_general(wt_ref[0], xb[ROW_TILE - WINDOWS[0]:, :], nt,
                                              preferred_element_type=F32)


def _kv_tail(x_layers, wt_groups, batch, seq):
    n_layers = len(x_layers)
    nj = WINDOWS[2] // ROW_TILE
    first = (seq - WINDOWS[2]) // ROW_TILE
    nblk = seq // ROW_TILE
    last = lambda n: pl.BlockSpec((None, None, 2 * C_WIDTH, n), lambda l, b, j: (l, b, 0, 0))
    x_spec = pl.BlockSpec((ROW_TILE, D_MODEL), lambda l, b, j: (b * nblk + first + j, 0))
    return pl.pallas_call(
        _kv_tail_kernel,
        grid=(n_layers, batch, nj),
        in_specs=[x_spec] * n_layers
                 + [pl.BlockSpec((None,) + wt_groups.shape[1:], lambda l, b, j: (l, 0, 0, 0))],
        out_specs=[last(WINDOWS[0]), last(WINDOWS[1]),
                   pl.BlockSpec((None, None, 2 * C_WIDTH, ROW_TILE), lambda l, b, j: (l, b, 0, j))],
        out_shape=[jax.ShapeDtypeStruct((n_layers, batch, 2 * C_WIDTH, n), F32) for n in WINDOWS],
        compiler_params=_params(3),
        name="kv_tail",
    )(*x_layers, wt_groups)


def _attn_kernel(qt_ref, kc_ref, kp_ref, vtc_ref, vtp_ref, o_ref, lse_ref, bias_scr, *, dil, chain):
    first_step = (pl.program_id(0) == 0) & (pl.program_id(1) == 0) & (pl.program_id(2) == 0)

    @pl.when(first_step)
    def _():
        kj = lax.broadcasted_iota(jnp.int32, (CHUNK, CHUNK), 0)
        qi = lax.broadcasted_iota(jnp.int32, (CHUNK, CHUNK), 1)
        back_own = qi - kj
        back_prev = back_own + CHUNK
        for h in range(H_G):
            slope = SLOPES[h] * dil * LOG2E
            bias_scr[0, h] = jnp.where(back_prev <= N_BACK, -slope * back_prev.astype(F32), NEG_INF)
            bias_scr[1, h] = jnp.full((CHUNK, CHUNK), NEG_INF, F32)
            bias_scr[2, h] = jnp.where(back_own >= 0, -slope * back_own.astype(F32), NEG_INF)

    n_sub = ROW_TILE // CHUNK
    if chain:
        run_steps = PERM_TILE // dil // ROW_TILE
        at_start = (pl.program_id(1) == 0) & (pl.program_id(2) % run_steps == 0)
    else:
        at_start = pl.program_id(1) == 0
    zeros_half = jnp.zeros((HEAD_DIM, CHUNK), BF16)
    lse_pad = jnp.zeros((CHUNK - H_G, CHUNK), F32)

    def half_tile(k_rows, qt_h, vt_cols, bias):
        st = jnp.dot(k_rows, qt_h, preferred_element_type=F32) + bias
        m = jnp.max(st, axis=0, keepdims=True)
        p = jnp.exp2(st - m)
        l = jnp.sum(p, axis=0, keepdims=True)
        return m, l, jnp.dot(vt_cols, p.astype(BF16), preferred_element_type=F32)

    for s in range(n_sub):
        own = slice(s * CHUNK, (s + 1) * CHUNK)
        if chain and s > 0:
            prev_k, prev_vt = kc_ref, vtc_ref
            prv = slice((s - 1) * CHUNK, s * CHUNK)
        else:
            prev_k, prev_vt = kp_ref, vtp_ref
            prv = own if not chain else slice(0, CHUNK)
        masked = at_start if (s == 0 or not chain) else None
        lses = []
        for pair in range(H_G // 2):
            lanes = slice(pair * 2 * HEAD_DIM, (pair + 1) * 2 * HEAD_DIM)
            qt_pair = qt_ref[lanes, own]
            outs = []
            for half in range(2):
                h = 2 * pair + half
                chans = slice(h * HEAD_DIM, (h + 1) * HEAD_DIM)
                qt_h = (jnp.concatenate([qt_pair[0:HEAD_DIM], zeros_half], axis=0) if half == 0
                        else jnp.concatenate([zeros_half, qt_pair[HEAD_DIM:]], axis=0))
                bias_prev = bias_scr[0, h] if masked is None else bias_scr[jnp.where(masked, 1, 0), h]
                m_a, l_a, o_a = half_tile(prev_k[prv, lanes], qt_h, prev_vt[chans, prv], bias_prev)
                m_b, l_b, o_b = half_tile(kc_ref[own, lanes], qt_h, vtc_ref[chans, own], bias_scr[2, h])
                m = jnp.maximum(m_a, m_b)
                w_a, w_b = jnp.exp2(m_a - m), jnp.exp2(m_b - m)
                l = w_a * l_a + w_b * l_b
                outs.append((w_a * o_a + w_b * o_b) * (1.0 / l))
                lses.append((m + jnp.log2(l)) * LN2)
            o_ref[own, lanes] = jnp.concatenate(outs, axis=0).T
        lse_ref[own, :] = jnp.concatenate(lses + [lse_pad], axis=0).T


def _attn_prompt(k_all, qt_all, vt_all, batch, seq, g):
    d = DILATIONS[g]
    nt = seq // PERM_TILE
    nj = PERM_TILE // ROW_TILE
    sub_per_step = ROW_TILE // CHUNK
    run = PERM_TILE // d
    chain = run > CHUNK
    assert run % ROW_TILE == 0 if chain else run == CHUNK
    cur = lambda b, t, j: (b * nt + t) * nj + j
    if chain:
        back = (PERM_TILE - run) // CHUNK + 1
        prev_rows = CHUNK
        prev = lambda b, t, j: jnp.maximum(cur(b, t, j) * sub_per_step - back, 0)
    else:
        prev_rows = ROW_TILE
        prev = lambda b, t, j: jnp.maximum(cur(b, t, j) - nj, 0)
    tok = lambda rows, idx: pl.BlockSpec((None, rows, C_WIDTH), lambda b, t, j: (g, idx(b, t, j), 0))
    chn = lambda cols, idx: pl.BlockSpec((None, C_WIDTH, cols), lambda b, t, j: (g, 0, idx(b, t, j)))
    return pl.pallas_call(
        functools.partial(_attn_kernel, dil=d, chain=chain),
        grid=(batch, nt, nj),
        in_specs=[chn(ROW_TILE, cur), tok(ROW_TILE, cur), tok(prev_rows, prev),
                  chn(ROW_TILE, cur), chn(prev_rows, prev)],
        out_specs=[pl.BlockSpec((ROW_TILE, C_WIDTH), lambda b, t, j: (cur(b, t, j), 0)),
                   pl.BlockSpec((ROW_TILE, LSE_LANES), lambda b, t, j: (cur(b, t, j), 0))],
        out_shape=[jax.ShapeDtypeStruct((batch * seq, C_WIDTH), F32),
                   jax.ShapeDtypeStruct((batch * seq, LSE_LANES), F32)],
        scratch_shapes=[pltpu.VMEM((3, H_G, CHUNK, CHUNK), F32)],
        compiler_params=_params(3),
        name=f"attn_prompt_d{d}",
    )(qt_all, k_all, k_all, vt_all, vt_all)


def _sattn_kernel(*refs, buf_len, dil, t_new, roll):
    q_ref, kn_ref, vn_ref, cache_ref = refs[:4]
    bias_c, bias_n = refs[-2:]
    o_ref, lse_ref = refs[-5:-3] if roll else refs[-4:-2]
    newc_ref = refs[-3] if roll else None
    new_lane0 = CHUNK - t_new

    @pl.when(pl.program_id(0) == 0)
    def _():
        def bias(dist, ok, h):
            b0 = jnp.where((dist & (dil - 1)) == 0, -SLOPES[h] * dist.astype(F32), NEG_INF)
            b0 = jnp.where(dist >= 0, b0, NEG_INF)
            b0 = jnp.where(dist <= N_BACK * dil, b0, NEG_INF)
            return b0 if ok is None else jnp.where(ok, b0, NEG_INF)
        t_c = lax.broadcasted_iota(jnp.int32, (t_new, buf_len), 0)
        p_c = lax.broadcasted_iota(jnp.int32, (t_new, buf_len), 1)
        t_n = lax.broadcasted_iota(jnp.int32, (t_new, CHUNK), 0)
        j_n = lax.broadcasted_iota(jnp.int32, (t_new, CHUNK), 1)
        for h in range(H_G):
            bias_c[h * t_new:(h + 1) * t_new, :] = bias(buf_len + t_c - p_c, None, h)
            bias_n[h * t_new:(h + 1) * t_new, :] = bias(t_n - (j_n - new_lane0), j_n >= new_lane0, h)

    pad = jnp.zeros((new_lane0, C_WIDTH), F32)
    nt = (((1,), (1,)), ((), ()))
    n_rows = H_G * t_new
    assert t_new & (t_new - 1) == 0 and HEAD_DIM & (HEAD_DIM - 1) == 0
    row_head = lax.broadcasted_iota(jnp.int32, (n_rows, C_WIDTH), 0) >> (t_new.bit_length() - 1)
    col_head = lax.broadcasted_iota(jnp.int32, (n_rows, C_WIDTH), 1) >> (HEAD_DIM.bit_length() - 1)
    own_head = row_head == col_head
    lane = lax.broadcasted_iota(jnp.int32, (HEAD_DIM, CHUNK), 1)

    def per_query(a):
        return jnp.sum(jnp.where(own_head, a, 0.0).reshape(H_G, t_new, C_WIDTH), axis=0)

    for sb in range(cache_ref.shape[0]):
        rows = slice(sb * t_new, (sb + 1) * t_new)
        kn_t = jnp.concatenate([pad, kn_ref[rows, :]], axis=0).T
        vn_t = jnp.concatenate([pad, vn_ref[rows, :]], axis=0).T
        q = q_ref[rows, :] * QK_SCALE
        q_bd = jnp.where(own_head, jnp.concatenate([q] * H_G, axis=0), 0.0).astype(BF16)
        k_t = cache_ref[sb, 0].reshape(C_WIDTH, buf_len)
        v_t = cache_ref[sb, 1].reshape(C_WIDTH, buf_len)
        lc = jnp.dot(q_bd, k_t.astype(BF16), preferred_element_type=F32) + bias_c[...]
        ln = jnp.dot(q_bd, kn_t.astype(BF16), preferred_element_type=F32) + bias_n[...]
        m = jnp.maximum(jnp.max(lc, axis=-1, keepdims=True), jnp.max(ln, axis=-1, keepdims=True))
        pc = jnp.exp(lc - m)
        pn = jnp.exp(ln - m)
        l = jnp.sum(pc, axis=-1, keepdims=True) + jnp.sum(pn, axis=-1, keepdims=True)
        o = (lax.dot_general(pc.astype(BF16), v_t.astype(BF16), nt, preferred_element_type=F32)
             + lax.dot_general(pn.astype(BF16), vn_t.astype(BF16), nt, preferred_element_type=F32))
        o_ref[rows, :] = per_query(o / l)
        lse_ref[rows, :] = per_query(jnp.broadcast_to(m + jnp.log(l), (n_rows, C_WIDTH)))
        if roll:
            for kv, old_t, new_t in ((0, k_t, kn_t), (1, v_t, vn_t)):
                for h in range(H_G):
                    chans = slice(h * HEAD_DIM, (h + 1) * HEAD_DIM)
                    rolled = pltpu.roll(old_t[chans], buf_len - t_new, 1)
                    if buf_len > CHUNK:
                        newc_ref[sb, kv, h, :, 0:buf_len - CHUNK] = rolled[:, 0:buf_len - CHUNK]
                    newc_ref[sb, kv, h, :, buf_len - CHUNK:buf_len] = jnp.where(
                        lane >= new_lane0, new_t[chans], rolled[:, buf_len - CHUNK:buf_len])


def _attn_sample(qkv, cache, layer_idx, g, t_new, roll, prev_out=None):
    n_layers, n_seq, _, _, _, buf_len = cache.shape
    d = DILATIONS[g]
    n_sb = max(1, min(n_seq, SAMPLE_STEP_POSITIONS // buf_len))
    assert n_seq % n_sb == 0
    row = lambda sec: pl.BlockSpec((n_sb * t_new, C_WIDTH), lambda b: (b, sec * N_GROUPS + g))
    cache_spec = pl.BlockSpec((None, n_sb, 2, H_G, HEAD_DIM, buf_len), lambda b: (layer_idx, b, 0, 0, 0, 0))
    out_row = pl.BlockSpec((n_sb * t_new, C_WIDTH), lambda b: (b, 0))
    in_specs, args, aliases = [row(0), row(1), row(2), cache_spec], [qkv, qkv, qkv, cache], {}
    out_specs = [out_row, out_row]
    out_shape = [jax.ShapeDtypeStruct((n_seq * t_new, C_WIDTH), F32)] * 2
    if roll:
        out_specs.append(cache_spec)
        out_shape.append(jax.ShapeDtypeStruct(cache.shape, F32))
        if prev_out is not None:
            in_specs.append(pl.BlockSpec(memory_space=pl.ANY))
            args.append(prev_out)
            aliases = {4: 2}
    return pl.pallas_call(
        functools.partial(_sattn_kernel, buf_len=buf_len, dil=d, t_new=t_new, roll=roll),
        grid=(n_seq // n_sb,),
        in_specs=in_specs,
        out_specs=out_specs,
        out_shape=out_shape,
        scratch_shapes=[pltpu.VMEM((H_G * t_new, buf_len), F32), pltpu.VMEM((H_G * t_new, CHUNK), F32)],
        input_output_aliases=aliases,
        compiler_params=_params(1),
        name=f"attn_sample_d{d}",
    )(*args)


def _patch_kernel(*refs, t_new):
    tail_ref, out_ref = refs[-2:]
    new_lane0 = CHUNK - t_new
    pad = jnp.zeros((new_lane0, C_WIDTH), F32)
    lane = lax.broadcasted_iota(jnp.int32, (HEAD_DIM, CHUNK), 1)

    def patch(kn_ref, vn_ref):
        for sb in range(tail_ref.shape[0]):
            rows = slice(sb * t_new, (sb + 1) * t_new)
            for kv, new_ref in ((0, kn_ref), (1, vn_ref)):
                new_t = jnp.concatenate([pad, new_ref[rows, :]], axis=0).T
                for h in range(H_G):
                    out_ref[sb, kv, h] = jnp.where(lane >= new_lane0, new_t[h * HEAD_DIM:(h + 1) * HEAD_DIM],
                                                   tail_ref[sb, kv, h])

    for layer in range((len(refs) - 2) // 2):
        pl.when(pl.program_id(0) == layer)(functools.partial(patch, refs[2 * layer], refs[2 * layer + 1]))


def _patch_rolled(rolled, qkv_layers, g, t_new):
    n_layers, n_seq, _, _, _, buf_len = rolled.shape
    assert n_layers == len(qkv_layers)
    n_sb = min(n_seq, 8)
    assert n_seq % n_sb == 0
    row = lambda sec: pl.BlockSpec((n_sb * t_new, C_WIDTH), lambda l, b: (b, sec * N_GROUPS + g))
    tail_spec = pl.BlockSpec((None, n_sb, 2, H_G, HEAD_DIM, CHUNK),
                             lambda l, b: (l, b, 0, 0, 0, buf_len // CHUNK - 1))
    return pl.pallas_call(
        functools.partial(_patch_kernel, t_new=t_new),
        grid=(n_layers, n_seq // n_sb),
        in_specs=[row(1), row(2)] * n_layers + [tail_spec],
        out_specs=tail_spec,
        out_shape=jax.ShapeDtypeStruct(rolled.shape, rolled.dtype),
        input_output_aliases={2 * n_layers: 0},
        compiler_params=_params(2),
        name=f"patch_rolled_d{DILATIONS[g]}",
    )(*[q for qkv in qkv_layers for q in (qkv, qkv)], rolled)


def _merge_kernel(*refs, tm, dils):
    x_ref = refs[0]
    ol_refs = refs[1:1 + 2 * N_GROUPS]
    w_ref, g_ref, b_ref, out_ref = refs[1 + 2 * N_GROUPS:5 + 2 * N_GROUPS]
    scrs = list(refs[5 + 2 * N_GROUPS:])

    def natural(ref, d):
        if d == 1:
            return lambda rows: ref[rows, :]
        scr = scrs.pop()
        n = tm // d
        slabs = ref.shape[-1] // 128
        for r in range(d):
            for s in range(slabs):
                scr[s, pl.ds(r, n, stride=d), :] = ref[r, :, s * 128:(s + 1) * 128]
        return lambda rows: jnp.concatenate([scr[s, rows, :] for s in range(slabs)], axis=-1)

    loaders = [natural(ol_refs[k], dils[k % N_GROUPS]) for k in range(2 * N_GROUPS)]
    compact = ol_refs[N_GROUPS].shape[-1] != C_WIDTH
    if compact:
        lane = lax.broadcasted_iota(jnp.int32, (2 * LSE_LANES, C_WIDTH), 0) & (LSE_LANES - 1)
        chan = lax.broadcasted_iota(jnp.int32, (2 * LSE_LANES, C_WIDTH), 1)
        spread = jnp.where((chan >> (HEAD_DIM.bit_length() - 1)) == lane, 1.0, 0.0).astype(BF16)

        def expand(a):
            hi = a.astype(BF16)
            lo = (a - hi.astype(F32)).astype(BF16)
            return jnp.dot(jnp.concatenate([hi, lo], axis=-1), spread, preferred_element_type=F32)

    o0, o1, o2, l0, l1, l2 = [load(slice(0, tm)) for load in loaders]
    m = jnp.maximum(jnp.maximum(l0, l1), l2)
    e0, e1, e2 = jnp.exp(l0 - m), jnp.exp(l1 - m), jnp.exp(l2 - m)
    inv = 1.0 / (e0 + e1 + e2)
    alphas = [e0 * inv, e1 * inv, e2 * inv]
    if compact:
        alphas = [expand(a) for a in alphas]
    o = alphas[0] * o0 + alphas[1] * o1 + alphas[2] * o2
    mix = jnp.dot(o.astype(BF16), w_ref[...], preferred_element_type=F32)
    out_ref[...] = _layer_norm(ALPHA * x_ref[...] + mix, g_ref[...], b_ref[...])


def _merge_layer(x, outs, lses, w_out, g, b, dils):
    rows = x.shape[0]
    tm = min(MERGE_TILE, rows)
    per_tile = PERM_TILE // tm
    row_spec = lambda cols: pl.BlockSpec((tm, cols), lambda t: (t, 0))

    def group_arg(a, d):
        cols = a.shape[-1]
        if d == 1:
            return a, row_spec(cols), None
        run = PERM_TILE // d
        view = a.reshape(rows // PERM_TILE, d, run, cols)
        spec = pl.BlockSpec((None, d, tm // d, cols), lambda t: (t // per_tile, 0, t % per_tile, 0))
        return view, spec, pltpu.VMEM((cols // 128, tm, 128), F32)

    args, specs, scratch = zip(*[group_arg(a, dils[k % N_GROUPS]) for k, a in enumerate(list(outs) + list(lses))])
    scratch = [s for s in scratch if s is not None][::-1]
    weights, w_specs = _resident_all((w_out, g, b))
    return pl.pallas_call(
        functools.partial(_merge_kernel, tm=tm, dils=dils),
        grid=(rows // tm,),
        in_specs=[row_spec(D_MODEL)] + list(specs) + w_specs,
        out_specs=row_spec(D_MODEL),
        out_shape=jax.ShapeDtypeStruct((rows, D_MODEL), F32),
        scratch_shapes=scratch,
        compiler_params=_params(1),
        name="merge_layer",
    )(x, *args, *weights)


def kernel(x_prompt, x_sample, state_conv, cache_kv_w128, cache_kv_w512, cache_kv_w2048, w_in_ab, ln_v_g, ln_v_b, w_spatial, b_spatial, conv_w, w_out_ab, w_qkv_c, w_out_c, ln1_g, ln1_b, ln2_g, ln2_b, w_mlp_up, w_mlp_down):
    batch, seq, _ = x_prompt.shape
    n_seq, t_new, _ = x_sample.shape
    n_tok_s = n_seq * t_new
    xp = x_prompt.reshape(batch * seq, D_MODEL)
    xs = x_sample.reshape(n_tok_s, D_MODEL)
    caches = [jnp.transpose(c, (0, 1, 3, 4, 5, 2)) for c in (cache_kv_w128, cache_kv_w512, cache_kv_w2048)]
    rows3 = lambda a: a.reshape(a.shape[0], 1, a.shape[1])
    ln_v_g3, ln_v_b3, ln1_g3, ln1_b3, ln2_g3, ln2_b3 = map(rows3, (ln_v_g, ln_v_b, ln1_g, ln1_b, ln2_g, ln2_b))

    causal = jnp.tril(jnp.ones((CHUNK, CHUNK), dtype=bool))
    w_in_b, w_out_ab_b, w_out_c_b = w_in_ab.astype(BF16), w_out_ab.astype(BF16), w_out_c.astype(BF16)
    cast_later = (w_mlp_up.reshape(-1, D_FF), w_mlp_down.reshape(-1, D_MODEL))
    w_qkv_b = w_qkv_c.astype(BF16)
    n_c = w_qkv_b.shape[0]
    wq5 = w_qkv_b.reshape(n_c, D_MODEL, 3, N_GROUPS, C_WIDTH)
    wk_groups = jnp.transpose(wq5[:, :, 1], (0, 2, 1, 3))
    wqv_t_groups = jnp.transpose(wq5[:, :, 0::2], (0, 3, 2, 4, 1)).reshape(n_c, N_GROUPS, 2 * C_WIDTH, D_MODEL)
    wkv_t_groups = jnp.transpose(wq5[:, :, 1:], (0, 3, 2, 4, 1)).reshape(n_c, N_GROUPS, 2 * C_WIDTH, D_MODEL)
    conv_p, conv_s, chunk_v_s = [], [], []
    c_inputs = []
    qkv_s_layers, rolled = [], None
    new_caches = [None] * N_GROUPS
    hosted = [g for g in range(N_GROUPS) if caches[g].shape[-1] > SAMPLE_STEP_POSITIONS]
    mlp_steps = batch * seq // ROW_TILE
    assert DEPTH * mlp_steps == caches[0].shape[0] * n_seq * 2

    for layer in range(DEPTH):
        i = layer // 2
        if layer % 2 == 0:
            w_tril = jnp.where(causal[None], w_spatial[i], 0.0)
            bmix = jnp.repeat(b_spatial[i].T, CHUNK, axis=1)
            shared = (_Layer(w_in_b, i), _Layer(ln_v_g3, i), _Layer(ln_v_b3, i))
            tail = (_Layer(conv_w, i), _Layer(w_out_ab_b, i), _Layer(ln1_g3, layer), _Layer(ln1_b3, layer))
            xp, buf_p, cast_done = _ab_layer_prompt(xp, batch, seq, shared + (w_tril.astype(BF16), bmix) + tail,
                                                    cast_later if layer == 0 else ())
            if layer == 0:
                w_up_b = cast_done[0].reshape(w_mlp_up.shape)
                w_down_b = cast_done[1].reshape(w_mlp_down.shape)
            eye = jnp.eye(n_seq, dtype=F32)
            w_blk = jnp.stack([jnp.kron(eye, w_tril[g, :t_new, :t_new]) for g in range(G_A)])
            bmix_s = jnp.tile(bmix[:t_new], (n_seq, 1))
            st = state_conv[i]
            s1 = jnp.concatenate([st[:, 1:2], jnp.zeros((n_seq, t_new - 1, W_B), F32)], axis=1)
            s2 = jnp.concatenate([st, jnp.zeros((n_seq, t_new - 2, W_B), F32)], axis=1)
            xs, hc_s, v_s = _ab_layer_sample(xs, s1.reshape(n_tok_s, W_B), s2.reshape(n_tok_s, W_B),
                                             shared + (w_blk.astype(BF16), bmix_s) + tail)
            conv_p.append(buf_p)
            conv_s.append(hc_s.reshape(n_seq, t_new, W_B)[:, t_new - 2:])
            chunk_v_s.append(v_s.reshape(n_seq, t_new, W_A))
        else:
            wo = _Layer(w_out_c_b, i)
            g1, b1 = _Layer(ln1_g3, layer), _Layer(ln1_b3, layer)
            c_inputs.append(xp)
            k_all, qt_all, vt_all = _qkv_proj_perm(xp, wk_groups, wqv_t_groups, i, batch, seq)
            outs, lses = zip(*[_attn_prompt(k_all, qt_all, vt_all, batch, seq, g) for g in range(N_GROUPS)])
            xp = _merge_layer(xp, outs, lses, wo, g1, b1, DILATIONS)
            qkv_s = _qkv_proj(xs, _Layer(w_qkv_b, i))
            qkv_s_layers.append(qkv_s)
            outs, lses = [], []
            for g in range(N_GROUPS):
                res = _attn_sample(qkv_s, caches[g], i, g, t_new, g not in hosted, new_caches[g])
                outs.append(res[0])
                lses.append(res[1])
                if g not in hosted:
                    new_caches[g] = res[2]
            xs = _merge_layer(xs, outs, lses, wo, g1, b1, (1,) * N_GROUPS)
        w_up, w_down = _Layer(w_up_b, layer), _Layer(w_down_b, layer)
        g2, b2 = _Layer(ln2_g3, layer), _Layer(ln2_b3, layer)
        xp, xs, rolled = _mlp_layer(xp, xs, w_up, w_down, g2, b2,
                                    roll=([caches[g] for g in hosted], rolled, layer * mlp_steps, t_new))

    for k, g in enumerate(hosted):
        new_caches[g] = _patch_rolled(rolled[k], qkv_s_layers, g, t_new)
    kv_s = [jnp.transpose(nc, (0, 1, 5, 2, 3, 4)) for nc in new_caches]
    tails = _kv_tail(c_inputs, wkv_t_groups, batch, seq)
    kv_p = [jnp.transpose(t.reshape(n_c, batch, 2, H_G, HEAD_DIM, WINDOWS[g]), (0, 1, 5, 2, 3, 4))
            for g, t in enumerate(tails)]
    return (xp.reshape(batch, seq, D_MODEL), xs.reshape(n_seq, t_new, D_MODEL),
            jnp.stack(conv_p), jnp.stack(conv_s), jnp.stack(chunk_v_s),
            kv_p[0], kv_p[1], kv_p[2], kv_s[0], kv_s[1], kv_s[2])
```

```python
import functools
import math
from typing import NamedTuple

import jax
import jax.numpy as jnp
from jax import lax
from jax.experimental import pallas as pl
from jax.experimental.pallas import tpu as pltpu

F32 = jnp.float32
BF16 = jnp.bfloat16

D_MODEL = 1024
DEPTH = 4
CHUNK = 128
W_A = 512
G_A = 4
W_B = 512
N_GROUPS = 3
WINDOWS = (128, 512, 2048)
DILATIONS = (1, 4, 16)
N_BACK = 128
H_G = 8
HEAD_DIM = 64
C_WIDTH = H_G * HEAD_DIM
QKV_COLS = N_GROUPS * C_WIDTH
QKV_BLOCKS = 3 * N_GROUPS
D_FF = 4 * D_MODEL
ALPHA = (2.0 * DEPTH) ** 0.25
LN_EPS = 1e-5
NEG_INF = -1e30
SLOPES = tuple(2.0 ** (-(8.0 / H_G) * j) for j in range(1, H_G + 1))
QK_SCALE = HEAD_DIM ** -0.5
LOG2E = math.log2(math.e)
LN2 = math.log(2.0)

VMEM_LIMIT_BYTES = 52 * 1024 * 1024
ROW_TILE = 512
PERM_TILE = CHUNK * max(DILATIONS)
X_SLABS = D_MODEL // 128
MERGE_TILE = 512
AB_PARTS = 2
MLP_PARTS = 2
SAMPLE_STEP_POSITIONS = 1024
LSE_LANES = 128


def _layer_norm(x, g, b):
    mu = jnp.mean(x, axis=-1, keepdims=True)
    xc = x - mu
    var = jnp.mean(xc * xc, axis=-1, keepdims=True)
    return xc * lax.rsqrt(var + LN_EPS) * g + b


def _gelu(x):
    c = math.sqrt(2.0 / math.pi)
    return x * (0.5 * (1.0 + jnp.tanh(c * (x + 0.044715 * (x * x * x)))))


class _Layer(NamedTuple):
    stacked: jax.Array
    index: int


def _resident(w):
    if isinstance(w, _Layer):
        shape = w.stacked.shape[1:]
        return w.stacked, pl.BlockSpec((None,) + shape, lambda *_: (w.index,) + (0,) * len(shape),
                                       pipeline_mode=pl.Buffered(1))
    return w, pl.BlockSpec(w.shape, lambda *_: (0,) * w.ndim, pipeline_mode=pl.Buffered(1))


def _resident_all(ws):
    arrays, specs = zip(*[_resident(w) for w in ws])
    return list(arrays), list(specs)


def _params(n_grid):
    return pltpu.CompilerParams(dimension_semantics=("arbitrary",) * n_grid,
                                vmem_limit_bytes=VMEM_LIMIT_BYTES)


def _ab_kernel(*refs, tm, chunk, sample, n_parts):
    if sample:
        (x_ref, s1_ref, s2_ref, w_in_ref, lvg_ref, lvb_ref, wmix_ref, bmix_ref, cw_ref, w_out_ref,
         g1_ref, b1_ref, o_ref, hc_ref, v_ref, y_scr) = refs
    else:
        (x_ref, w_in_ref, lvg_ref, lvb_ref, wmix_ref, bmix_ref, cw_ref, w_out_ref, g1_ref, b1_ref) = refs[:10]
        n_cast = (len(refs) - 14) // 2
        cast_in, cast_out = refs[10:10 + n_cast], refs[12 + n_cast:12 + 2 * n_cast]
        o_ref, cb_ref = refs[10 + n_cast:12 + n_cast]
        y_scr, h_scr = refs[-2:]
        for src, dst in zip(cast_in, cast_out):
            dst[...] = src[...].astype(BF16)

    if not sample:
        @pl.when(pl.program_id(1) == 0)
        def _():
            h_scr[0:8, :] = jnp.zeros((8, W_B), F32)

    n = tm // n_parts

    def project(r0):
        x = x_ref[r0:r0 + n, :]
        xb = x.astype(BF16)
        return x, [jnp.dot(xb, w_in_ref[:, k * W_A:(k + 1) * W_A], preferred_element_type=F32)
                   for k in range(5)]

    def mixers(r0, z):
        z_u, z_v, z_bg, z_cg, z_h = z
        u = _gelu(z_u)
        v = _layer_norm(_gelu(z_v), lvg_ref[...], lvb_ref[...])
        if sample:
            v_ref[...] = v
        vb = v.astype(BF16)
        for c in range(n // chunk):
            rows = slice(c * chunk, (c + 1) * chunk)
            for g in range(G_A):
                cols = slice(g * 128, (g + 1) * 128)
                y_scr[r0 + c * chunk:r0 + (c + 1) * chunk, cols] = (
                    jnp.dot(wmix_ref[g], vb[rows, cols], preferred_element_type=F32) + bmix_ref[:, cols])
        a_out = (u * y_scr[r0:r0 + n, :]).astype(BF16)

        hc = z_cg * z_h
        if sample:
            hc_ref[...] = hc
            pos = lax.broadcasted_iota(jnp.int32, hc.shape, 0) & 7
            sh1 = jnp.where(pos == 0, s1_ref[...], pltpu.roll(hc, 1, 0))
            sh2 = jnp.where(pos < 2, s2_ref[...], pltpu.roll(hc, 2, 0))
        else:
            h_scr[8 + r0:8 + r0 + n, :] = hc
            sh1 = h_scr[7 + r0:7 + r0 + n, :]
            sh2 = h_scr[6 + r0:6 + r0 + n, :]
        conv = cw_ref[0:1, :] * sh2 + cw_ref[1:2, :] * sh1 + cw_ref[2:3, :] * hc
        b_out = (z_bg * conv).astype(BF16)
        return (jnp.dot(a_out, w_out_ref[0:W_A, :], preferred_element_type=F32)
                + jnp.dot(b_out, w_out_ref[W_A:W_A + W_B, :], preferred_element_type=F32))

    def finish(r0, x, mix):
        o_ref[r0:r0 + n, :] = _layer_norm(ALPHA * x + mix, g1_ref[...], b1_ref[...])

    starts = [p * n for p in range(n_parts)]
    projected = [project(r0) for r0 in starts]
    for r0, (x, z) in zip(starts, projected):
        finish(r0, x, mixers(r0, z))
    if not sample:
        cb_ref[...] = h_scr[tm + 6:tm + 8, :]
        h_scr[0:8, :] = h_scr[tm:tm + 8, :]


def _ab_layer_prompt(x, batch, seq, weights, cast=()):
    tm = ROW_TILE
    nt = seq // tm
    steps = batch * nt
    row_spec = pl.BlockSpec((tm, D_MODEL), lambda b, t: (b * nt + t, 0))
    weights, w_specs = _resident_all(weights)
    cast_specs = [pl.BlockSpec((a.shape[0] // steps, a.shape[1]), lambda b, t: (b * nt + t, 0)) for a in cast]
    assert all(a.shape[0] % (16 * steps) == 0 for a in cast)
    outs = pl.pallas_call(
        functools.partial(_ab_kernel, tm=tm, chunk=CHUNK, sample=False, n_parts=AB_PARTS),
        grid=(batch, nt),
        in_specs=[row_spec] + w_specs + cast_specs,
        out_specs=[row_spec, pl.BlockSpec((None, 2, W_B), lambda b, t: (b, 0, 0))] + cast_specs,
        out_shape=[jax.ShapeDtypeStruct((batch * seq, D_MODEL), F32),
                   jax.ShapeDtypeStruct((batch, 2, W_B), F32)]
                  + [jax.ShapeDtypeStruct(a.shape, BF16) for a in cast],
        scratch_shapes=[pltpu.VMEM((tm, W_A), F32), pltpu.VMEM((tm + 8, W_B), F32)],
        compiler_params=_params(2),
        name="ab_layer_prompt",
    )(x, *weights, *cast)
    return outs[0], outs[1], list(outs[2:])


def _ab_layer_sample(x, s1, s2, weights):
    tm = x.shape[0]
    full = lambda cols: pl.BlockSpec((tm, cols), lambda i: (0, 0))
    weights, w_specs = _resident_all(weights)
    return pl.pallas_call(
        functools.partial(_ab_kernel, tm=tm, chunk=tm, sample=True, n_parts=1),
        grid=(1,),
        in_specs=[full(D_MODEL), full(W_B), full(W_B)] + w_specs,
        out_specs=[full(D_MODEL), full(W_B), full(W_A)],
        out_shape=[jax.ShapeDtypeStruct((tm, D_MODEL), F32),
                   jax.ShapeDtypeStruct((tm, W_B), F32),
                   jax.ShapeDtypeStruct((tm, W_A), F32)],
        scratch_shapes=[pltpu.VMEM((tm, W_A), F32)],
        compiler_params=_params(1),
        name="ab_layer_sample",
    )(x, s1, s2, *weights)


def _mlp_kernel(x_ref, xs_ref, w_up_ref, w_down_ref, g_ref, b_ref, *rest, n_roll, roll_by):
    old_refs, new_refs = rest[:n_roll], rest[len(rest) - n_roll:]
    o_ref, os_ref = rest[len(rest) - n_roll - 2], rest[len(rest) - n_roll - 1]
    for old, new in zip(old_refs, new_refs):
        buf_len = old.shape[-1]
        for h in range(H_G):
            new[h] = pltpu.roll(old[h], buf_len - roll_by, 1)

    def mlp_rows(src, dst, n_parts):
        n = src.shape[0] // n_parts
        xs = [src[p * n:(p + 1) * n, :] for p in range(n_parts)]
        xbs = [x.astype(BF16) for x in xs]
        accs = [None] * n_parts
        for c in range(D_FF // D_MODEL):
            cols = slice(c * D_MODEL, (c + 1) * D_MODEL)
            hs = [jnp.maximum(jnp.dot(xb, w_up_ref[:, cols], preferred_element_type=F32), 0.0) for xb in xbs]
            parts = [jnp.dot((h * h).astype(BF16), w_down_ref[cols, :], preferred_element_type=F32) for h in hs]
            accs = [part if acc is None else acc + part for acc, part in zip(accs, parts)]
        for p in range(n_parts):
            dst[p * n:(p + 1) * n, :] = _layer_norm(ALPHA * xs[p] + accs[p], g_ref[...], b_ref[...])

    mlp_rows(x_ref, o_ref, MLP_PARTS)

    @pl.when(pl.program_id(0) == pl.num_programs(0) - 1)
    def _():
        mlp_rows(xs_ref, os_ref, 1)


def _mlp_layer(x, x_small, w_up, w_down, g, b, roll):
    rows = x.shape[0]
    tm = min(ROW_TILE, rows)
    row_spec = pl.BlockSpec((tm, D_MODEL), lambda t: (t, 0))
    small_spec = pl.BlockSpec(x_small.shape, lambda t: (0, 0))
    weights, w_specs = _resident_all((w_up, w_down, g, b))
    bufs, rolled, first, roll_by = roll
    n_roll = len(bufs)
    n_seq = bufs[0].shape[1]

    def block_spec(buf):
        def index(t):
            blk = first + t
            return (blk // (2 * n_seq), (blk % (2 * n_seq)) // 2, blk % 2, 0, 0, 0)
        return pl.BlockSpec((None, None, None) + buf.shape[3:], index)

    args = [x, x_small] + weights + list(bufs)
    in_specs = [row_spec, small_spec] + w_specs + [block_spec(buf) for buf in bufs]
    aliases = {}
    if rolled is not None:
        aliases = {len(args) + k: 2 + k for k in range(n_roll)}
        args += list(rolled)
        in_specs += [pl.BlockSpec(memory_space=pl.ANY)] * n_roll
    outs = pl.pallas_call(
        functools.partial(_mlp_kernel, n_roll=n_roll, roll_by=roll_by),
        grid=(rows // tm,),
        in_specs=in_specs,
        out_specs=[row_spec, small_spec] + [block_spec(buf) for buf in bufs],
        out_shape=[jax.ShapeDtypeStruct(x.shape, F32), jax.ShapeDtypeStruct(x_small.shape, F32)]
                  + [jax.ShapeDtypeStruct(buf.shape, buf.dtype) for buf in bufs],
        input_output_aliases=aliases,
        compiler_params=_params(1),
        name="mlp_layer",
    )(*args)
    return outs[0], outs[1], list(outs[2:])


def _qkv_kernel(x_ref, w_ref, o_ref):
    xb = x_ref[...].astype(BF16)
    for c in range(QKV_BLOCKS):
        cols = slice(c * C_WIDTH, (c + 1) * C_WIDTH)
        o_ref[:, cols] = jnp.dot(xb, w_ref[:, cols], preferred_element_type=F32)


def _qkv_proj(x, w):
    rows = x.shape[0]
    tm = min(256, rows)
    w, w_spec = _resident(w)
    return pl.pallas_call(
        _qkv_kernel,
        grid=(rows // tm,),
        in_specs=[pl.BlockSpec((tm, D_MODEL), lambda t: (t, 0)), w_spec],
        out_specs=pl.BlockSpec((tm, 3 * QKV_COLS), lambda t: (t, 0)),
        out_shape=jax.ShapeDtypeStruct((rows, 3 * QKV_COLS), F32),
        compiler_params=_params(1),
        name="qkv_proj",
    )(x, w)


def _qkv_perm_kernel(*refs):
    x_refs, (wk_ref, wqv_t_ref, k_ref, qt_ref, vt_ref, xp_scr) = refs[:X_SLABS], refs[X_SLABS:]
    nt_dims = (((1,), (1,)), ((), ()))

    def step(d):
        n = PERM_TILE // d
        for c in range(PERM_TILE // ROW_TILE):
            rows = slice(c * ROW_TILE, (c + 1) * ROW_TILE)
            for r in range(d):
                lo, hi = max(r * n, rows.start), min((r + 1) * n, rows.stop)
                if lo >= hi:
                    continue
                for s in range(X_SLABS):
                    src = x_refs[s]
                    vals = (src[lo:hi, :] if d == 1 else src[pl.ds(r + (lo - r * n) * d, hi - lo, stride=d), :])
                    xp_scr[lo:hi, s * 128:(s + 1) * 128] = vals.astype(BF16)
            xs = xp_scr[rows, :]
            k_ref[rows, :] = jnp.dot(xs, wk_ref[...], preferred_element_type=F32).astype(BF16)
            qv_t = lax.dot_general(wqv_t_ref[...], xs, nt_dims, preferred_element_type=F32)
            qt_ref[:, rows] = (qv_t[0:C_WIDTH] * (QK_SCALE * LOG2E)).astype(BF16)
            vt_ref[:, rows] = qv_t[C_WIDTH:2 * C_WIDTH].astype(BF16)

    for g in range(N_GROUPS):
        pl.when(pl.program_id(2) == g)(functools.partial(step, DILATIONS[g]))


def _qkv_proj_perm(x, wk_groups, wqv_t_groups, layer_idx, batch, seq):
    nt = seq // PERM_TILE
    slab = lambda s: pl.BlockSpec((PERM_TILE, 128), lambda b, t, g: (b * nt + t, s))
    chan_spec = pl.BlockSpec((None, C_WIDTH, PERM_TILE), lambda b, t, g: (g, 0, b * nt + t))
    chan_shape = jax.ShapeDtypeStruct((N_GROUPS, C_WIDTH, batch * seq), BF16)
    return pl.pallas_call(
        _qkv_perm_kernel,
        grid=(batch, nt, N_GROUPS),
        in_specs=[slab(s) for s in range(X_SLABS)]
                 + [pl.BlockSpec((None, None, D_MODEL, C_WIDTH), lambda b, t, g: (layer_idx, g, 0, 0)),
                    pl.BlockSpec((None, None, 2 * C_WIDTH, D_MODEL), lambda b, t, g: (layer_idx, g, 0, 0))],
        out_specs=[pl.BlockSpec((None, PERM_TILE, C_WIDTH), lambda b, t, g: (g, b * nt + t, 0)),
                   chan_spec, chan_spec],
        out_shape=[jax.ShapeDtypeStruct((N_GROUPS, batch * seq, C_WIDTH), BF16), chan_shape, chan_shape],
        scratch_shapes=[pltpu.VMEM((PERM_TILE, D_MODEL), BF16)],
        compiler_params=_params(3),
        name="qkv_proj_perm",
    )(*([x] * X_SLABS), wk_groups, wqv_t_groups)


def _kv_tail_kernel(*refs):
    n_layers = len(refs) - 4
    x_refs, wt_ref, (o0_ref, o1_ref, o2_ref) = refs[:n_layers], refs[n_layers], refs[n_layers + 1:]
    nt = (((1,), (1,)), ((), ()))
    for layer in range(n_layers):
        @pl.when(pl.program_id(0) == layer)
        def _(x_ref=x_refs[layer]):
            xb = x_ref[...].astype(BF16)
            o2_ref[...] = lax.dot_general(wt_ref[2], xb, nt, preferred_element_type=F32)

            @pl.when(pl.program_id(2) == pl.num_programs(2) - 1)
            def _():
                o1_ref[...] = lax.dot_general(wt_ref[1], xb, nt, preferred_element_type=F32)
                o0_ref[...] = lax.dot_general(wt_ref[0], xb[ROW_TILE - WINDOWS[0]:, :], nt,
                                              preferred_element_type=F32)


def _kv_tail(x_layers, wt_groups, batch, seq):
    n_layers = len(x_layers)
    nj = WINDOWS[2] // ROW_TILE
    first = (seq - WINDOWS[2]) // ROW_TILE
    nblk = seq // ROW_TILE
    last = lambda n: pl.BlockSpec((None, None, 2 * C_WIDTH, n), lambda l, b, j: (l, b, 0, 0))
    def x_spec(layer):
        return pl.BlockSpec((ROW_TILE, D_MODEL),
                            lambda l, b, j: (jnp.where(l == layer, b * nblk + first + j, 0), 0))
    return pl.pallas_call(
        _kv_tail_kernel,
        grid=(n_layers, batch, nj),
        in_specs=[x_spec(layer) for layer in range(n_layers)]
                 + [pl.BlockSpec((None,) + wt_groups.shape[1:], lambda l, b, j: (l, 0, 0, 0))],
        out_specs=[last(WINDOWS[0]), last(WINDOWS[1]),
                   pl.BlockSpec((None, None, 2 * C_WIDTH, ROW_TILE), lambda l, b, j: (l, b, 0, j))],
        out_shape=[jax.ShapeDtypeStruct((n_layers, batch, 2 * C_WIDTH, n), F32) for n in WINDOWS],
        compiler_params=_params(3),
        name="kv_tail",
    )(*x_layers, wt_groups)


def _attn_kernel(qt_ref, kc_ref, kp_ref, vtc_ref, vtp_ref, o_ref, lse_ref, bias_scr, *, dil, chain):
    first_step = (pl.program_id(0) == 0) & (pl.program_id(1) == 0) & (pl.program_id(2) == 0)

    @pl.when(first_step)
    def _():
        kj = lax.broadcasted_iota(jnp.int32, (CHUNK, CHUNK), 0)
        qi = lax.broadcasted_iota(jnp.int32, (CHUNK, CHUNK), 1)
        back_own = qi - kj
        back_prev = back_own + CHUNK
        for h in range(H_G):
            slope = SLOPES[h] * dil * LOG2E
            bias_scr[0, h] = jnp.where(back_prev <= N_BACK, -slope * back_prev.astype(F32), NEG_INF)
            bias_scr[1, h] = jnp.full((CHUNK, CHUNK), NEG_INF, F32)
            bias_scr[2, h] = jnp.where(back_own >= 0, -slope * back_own.astype(F32), NEG_INF)

    n_sub = ROW_TILE // CHUNK
    if chain:
        run_steps = PERM_TILE // dil // ROW_TILE
        at_start = (pl.program_id(1) == 0) & (pl.program_id(2) % run_steps == 0)
    else:
        at_start = pl.program_id(1) == 0
    zeros_half = jnp.zeros((HEAD_DIM, CHUNK), BF16)
    lse_pad = jnp.zeros((CHUNK - H_G, CHUNK), F32)

    def half_tile(k_rows, qt_h, vt_cols, bias):
        st = jnp.dot(k_rows, qt_h, preferred_element_type=F32) + bias
        m = jnp.max(st, axis=0, keepdims=True)
        p = jnp.exp2(st - m)
        l = jnp.sum(p, axis=0, keepdims=True)
        return m, l, jnp.dot(vt_cols, p.astype(BF16), preferred_element_type=F32)

    for s in range(n_sub):
        own = slice(s * CHUNK, (s + 1) * CHUNK)
        if chain and s > 0:
            prev_k, prev_vt = kc_ref, vtc_ref
            prv = slice((s - 1) * CHUNK, s * CHUNK)
        else:
            prev_k, prev_vt = kp_ref, vtp_ref
            prv = own if not chain else slice(0, CHUNK)
        masked = at_start if (s == 0 or not chain) else None
        lses = []
        for pair in range(H_G // 2):
            lanes = slice(pair * 2 * HEAD_DIM, (pair + 1) * 2 * HEAD_DIM)
            qt_pair = qt_ref[lanes, own]
            outs = []
            for half in range(2):
                h = 2 * pair + half
                chans = slice(h * HEAD_DIM, (h + 1) * HEAD_DIM)
                qt_h = (jnp.concatenate([qt_pair[0:HEAD_DIM], zeros_half], axis=0) if half == 0
                        else jnp.concatenate([zeros_half, qt_pair[HEAD_DIM:]], axis=0))
                bias_prev = bias_scr[0, h] if masked is None else bias_scr[jnp.where(masked, 1, 0), h]
                m_a, l_a, o_a = half_tile(prev_k[prv, lanes], qt_h, prev_vt[chans, prv], bias_prev)
                m_b, l_b, o_b = half_tile(kc_ref[own, lanes], qt_h, vtc_ref[chans, own], bias_scr[2, h])
                m = jnp.maximum(m_a, m_b)
                w_a, w_b = jnp.exp2(m_a - m), jnp.exp2(m_b - m)
                l = w_a * l_a + w_b * l_b
                outs.append((w_a * o_a + w_b * o_b) * (1.0 / l))
                lses.append((m + jnp.log2(l)) * LN2)
            o_ref[own, lanes] = jnp.concatenate(outs, axis=0).T
        lse_ref[own, :] = jnp.concatenate(lses + [lse_pad], axis=0).T


def _attn_prompt(k_all, qt_all, vt_all, batch, seq, g):
    d = DILATIONS[g]
    nt = seq // PERM_TILE
    nj = PERM_TILE // ROW_TILE
    sub_per_step = ROW_TILE // CHUNK
    run = PERM_TILE // d
    chain = run > CHUNK
    assert run % ROW_TILE == 0 if chain else run == CHUNK
    cur = lambda b, t, j: (b * nt + t) * nj + j
    if chain:
        back = (PERM_TILE - run) // CHUNK + 1
        prev_rows = CHUNK
        prev = lambda b, t, j: jnp.maximum(cur(b, t, j) * sub_per_step - back, 0)
    else:
        prev_rows = ROW_TILE
        prev = lambda b, t, j: jnp.maximum(cur(b, t, j) - nj, 0)
    tok = lambda rows, idx: pl.BlockSpec((None, rows, C_WIDTH), lambda b, t, j: (g, idx(b, t, j), 0))
    chn = lambda cols, idx: pl.BlockSpec((None, C_WIDTH, cols), lambda b, t, j: (g, 0, idx(b, t, j)))
    return pl.pallas_call(
        functools.partial(_attn_kernel, dil=d, chain=chain),
        grid=(batch, nt, nj),
        in_specs=[chn(ROW_TILE, cur), tok(ROW_TILE, cur), tok(prev_rows, prev),
                  chn(ROW_TILE, cur), chn(prev_rows, prev)],
        out_specs=[pl.BlockSpec((ROW_TILE, C_WIDTH), lambda b, t, j: (cur(b, t, j), 0)),
                   pl.BlockSpec((ROW_TILE, LSE_LANES), lambda b, t, j: (cur(b, t, j), 0))],
        out_shape=[jax.ShapeDtypeStruct((batch * seq, C_WIDTH), F32),
                   jax.ShapeDtypeStruct((batch * seq, LSE_LANES), F32)],
        scratch_shapes=[pltpu.VMEM((3, H_G, CHUNK, CHUNK), F32)],
        compiler_params=_params(3),
        name=f"attn_prompt_d{d}",
    )(qt_all, k_all, k_all, vt_all, vt_all)


def _sattn_kernel(*refs, buf_len, dil, t_new, roll):
    q_ref, kn_ref, vn_ref, cache_ref = refs[:4]
    bias_c, bias_n = refs[-2:]
    o_ref, lse_ref = refs[-5:-3] if roll else refs[-4:-2]
    newc_ref = refs[-3] if roll else None
    new_lane0 = CHUNK - t_new

    @pl.when(pl.program_id(0) == 0)
    def _():
        def bias(dist, ok, h):
            b0 = jnp.where((dist & (dil - 1)) == 0, -SLOPES[h] * dist.astype(F32), NEG_INF)
            b0 = jnp.where(dist >= 0, b0, NEG_INF)
            b0 = jnp.where(dist <= N_BACK * dil, b0, NEG_INF)
            return b0 if ok is None else jnp.where(ok, b0, NEG_INF)
        t_c = lax.broadcasted_iota(jnp.int32, (t_new, buf_len), 0)
        p_c = lax.broadcasted_iota(jnp.int32, (t_new, buf_len), 1)
        t_n = lax.broadcasted_iota(jnp.int32, (t_new, CHUNK), 0)
        j_n = lax.broadcasted_iota(jnp.int32, (t_new, CHUNK), 1)
        for h in range(H_G):
            bias_c[h * t_new:(h + 1) * t_new, :] = bias(buf_len + t_c - p_c, None, h)
            bias_n[h * t_new:(h + 1) * t_new, :] = bias(t_n - (j_n - new_lane0), j_n >= new_lane0, h)

    pad = jnp.zeros((new_lane0, C_WIDTH), F32)
    nt = (((1,), (1,)), ((), ()))
    n_rows = H_G * t_new
    assert t_new & (t_new - 1) == 0 and HEAD_DIM & (HEAD_DIM - 1) == 0
    row_head = lax.broadcasted_iota(jnp.int32, (n_rows, C_WIDTH), 0) >> (t_new.bit_length() - 1)
    col_head = lax.broadcasted_iota(jnp.int32, (n_rows, C_WIDTH), 1) >> (HEAD_DIM.bit_length() - 1)
    own_head = row_head == col_head
    lane = lax.broadcasted_iota(jnp.int32, (HEAD_DIM, CHUNK), 1)

    def per_query(a):
        return jnp.sum(jnp.where(own_head, a, 0.0).reshape(H_G, t_new, C_WIDTH), axis=0)

    for sb in range(cache_ref.shape[0]):
        rows = slice(sb * t_new, (sb + 1) * t_new)
        kn_t = jnp.concatenate([pad, kn_ref[rows, :]], axis=0).T
        vn_t = jnp.concatenate([pad, vn_ref[rows, :]], axis=0).T
        q = q_ref[rows, :] * QK_SCALE
        q_bd = jnp.where(own_head, jnp.concatenate([q] * H_G, axis=0), 0.0).astype(BF16)
        k_t = cache_ref[sb, 0].reshape(C_WIDTH, buf_len)
        v_t = cache_ref[sb, 1].reshape(C_WIDTH, buf_len)
        lc = jnp.dot(q_bd, k_t.astype(BF16), preferred_element_type=F32) + bias_c[...]
        ln = jnp.dot(q_bd, kn_t.astype(BF16), preferred_element_type=F32) + bias_n[...]
        m = jnp.maximum(jnp.max(lc, axis=-1, keepdims=True), jnp.max(ln, axis=-1, keepdims=True))
        pc = jnp.exp(lc - m)
        pn = jnp.exp(ln - m)
        l = jnp.sum(pc, axis=-1, keepdims=True) + jnp.sum(pn, axis=-1, keepdims=True)
        o = (lax.dot_general(pc.astype(BF16), v_t.astype(BF16), nt, preferred_element_type=F32)
             + lax.dot_general(pn.astype(BF16), vn_t.astype(BF16), nt, preferred_element_type=F32))
        o_ref[rows, :] = per_query(o / l)
        lse_ref[rows, :] = per_query(jnp.broadcast_to(m + jnp.log(l), (n_rows, C_WIDTH)))
        if roll:
            for kv, old_t, new_t in ((0, k_t, kn_t), (1, v_t, vn_t)):
                for h in range(H_G):
                    chans = slice(h * HEAD_DIM, (h + 1) * HEAD_DIM)
                    rolled = pltpu.roll(old_t[chans], buf_len - t_new, 1)
                    if buf_len > CHUNK:
                        newc_ref[sb, kv, h, :, 0:buf_len - CHUNK] = rolled[:, 0:buf_len - CHUNK]
                    newc_ref[sb, kv, h, :, buf_len - CHUNK:buf_len] = jnp.where(
                        lane >= new_lane0, new_t[chans], rolled[:, buf_len - CHUNK:buf_len])


def _attn_sample(qkv, cache, layer_idx, g, t_new, roll, prev_out=None):
    n_layers, n_seq, _, _, _, buf_len = cache.shape
    d = DILATIONS[g]
    n_sb = max(1, min(n_seq, SAMPLE_STEP_POSITIONS // buf_len))
    assert n_seq % n_sb == 0
    row = lambda sec: pl.BlockSpec((n_sb * t_new, C_WIDTH), lambda b: (b, sec * N_GROUPS + g))
    cache_spec = pl.BlockSpec((None, n_sb, 2, H_G, HEAD_DIM, buf_len), lambda b: (layer_idx, b, 0, 0, 0, 0))
    out_row = pl.BlockSpec((n_sb * t_new, C_WIDTH), lambda b: (b, 0))
    in_specs, args, aliases = [row(0), row(1), row(2), cache_spec], [qkv, qkv, qkv, cache], {}
    out_specs = [out_row, out_row]
    out_shape = [jax.ShapeDtypeStruct((n_seq * t_new, C_WIDTH), F32)] * 2
    if roll:
        out_specs.append(cache_spec)
        out_shape.append(jax.ShapeDtypeStruct(cache.shape, F32))
        if prev_out is not None:
            in_specs.append(pl.BlockSpec(memory_space=pl.ANY))
            args.append(prev_out)
            aliases = {4: 2}
    return pl.pallas_call(
        functools.partial(_sattn_kernel, buf_len=buf_len, dil=d, t_new=t_new, roll=roll),
        grid=(n_seq // n_sb,),
        in_specs=in_specs,
        out_specs=out_specs,
        out_shape=out_shape,
        scratch_shapes=[pltpu.VMEM((H_G * t_new, buf_len), F32), pltpu.VMEM((H_G * t_new, CHUNK), F32)],
        input_output_aliases=aliases,
        compiler_params=_params(1),
        name=f"attn_sample_d{d}",
    )(*args)


def _patch_kernel(*refs, t_new):
    tail_ref, out_ref = refs[-2:]
    new_lane0 = CHUNK - t_new
    pad = jnp.zeros((new_lane0, C_WIDTH), F32)
    lane = lax.broadcasted_iota(jnp.int32, (HEAD_DIM, CHUNK), 1)

    def patch(kn_ref, vn_ref):
        for sb in range(tail_ref.shape[0]):
            rows = slice(sb * t_new, (sb + 1) * t_new)
            for kv, new_ref in ((0, kn_ref), (1, vn_ref)):
                new_t = jnp.concatenate([pad, new_ref[rows, :]], axis=0).T
                for h in range(H_G):
                    out_ref[sb, kv, h] = jnp.where(lane >= new_lane0, new_t[h * HEAD_DIM:(h + 1) * HEAD_DIM],
                                                   tail_ref[sb, kv, h])

    for layer in range((len(refs) - 2) // 2):
        pl.when(pl.program_id(0) == layer)(functools.partial(patch, refs[2 * layer], refs[2 * layer + 1]))


def _patch_rolled(rolled, qkv_layers, g, t_new):
    n_layers, n_seq, _, _, _, buf_len = rolled.shape
    assert n_layers == len(qkv_layers)
    n_sb = min(n_seq, 8)
    assert n_seq % n_sb == 0
    row = lambda sec: pl.BlockSpec((n_sb * t_new, C_WIDTH), lambda l, b: (b, sec * N_GROUPS + g))
    tail_spec = pl.BlockSpec((None, n_sb, 2, H_G, HEAD_DIM, CHUNK),
                             lambda l, b: (l, b, 0, 0, 0, buf_len // CHUNK - 1))
    return pl.pallas_call(
        functools.partial(_patch_kernel, t_new=t_new),
        grid=(n_layers, n_seq // n_sb),
        in_specs=[row(1), row(2)] * n_layers + [tail_spec],
        out_specs=tail_spec,
        out_shape=jax.ShapeDtypeStruct(rolled.shape, rolled.dtype),
        input_output_aliases={2 * n_layers: 0},
        compiler_params=_params(2),
        name=f"patch_rolled_d{DILATIONS[g]}",
    )(*[q for qkv in qkv_layers for q in (qkv, qkv)], rolled)


def _merge_kernel(*refs, tm, dils):
    x_ref = refs[0]
    ol_refs = refs[1:1 + 2 * N_GROUPS]
    w_ref, g_ref, b_ref, out_ref = refs[1 + 2 * N_GROUPS:5 + 2 * N_GROUPS]
    scrs = list(refs[5 + 2 * N_GROUPS:])

    def natural(ref, d):
        if d == 1:
            return lambda rows: ref[rows, :]
        scr = scrs.pop()
        n = tm // d
        slabs = ref.shape[-1] // 128
        for r in range(d):
            for s in range(slabs):
                scr[s, pl.ds(r, n, stride=d), :] = ref[r, :, s * 128:(s + 1) * 128]
        return lambda rows: jnp.concatenate([scr[s, rows, :] for s in range(slabs)], axis=-1)

    loaders = [natural(ol_refs[k], dils[k % N_GROUPS]) for k in range(2 * N_GROUPS)]
    compact = ol_refs[N_GROUPS].shape[-1] != C_WIDTH
    if compact:
        lane = lax.broadcasted_iota(jnp.int32, (2 * LSE_LANES, C_WIDTH), 0) & (LSE_LANES - 1)
        chan = lax.broadcasted_iota(jnp.int32, (2 * LSE_LANES, C_WIDTH), 1)
        spread = jnp.where((chan >> (HEAD_DIM.bit_length() - 1)) == lane, 1.0, 0.0).astype(BF16)

        def expand(a):
            hi = a.astype(BF16)
            lo = (a - hi.astype(F32)).astype(BF16)
            return jnp.dot(jnp.concatenate([hi, lo], axis=-1), spread, preferred_element_type=F32)

    o0, o1, o2, l0, l1, l2 = [load(slice(0, tm)) for load in loaders]
    m = jnp.maximum(jnp.maximum(l0, l1), l2)
    e0, e1, e2 = jnp.exp(l0 - m), jnp.exp(l1 - m), jnp.exp(l2 - m)
    inv = 1.0 / (e0 + e1 + e2)
    alphas = [e0 * inv, e1 * inv, e2 * inv]
    if compact:
        alphas = [expand(a) for a in alphas]
    o = alphas[0] * o0 + alphas[1] * o1 + alphas[2] * o2
    mix = jnp.dot(o.astype(BF16), w_ref[...], preferred_element_type=F32)
    out_ref[...] = _layer_norm(ALPHA * x_ref[...] + mix, g_ref[...], b_ref[...])


def _merge_layer(x, outs, lses, w_out, g, b, dils):
    rows = x.shape[0]
    tm = min(MERGE_TILE, rows)
    per_tile = PERM_TILE // tm
    row_spec = lambda cols: pl.BlockSpec((tm, cols), lambda t: (t, 0))

    def group_arg(a, d):
        cols = a.shape[-1]
        if d == 1:
            return a, row_spec(cols), None
        run = PERM_TILE // d
        view = a.reshape(rows // PERM_TILE, d, run, cols)
        spec = pl.BlockSpec((None, d, tm // d, cols), lambda t: (t // per_tile, 0, t % per_tile, 0))
        return view, spec, pltpu.VMEM((cols // 128, tm, 128), F32)

    args, specs, scratch = zip(*[group_arg(a, dils[k % N_GROUPS]) for k, a in enumerate(list(outs) + list(lses))])
    scratch = [s for s in scratch if s is not None][::-1]
    weights, w_specs = _resident_all((w_out, g, b))
    return pl.pallas_call(
        functools.partial(_merge_kernel, tm=tm, dils=dils),
        grid=(rows // tm,),
        in_specs=[row_spec(D_MODEL)] + list(specs) + w_specs,
        out_specs=row_spec(D_MODEL),
        out_shape=jax.ShapeDtypeStruct((rows, D_MODEL), F32),
        scratch_shapes=scratch,
        compiler_params=_params(1),
        name="merge_layer",
    )(x, *args, *weights)


def kernel(x_prompt, x_sample, state_conv, cache_kv_w128, cache_kv_w512, cache_kv_w2048, w_in_ab, ln_v_g, ln_v_b, w_spatial, b_spatial, conv_w, w_out_ab, w_qkv_c, w_out_c, ln1_g, ln1_b, ln2_g, ln2_b, w_mlp_up, w_mlp_down):
    batch, seq, _ = x_prompt.shape
    n_seq, t_new, _ = x_sample.shape
    n_tok_s = n_seq * t_new
    xp = x_prompt.reshape(batch * seq, D_MODEL)
    xs = x_sample.reshape(n_tok_s, D_MODEL)
    caches = [jnp.transpose(c, (0, 1, 3, 4, 5, 2)) for c in (cache_kv_w128, cache_kv_w512, cache_kv_w2048)]
    rows3 = lambda a: a.reshape(a.shape[0], 1, a.shape[1])
    ln_v_g3, ln_v_b3, ln1_g3, ln1_b3, ln2_g3, ln2_b3 = map(rows3, (ln_v_g, ln_v_b, ln1_g, ln1_b, ln2_g, ln2_b))

    causal = jnp.tril(jnp.ones((CHUNK, CHUNK), dtype=bool))
    w_in_b, w_out_ab_b, w_out_c_b = w_in_ab.astype(BF16), w_out_ab.astype(BF16), w_out_c.astype(BF16)
    cast_later = (w_mlp_up.reshape(-1, D_FF), w_mlp_down.reshape(-1, D_MODEL))
    w_qkv_b = w_qkv_c.astype(BF16)
    n_c = w_qkv_b.shape[0]
    wq5 = w_qkv_b.reshape(n_c, D_MODEL, 3, N_GROUPS, C_WIDTH)
    wk_groups = jnp.transpose(wq5[:, :, 1], (0, 2, 1, 3))
    wqv_t_groups = jnp.transpose(wq5[:, :, 0::2], (0, 3, 2, 4, 1)).reshape(n_c, N_GROUPS, 2 * C_WIDTH, D_MODEL)
    wkv_t_groups = jnp.transpose(wq5[:, :, 1:], (0, 3, 2, 4, 1)).reshape(n_c, N_GROUPS, 2 * C_WIDTH, D_MODEL)
    conv_p, conv_s, chunk_v_s = [], [], []
    c_inputs = []
    qkv_s_layers, rolled = [], None
    new_caches = [None] * N_GROUPS
    hosted = [g for g in range(N_GROUPS) if caches[g].shape[-1] > SAMPLE_STEP_POSITIONS]
    mlp_steps = batch * seq // ROW_TILE
    assert DEPTH * mlp_steps == caches[0].shape[0] * n_seq * 2

    for layer in range(DEPTH):
        i = layer // 2
        if layer % 2 == 0:
            w_tril = jnp.where(causal[None], w_spatial[i], 0.0)
            bmix = jnp.repeat(b_spatial[i].T, CHUNK, axis=1)
            shared = (_Layer(w_in_b, i), _Layer(ln_v_g3, i), _Layer(ln_v_b3, i))
            tail = (_Layer(conv_w, i), _Layer(w_out_ab_b, i), _Layer(ln1_g3, layer), _Layer(ln1_b3, layer))
            xp, buf_p, cast_done = _ab_layer_prompt(xp, batch, seq, shared + (w_tril.astype(BF16), bmix) + tail,
                                                    cast_later if layer == 0 else ())
            if layer == 0:
                w_up_b = cast_done[0].reshape(w_mlp_up.shape)
                w_down_b = cast_done[1].reshape(w_mlp_down.shape)
            eye = jnp.eye(n_seq, dtype=F32)
            w_blk = jnp.stack([jnp.kron(eye, w_tril[g, :t_new, :t_new]) for g in range(G_A)])
            bmix_s = jnp.tile(bmix[:t_new], (n_seq, 1))
            st = state_conv[i]
            s1 = jnp.concatenate([st[:, 1:2], jnp.zeros((n_seq, t_new - 1, W_B), F32)], axis=1)
            s2 = jnp.concatenate([st, jnp.zeros((n_seq, t_new - 2, W_B), F32)], axis=1)
            xs, hc_s, v_s = _ab_layer_sample(xs, s1.reshape(n_tok_s, W_B), s2.reshape(n_tok_s, W_B),
                                             shared + (w_blk.astype(BF16), bmix_s) + tail)
            conv_p.append(buf_p)
            conv_s.append(hc_s.reshape(n_seq, t_new, W_B)[:, t_new - 2:])
            chunk_v_s.append(v_s.reshape(n_seq, t_new, W_A))
        else:
            wo = _Layer(w_out_c_b, i)
            g1, b1 = _Layer(ln1_g3, layer), _Layer(ln1_b3, layer)
            c_inputs.append(xp)
            k_all, qt_all, vt_all = _qkv_proj_perm(xp, wk_groups, wqv_t_groups, i, batch, seq)
            outs, lses = zip(*[_attn_prompt(k_all, qt_all, vt_all, batch, seq, g) for g in range(N_GROUPS)])
            xp = _merge_layer(xp, outs, lses, wo, g1, b1, DILATIONS)
            qkv_s = _qkv_proj(xs, _Layer(w_qkv_b, i))
            qkv_s_layers.append(qkv_s)
            outs, lses = [], []
            for g in range(N_GROUPS):
                res = _attn_sample(qkv_s, caches[g], i, g, t_new, g not in hosted, new_caches[g])
                outs.append(res[0])
                lses.append(res[1])
                if g not in hosted:
                    new_caches[g] = res[2]
            xs = _merge_layer(xs, outs, lses, wo, g1, b1, (1,) * N_GROUPS)
        w_up, w_down = _Layer(w_up_b, layer), _Layer(w_down_b, layer)
        g2, b2 = _Layer(ln2_g3, layer), _Layer(ln2_b3, layer)
        xp, xs, rolled = _mlp_layer(xp, xs, w_up, w_down, g2, b2,
                                    roll=([caches[g] for g in hosted], rolled, layer * mlp_steps, t_new))

    for k, g in enumerate(hosted):
        new_caches[g] = _patch_rolled(rolled[k], qkv_s_layers, g, t_new)
    kv_s = [jnp.transpose(nc, (0, 1, 5, 2, 3, 4)) for nc in new_caches]
    tails = _kv_tail(c_inputs, wkv_t_groups, batch, seq)
    kv_p = [jnp.transpose(t.reshape(n_c, batch, 2, H_G, HEAD_DIM, WINDOWS[g]), (0, 1, 5, 2, 3, 4))
            for g, t in enumerate(tails)]
    return (xp.reshape(batch, seq, D_MODEL), xs.reshape(n_seq, t_new, D_MODEL),
            jnp.stack(conv_p), jnp.stack(conv_s), jnp.stack(chunk_v_s),
            kv_p[0], kv_p[1], kv_p[2], kv_s[0], kv_s[1], kv_s[2])
```

```python
import functools
import math
from typing import NamedTuple

import jax
import jax.numpy as jnp
from jax import lax
from jax.experimental import pallas as pl
from jax.experimental.pallas import tpu as pltpu

F32 = jnp.float32
BF16 = jnp.bfloat16

D_MODEL = 1024
DEPTH = 4
CHUNK = 128
W_A = 512
G_A = 4
W_B = 512
N_GROUPS = 3
WINDOWS = (128, 512, 2048)
DILATIONS = (1, 4, 16)
N_BACK = 128
H_G = 8
HEAD_DIM = 64
C_WIDTH = H_G * HEAD_DIM
QKV_COLS = N_GROUPS * C_WIDTH
QKV_BLOCKS = 3 * N_GROUPS
D_FF = 4 * D_MODEL
ALPHA = (2.0 * DEPTH) ** 0.25
LN_EPS = 1e-5
NEG_INF = -1e30
SLOPES = tuple(2.0 ** (-(8.0 / H_G) * j) for j in range(1, H_G + 1))
QK_SCALE = HEAD_DIM ** -0.5
LOG2E = math.log2(math.e)
LN2 = math.log(2.0)

VMEM_LIMIT_BYTES = 52 * 1024 * 1024
ROW_TILE = 512
PERM_TILE = CHUNK * max(DILATIONS)
X_SLABS = D_MODEL // 128
MERGE_TILE = 512
AB_PARTS = 2
MLP_PARTS = 2
MERGE_PARTS = 2
SAMPLE_STEP_POSITIONS = 1024
LSE_LANES = 128


def _layer_norm(x, g, b):
    mu = jnp.mean(x, axis=-1, keepdims=True)
    xc = x - mu
    var = jnp.mean(xc * xc, axis=-1, keepdims=True)
    return xc * lax.rsqrt(var + LN_EPS) * g + b


def _gelu(x):
    c = math.sqrt(2.0 / math.pi)
    return x * (0.5 * (1.0 + jnp.tanh(c * (x + 0.044715 * (x * x * x)))))


class _Layer(NamedTuple):
    stacked: jax.Array
    index: int


def _resident(w):
    if isinstance(w, _Layer):
        shape = w.stacked.shape[1:]
        return w.stacked, pl.BlockSpec((None,) + shape, lambda *_: (w.index,) + (0,) * len(shape),
                                       pipeline_mode=pl.Buffered(1))
    return w, pl.BlockSpec(w.shape, lambda *_: (0,) * w.ndim, pipeline_mode=pl.Buffered(1))


def _resident_all(ws):
    arrays, specs = zip(*[_resident(w) for w in ws])
    return list(arrays), list(specs)


def _params(n_grid):
    return pltpu.CompilerParams(dimension_semantics=("arbitrary",) * n_grid,
                                vmem_limit_bytes=VMEM_LIMIT_BYTES)


def _ab_kernel(*refs, tm, chunk, sample, n_parts):
    if sample:
        (x_ref, s1_ref, s2_ref, w_in_ref, lvg_ref, lvb_ref, wmix_ref, bmix_ref, cw_ref, w_out_ref,
         g1_ref, b1_ref, o_ref, hc_ref, v_ref, y_scr) = refs
    else:
        (x_ref, w_in_ref, lvg_ref, lvb_ref, wmix_ref, bmix_ref, cw_ref, w_out_ref, g1_ref, b1_ref) = refs[:10]
        n_cast = (len(refs) - 14) // 2
        cast_in, cast_out = refs[10:10 + n_cast], refs[12 + n_cast:12 + 2 * n_cast]
        o_ref, cb_ref = refs[10 + n_cast:12 + n_cast]
        y_scr, h_scr = refs[-2:]
        for src, dst in zip(cast_in, cast_out):
            dst[...] = src[...].astype(BF16)

    if not sample:
        @pl.when(pl.program_id(1) == 0)
        def _():
            h_scr[0:8, :] = jnp.zeros((8, W_B), F32)

    n = tm // n_parts

    def project(r0):
        x = x_ref[r0:r0 + n, :]
        xb = x.astype(BF16)
        return x, [jnp.dot(xb, w_in_ref[:, k * W_A:(k + 1) * W_A], preferred_element_type=F32)
                   for k in range(5)]

    def mixers(r0, z):
        z_u, z_v, z_bg, z_cg, z_h = z
        u = _gelu(z_u)
        v = _layer_norm(_gelu(z_v), lvg_ref[...], lvb_ref[...])
        if sample:
            v_ref[...] = v
        vb = v.astype(BF16)
        for c in range(n // chunk):
            rows = slice(c * chunk, (c + 1) * chunk)
            for g in range(G_A):
                cols = slice(g * 128, (g + 1) * 128)
                y_scr[r0 + c * chunk:r0 + (c + 1) * chunk, cols] = (
                    jnp.dot(wmix_ref[g], vb[rows, cols], preferred_element_type=F32) + bmix_ref[:, cols])
        a_out = (u * y_scr[r0:r0 + n, :]).astype(BF16)

        hc = z_cg * z_h
        if sample:
            hc_ref[...] = hc
            pos = lax.broadcasted_iota(jnp.int32, hc.shape, 0) & 7
            sh1 = jnp.where(pos == 0, s1_ref[...], pltpu.roll(hc, 1, 0))
            sh2 = jnp.where(pos < 2, s2_ref[...], pltpu.roll(hc, 2, 0))
        else:
            h_scr[8 + r0:8 + r0 + n, :] = hc
            sh1 = h_scr[7 + r0:7 + r0 + n, :]
            sh2 = h_scr[6 + r0:6 + r0 + n, :]
        conv = cw_ref[0:1, :] * sh2 + cw_ref[1:2, :] * sh1 + cw_ref[2:3, :] * hc
        b_out = (z_bg * conv).astype(BF16)
        return (jnp.dot(a_out, w_out_ref[0:W_A, :], preferred_element_type=F32)
                + jnp.dot(b_out, w_out_ref[W_A:W_A + W_B, :], preferred_element_type=F32))

    def finish(r0, x, mix):
        o_ref[r0:r0 + n, :] = _layer_norm(ALPHA * x + mix, g1_ref[...], b1_ref[...])

    starts = [p * n for p in range(n_parts)]
    projected = [project(r0) for r0 in starts]
    for r0, (x, z) in zip(starts, projected):
        finish(r0, x, mixers(r0, z))
    if not sample:
        cb_ref[...] = h_scr[tm + 6:tm + 8, :]
        h_scr[0:8, :] = h_scr[tm:tm + 8, :]


def _ab_layer_prompt(x, batch, seq, weights, cast=()):
    tm = ROW_TILE
    nt = seq // tm
    steps = batch * nt
    row_spec = pl.BlockSpec((tm, D_MODEL), lambda b, t: (b * nt + t, 0))
    weights, w_specs = _resident_all(weights)
    cast_specs = [pl.BlockSpec((a.shape[0] // steps, a.shape[1]), lambda b, t: (b * nt + t, 0)) for a in cast]
    assert all(a.shape[0] % (16 * steps) == 0 for a in cast)
    outs = pl.pallas_call(
        functools.partial(_ab_kernel, tm=tm, chunk=CHUNK, sample=False, n_parts=AB_PARTS),
        grid=(batch, nt),
        in_specs=[row_spec] + w_specs + cast_specs,
        out_specs=[row_spec, pl.BlockSpec((None, 2, W_B), lambda b, t: (b, 0, 0))] + cast_specs,
        out_shape=[jax.ShapeDtypeStruct((batch * seq, D_MODEL), F32),
                   jax.ShapeDtypeStruct((batch, 2, W_B), F32)]
                  + [jax.ShapeDtypeStruct(a.shape, BF16) for a in cast],
        scratch_shapes=[pltpu.VMEM((tm, W_A), F32), pltpu.VMEM((tm + 8, W_B), F32)],
        compiler_params=_params(2),
        name="ab_layer_prompt",
    )(x, *weights, *cast)
    return outs[0], outs[1], list(outs[2:])


def _ab_layer_sample(x, s1, s2, weights):
    tm = x.shape[0]
    full = lambda cols: pl.BlockSpec((tm, cols), lambda i: (0, 0))
    weights, w_specs = _resident_all(weights)
    return pl.pallas_call(
        functools.partial(_ab_kernel, tm=tm, chunk=tm, sample=True, n_parts=1),
        grid=(1,),
        in_specs=[full(D_MODEL), full(W_B), full(W_B)] + w_specs,
        out_specs=[full(D_MODEL), full(W_B), full(W_A)],
        out_shape=[jax.ShapeDtypeStruct((tm, D_MODEL), F32),
                   jax.ShapeDtypeStruct((tm, W_B), F32),
                   jax.ShapeDtypeStruct((tm, W_A), F32)],
        scratch_shapes=[pltpu.VMEM((tm, W_A), F32)],
        compiler_params=_params(1),
        name="ab_layer_sample",
    )(x, s1, s2, *weights)


def _mlp_kernel(x_ref, xs_ref, w_up_ref, w_down_ref, g_ref, b_ref, *rest, n_roll, roll_by):
    old_refs, new_refs = rest[:n_roll], rest[len(rest) - n_roll:]
    o_ref, os_ref = rest[len(rest) - n_roll - 2], rest[len(rest) - n_roll - 1]
    for old, new in zip(old_refs, new_refs):
        buf_len = old.shape[-1]
        for h in range(H_G):
            new[h] = pltpu.roll(old[h], buf_len - roll_by, 1)

    def mlp_rows(src, dst, n_parts):
        n = src.shape[0] // n_parts
        xs = [src[p * n:(p + 1) * n, :] for p in range(n_parts)]
        xbs = [x.astype(BF16) for x in xs]
        accs = [None] * n_parts
        for c in range(D_FF // D_MODEL):
            cols = slice(c * D_MODEL, (c + 1) * D_MODEL)
            hs = [jnp.maximum(jnp.dot(xb, w_up_ref[:, cols], preferred_element_type=F32), 0.0) for xb in xbs]
            parts = [jnp.dot((h * h).astype(BF16), w_down_ref[cols, :], preferred_element_type=F32) for h in hs]
            accs = [part if acc is None else acc + part for acc, part in zip(accs, parts)]
        for p in range(n_parts):
            dst[p * n:(p + 1) * n, :] = _layer_norm(ALPHA * xs[p] + accs[p], g_ref[...], b_ref[...])

    mlp_rows(x_ref, o_ref, MLP_PARTS)

    @pl.when(pl.program_id(0) == pl.num_programs(0) - 1)
    def _():
        mlp_rows(xs_ref, os_ref, 1)


def _mlp_layer(x, x_small, w_up, w_down, g, b, roll):
    rows = x.shape[0]
    tm = min(ROW_TILE, rows)
    row_spec = pl.BlockSpec((tm, D_MODEL), lambda t: (t, 0))
    small_spec = pl.BlockSpec(x_small.shape, lambda t: (0, 0))
    weights, w_specs = _resident_all((w_up, w_down, g, b))
    bufs, rolled, first, roll_by = roll
    n_roll = len(bufs)
    n_seq = bufs[0].shape[1]

    def block_spec(buf):
        def index(t):
            blk = first + t
            return (blk // (2 * n_seq), (blk % (2 * n_seq)) // 2, blk % 2, 0, 0, 0)
        return pl.BlockSpec((None, None, None) + buf.shape[3:], index)

    args = [x, x_small] + weights + list(bufs)
    in_specs = [row_spec, small_spec] + w_specs + [block_spec(buf) for buf in bufs]
    aliases = {}
    if rolled is not None:
        aliases = {len(args) + k: 2 + k for k in range(n_roll)}
        args += list(rolled)
        in_specs += [pl.BlockSpec(memory_space=pl.ANY)] * n_roll
    outs = pl.pallas_call(
        functools.partial(_mlp_kernel, n_roll=n_roll, roll_by=roll_by),
        grid=(rows // tm,),
        in_specs=in_specs,
        out_specs=[row_spec, small_spec] + [block_spec(buf) for buf in bufs],
        out_shape=[jax.ShapeDtypeStruct(x.shape, F32), jax.ShapeDtypeStruct(x_small.shape, F32)]
                  + [jax.ShapeDtypeStruct(buf.shape, buf.dtype) for buf in bufs],
        input_output_aliases=aliases,
        compiler_params=_params(1),
        name="mlp_layer",
    )(*args)
    return outs[0], outs[1], list(outs[2:])


def _qkv_kernel(x_ref, w_ref, o_ref):
    xb = x_ref[...].astype(BF16)
    for c in range(QKV_BLOCKS):
        cols = slice(c * C_WIDTH, (c + 1) * C_WIDTH)
        o_ref[:, cols] = jnp.dot(xb, w_ref[:, cols], preferred_element_type=F32)


def _qkv_proj(x, w):
    rows = x.shape[0]
    tm = min(256, rows)
    w, w_spec = _resident(w)
    return pl.pallas_call(
        _qkv_kernel,
        grid=(rows // tm,),
        in_specs=[pl.BlockSpec((tm, D_MODEL), lambda t: (t, 0)), w_spec],
        out_specs=pl.BlockSpec((tm, 3 * QKV_COLS), lambda t: (t, 0)),
        out_shape=jax.ShapeDtypeStruct((rows, 3 * QKV_COLS), F32),
        compiler_params=_params(1),
        name="qkv_proj",
    )(x, w)


def _qkv_perm_kernel(*refs):
    x_refs, (wk_ref, wqv_t_ref, k_ref, qt_ref, vt_ref, xp_scr) = refs[:X_SLABS], refs[X_SLABS:]
    nt_dims = (((1,), (1,)), ((), ()))

    def step(d):
        n = PERM_TILE // d
        for c in range(PERM_TILE // ROW_TILE):
            rows = slice(c * ROW_TILE, (c + 1) * ROW_TILE)
            for r in range(d):
                lo, hi = max(r * n, rows.start), min((r + 1) * n, rows.stop)
                if lo >= hi:
                    continue
                for s in range(X_SLABS):
                    src = x_refs[s]
                    vals = (src[lo:hi, :] if d == 1 else src[pl.ds(r + (lo - r * n) * d, hi - lo, stride=d), :])
                    xp_scr[lo:hi, s * 128:(s + 1) * 128] = vals.astype(BF16)
            xs = xp_scr[rows, :]
            k_ref[rows, :] = jnp.dot(xs, wk_ref[...], preferred_element_type=F32).astype(BF16)
            qv_t = lax.dot_general(wqv_t_ref[...], xs, nt_dims, preferred_element_type=F32)
            qt_ref[:, rows] = (qv_t[0:C_WIDTH] * (QK_SCALE * LOG2E)).astype(BF16)
            vt_ref[:, rows] = qv_t[C_WIDTH:2 * C_WIDTH].astype(BF16)

    for g in range(N_GROUPS):
        pl.when(pl.program_id(2) == g)(functools.partial(step, DILATIONS[g]))


def _qkv_proj_perm(x, wk_groups, wqv_t_groups, layer_idx, batch, seq):
    nt = seq // PERM_TILE
    slab = lambda s: pl.BlockSpec((PERM_TILE, 128), lambda b, t, g: (b * nt + t, s))
    chan_spec = pl.BlockSpec((None, C_WIDTH, PERM_TILE), lambda b, t, g: (g, 0, b * nt + t))
    chan_shape = jax.ShapeDtypeStruct((N_GROUPS, C_WIDTH, batch * seq), BF16)
    return pl.pallas_call(
        _qkv_perm_kernel,
        grid=(batch, nt, N_GROUPS),
        in_specs=[slab(s) for s in range(X_SLABS)]
                 + [pl.BlockSpec((None, None, D_MODEL, C_WIDTH), lambda b, t, g: (layer_idx, g, 0, 0)),
                    pl.BlockSpec((None, None, 2 * C_WIDTH, D_MODEL), lambda b, t, g: (layer_idx, g, 0, 0))],
        out_specs=[pl.BlockSpec((None, PERM_TILE, C_WIDTH), lambda b, t, g: (g, b * nt + t, 0)),
                   chan_spec, chan_spec],
        out_shape=[jax.ShapeDtypeStruct((N_GROUPS, batch * seq, C_WIDTH), BF16), chan_shape, chan_shape],
        scratch_shapes=[pltpu.VMEM((PERM_TILE, D_MODEL), BF16)],
        compiler_params=_params(3),
        name="qkv_proj_perm",
    )(*([x] * X_SLABS), wk_groups, wqv_t_groups)


def _kv_tail_kernel(*refs):
    n_layers = len(refs) - 4
    x_refs, wt_ref, (o0_ref, o1_ref, o2_ref) = refs[:n_layers], refs[n_layers], refs[n_layers + 1:]
    nt = (((1,), (1,)), ((), ()))
    for layer in range(n_layers):
        @pl.when(pl.program_id(0) == layer)
        def _(x_ref=x_refs[layer]):
            xb = x_ref[...].astype(BF16)
            o2_ref[...] = lax.dot_general(wt_ref[2], xb, nt, preferred_element_type=F32)

            @pl.when(pl.program_id(2) == pl.num_programs(2) - 1)
            def _():
                o1_ref[...] = lax.dot_general(wt_ref[1], xb, nt, preferred_element_type=F32)
                o0_ref[...] = lax.dot_general(wt_ref[0], xb[ROW_TILE - WINDOWS[0]:, :], nt,
                                              preferred_element_type=F32)


def _kv_tail(x_layers, wt_groups, batch, seq):
    n_layers = len(x_layers)
    nj = WINDOWS[2] // ROW_TILE
    first = (seq - WINDOWS[2]) // ROW_TILE
    nblk = seq // ROW_TILE
    last = lambda n: pl.BlockSpec((None, None, 2 * C_WIDTH, n), lambda l, b, j: (l, b, 0, 0))
    def x_spec(layer):
        return pl.BlockSpec((ROW_TILE, D_MODEL),
                            lambda l, b, j: (jnp.where(l == layer, b * nblk + first + j, 0), 0))
    return pl.pallas_call(
        _kv_tail_kernel,
        grid=(n_layers, batch, nj),
        in_specs=[x_spec(layer) for layer in range(n_layers)]
                 + [pl.BlockSpec((None,) + wt_groups.shape[1:], lambda l, b, j: (l, 0, 0, 0))],
        out_specs=[last(WINDOWS[0]), last(WINDOWS[1]),
                   pl.BlockSpec((None, None, 2 * C_WIDTH, ROW_TILE), lambda l, b, j: (l, b, 0, j))],
        out_shape=[jax.ShapeDtypeStruct((n_layers, batch, 2 * C_WIDTH, n), F32) for n in WINDOWS],
        compiler_params=_params(3),
        name="kv_tail",
    )(*x_layers, wt_groups)


def _attn_kernel(qt_ref, kc_ref, kp_ref, vtc_ref, vtp_ref, o_ref, lse_ref, bias_scr, *, dil, chain):
    first_step = (pl.program_id(0) == 0) & (pl.program_id(1) == 0) & (pl.program_id(2) == 0)

    @pl.when(first_step)
    def _():
        kj = lax.broadcasted_iota(jnp.int32, (CHUNK, CHUNK), 0)
        qi = lax.broadcasted_iota(jnp.int32, (CHUNK, CHUNK), 1)
        back_own = qi - kj
        back_prev = back_own + CHUNK
        for h in range(H_G):
            slope = SLOPES[h] * dil * LOG2E
            bias_scr[0, h] = jnp.where(back_prev <= N_BACK, -slope * back_prev.astype(F32), NEG_INF)
            bias_scr[1, h] = jnp.full((CHUNK, CHUNK), NEG_INF, F32)
            bias_scr[2, h] = jnp.where(back_own >= 0, -slope * back_own.astype(F32), NEG_INF)

    n_sub = ROW_TILE // CHUNK
    if chain:
        run_steps = PERM_TILE // dil // ROW_TILE
        at_start = (pl.program_id(1) == 0) & (pl.program_id(2) % run_steps == 0)
    else:
        at_start = pl.program_id(1) == 0
    zeros_half = jnp.zeros((HEAD_DIM, CHUNK), BF16)
    lse_pad = jnp.zeros((CHUNK - H_G, CHUNK), F32)

    def half_tile(k_rows, qt_h, vt_cols, bias):
        st = jnp.dot(k_rows, qt_h, preferred_element_type=F32) + bias
        m = jnp.max(st, axis=0, keepdims=True)
        p = jnp.exp2(st - m)
        l = jnp.sum(p, axis=0, keepdims=True)
        return m, l, jnp.dot(vt_cols, p.astype(BF16), preferred_element_type=F32)

    for s in range(n_sub):
        own = slice(s * CHUNK, (s + 1) * CHUNK)
        if chain and s > 0:
            prev_k, prev_vt = kc_ref, vtc_ref
            prv = slice((s - 1) * CHUNK, s * CHUNK)
        else:
            prev_k, prev_vt = kp_ref, vtp_ref
            prv = own if not chain else slice(0, CHUNK)
        masked = at_start if (s == 0 or not chain) else None
        lses = []
        for pair in range(H_G // 2):
            lanes = slice(pair * 2 * HEAD_DIM, (pair + 1) * 2 * HEAD_DIM)
            qt_pair = qt_ref[lanes, own]
            outs = []
            for half in range(2):
                h = 2 * pair + half
                chans = slice(h * HEAD_DIM, (h + 1) * HEAD_DIM)
                qt_h = (jnp.concatenate([qt_pair[0:HEAD_DIM], zeros_half], axis=0) if half == 0
                        else jnp.concatenate([zeros_half, qt_pair[HEAD_DIM:]], axis=0))
                bias_prev = bias_scr[0, h] if masked is None else bias_scr[jnp.where(masked, 1, 0), h]
                m_a, l_a, o_a = half_tile(prev_k[prv, lanes], qt_h, prev_vt[chans, prv], bias_prev)
                m_b, l_b, o_b = half_tile(kc_ref[own, lanes], qt_h, vtc_ref[chans, own], bias_scr[2, h])
                m = jnp.maximum(m_a, m_b)
                w_a, w_b = jnp.exp2(m_a - m), jnp.exp2(m_b - m)
                l = w_a * l_a + w_b * l_b
                outs.append((w_a * o_a + w_b * o_b) * (1.0 / l))
                lses.append((m + jnp.log2(l)) * LN2)
            o_ref[own, lanes] = jnp.concatenate(outs, axis=0).T
        lse_ref[own, :] = jnp.concatenate(lses + [lse_pad], axis=0).T


def _attn_prompt(k_all, qt_all, vt_all, batch, seq, g):
    d = DILATIONS[g]
    nt = seq // PERM_TILE
    nj = PERM_TILE // ROW_TILE
    sub_per_step = ROW_TILE // CHUNK
    run = PERM_TILE // d
    chain = run > CHUNK
    assert run % ROW_TILE == 0 if chain else run == CHUNK
    cur = lambda b, t, j: (b * nt + t) * nj + j
    if chain:
        back = (PERM_TILE - run) // CHUNK + 1
        prev_rows = CHUNK
        prev = lambda b, t, j: jnp.maximum(cur(b, t, j) * sub_per_step - back, 0)
    else:
        prev_rows = ROW_TILE
        prev = lambda b, t, j: jnp.maximum(cur(b, t, j) - nj, 0)
    tok = lambda rows, idx: pl.BlockSpec((None, rows, C_WIDTH), lambda b, t, j: (g, idx(b, t, j), 0))
    chn = lambda cols, idx: pl.BlockSpec((None, C_WIDTH, cols), lambda b, t, j: (g, 0, idx(b, t, j)))
    return pl.pallas_call(
        functools.partial(_attn_kernel, dil=d, chain=chain),
        grid=(batch, nt, nj),
        in_specs=[chn(ROW_TILE, cur), tok(ROW_TILE, cur), tok(prev_rows, prev),
                  chn(ROW_TILE, cur), chn(prev_rows, prev)],
        out_specs=[pl.BlockSpec((ROW_TILE, C_WIDTH), lambda b, t, j: (cur(b, t, j), 0)),
                   pl.BlockSpec((ROW_TILE, LSE_LANES), lambda b, t, j: (cur(b, t, j), 0))],
        out_shape=[jax.ShapeDtypeStruct((batch * seq, C_WIDTH), F32),
                   jax.ShapeDtypeStruct((batch * seq, LSE_LANES), F32)],
        scratch_shapes=[pltpu.VMEM((3, H_G, CHUNK, CHUNK), F32)],
        compiler_params=_params(3),
        name=f"attn_prompt_d{d}",
    )(qt_all, k_all, k_all, vt_all, vt_all)


def _sattn_kernel(*refs, buf_len, dil, t_new, roll):
    q_ref, kn_ref, vn_ref, cache_ref = refs[:4]
    bias_c, bias_n = refs[-2:]
    o_ref, lse_ref = refs[-5:-3] if roll else refs[-4:-2]
    newc_ref = refs[-3] if roll else None
    new_lane0 = CHUNK - t_new

    @pl.when(pl.program_id(0) == 0)
    def _():
        def bias(dist, ok, h):
            b0 = jnp.where((dist & (dil - 1)) == 0, -SLOPES[h] * dist.astype(F32), NEG_INF)
            b0 = jnp.where(dist >= 0, b0, NEG_INF)
            b0 = jnp.where(dist <= N_BACK * dil, b0, NEG_INF)
            return b0 if ok is None else jnp.where(ok, b0, NEG_INF)
        t_c = lax.broadcasted_iota(jnp.int32, (t_new, buf_len), 0)
        p_c = lax.broadcasted_iota(jnp.int32, (t_new, buf_len), 1)
        t_n = lax.broadcasted_iota(jnp.int32, (t_new, CHUNK), 0)
        j_n = lax.broadcasted_iota(jnp.int32, (t_new, CHUNK), 1)
        for h in range(H_G):
            bias_c[h * t_new:(h + 1) * t_new, :] = bias(buf_len + t_c - p_c, None, h)
            bias_n[h * t_new:(h + 1) * t_new, :] = bias(t_n - (j_n - new_lane0), j_n >= new_lane0, h)

    pad = jnp.zeros((new_lane0, C_WIDTH), F32)
    nt = (((1,), (1,)), ((), ()))
    n_rows = H_G * t_new
    assert t_new & (t_new - 1) == 0 and HEAD_DIM & (HEAD_DIM - 1) == 0
    row_head = lax.broadcasted_iota(jnp.int32, (n_rows, C_WIDTH), 0) >> (t_new.bit_length() - 1)
    col_head = lax.broadcasted_iota(jnp.int32, (n_rows, C_WIDTH), 1) >> (HEAD_DIM.bit_length() - 1)
    own_head = row_head == col_head
    lane = lax.broadcasted_iota(jnp.int32, (HEAD_DIM, CHUNK), 1)

    def per_query(a):
        return jnp.sum(jnp.where(own_head, a, 0.0).reshape(H_G, t_new, C_WIDTH), axis=0)

    for sb in range(cache_ref.shape[0]):
        rows = slice(sb * t_new, (sb + 1) * t_new)
        kn_t = jnp.concatenate([pad, kn_ref[rows, :]], axis=0).T
        vn_t = jnp.concatenate([pad, vn_ref[rows, :]], axis=0).T
        q = q_ref[rows, :] * QK_SCALE
        q_bd = jnp.where(own_head, jnp.concatenate([q] * H_G, axis=0), 0.0).astype(BF16)
        k_t = cache_ref[sb, 0].reshape(C_WIDTH, buf_len)
        v_t = cache_ref[sb, 1].reshape(C_WIDTH, buf_len)
        lc = jnp.dot(q_bd, k_t.astype(BF16), preferred_element_type=F32) + bias_c[...]
        ln = jnp.dot(q_bd, kn_t.astype(BF16), preferred_element_type=F32) + bias_n[...]
        m = jnp.maximum(jnp.max(lc, axis=-1, keepdims=True), jnp.max(ln, axis=-1, keepdims=True))
        pc = jnp.exp(lc - m)
        pn = jnp.exp(ln - m)
        l = jnp.sum(pc, axis=-1, keepdims=True) + jnp.sum(pn, axis=-1, keepdims=True)
        o = (lax.dot_general(pc.astype(BF16), v_t.astype(BF16), nt, preferred_element_type=F32)
             + lax.dot_general(pn.astype(BF16), vn_t.astype(BF16), nt, preferred_element_type=F32))
        o_ref[rows, :] = per_query(o / l)
        lse_ref[rows, :] = per_query(jnp.broadcast_to(m + jnp.log(l), (n_rows, C_WIDTH)))
        if roll:
            for kv, old_t, new_t in ((0, k_t, kn_t), (1, v_t, vn_t)):
                for h in range(H_G):
                    chans = slice(h * HEAD_DIM, (h + 1) * HEAD_DIM)
                    rolled = pltpu.roll(old_t[chans], buf_len - t_new, 1)
                    if buf_len > CHUNK:
                        newc_ref[sb, kv, h, :, 0:buf_len - CHUNK] = rolled[:, 0:buf_len - CHUNK]
                    newc_ref[sb, kv, h, :, buf_len - CHUNK:buf_len] = jnp.where(
                        lane >= new_lane0, new_t[chans], rolled[:, buf_len - CHUNK:buf_len])


def _attn_sample(qkv, cache, layer_idx, g, t_new, roll, prev_out=None):
    n_layers, n_seq, _, _, _, buf_len = cache.shape
    d = DILATIONS[g]
    n_sb = max(1, min(n_seq, SAMPLE_STEP_POSITIONS // buf_len))
    assert n_seq % n_sb == 0
    row = lambda sec: pl.BlockSpec((n_sb * t_new, C_WIDTH), lambda b: (b, sec * N_GROUPS + g))
    cache_spec = pl.BlockSpec((None, n_sb, 2, H_G, HEAD_DIM, buf_len), lambda b: (layer_idx, b, 0, 0, 0, 0))
    out_row = pl.BlockSpec((n_sb * t_new, C_WIDTH), lambda b: (b, 0))
    in_specs, args, aliases = [row(0), row(1), row(2), cache_spec], [qkv, qkv, qkv, cache], {}
    out_specs = [out_row, out_row]
    out_shape = [jax.ShapeDtypeStruct((n_seq * t_new, C_WIDTH), F32)] * 2
    if roll:
        out_specs.append(cache_spec)
        out_shape.append(jax.ShapeDtypeStruct(cache.shape, F32))
        if prev_out is not None:
            in_specs.append(pl.BlockSpec(memory_space=pl.ANY))
            args.append(prev_out)
            aliases = {4: 2}
    return pl.pallas_call(
        functools.partial(_sattn_kernel, buf_len=buf_len, dil=d, t_new=t_new, roll=roll),
        grid=(n_seq // n_sb,),
        in_specs=in_specs,
        out_specs=out_specs,
        out_shape=out_shape,
        scratch_shapes=[pltpu.VMEM((H_G * t_new, buf_len), F32), pltpu.VMEM((H_G * t_new, CHUNK), F32)],
        input_output_aliases=aliases,
        compiler_params=_params(1),
        name=f"attn_sample_d{d}",
    )(*args)


def _patch_kernel(*refs, t_new):
    tail_ref, out_ref = refs[-2:]
    new_lane0 = CHUNK - t_new
    pad = jnp.zeros((new_lane0, C_WIDTH), F32)
    lane = lax.broadcasted_iota(jnp.int32, (HEAD_DIM, CHUNK), 1)

    def patch(kn_ref, vn_ref):
        for sb in range(tail_ref.shape[0]):
            rows = slice(sb * t_new, (sb + 1) * t_new)
            for kv, new_ref in ((0, kn_ref), (1, vn_ref)):
                new_t = jnp.concatenate([pad, new_ref[rows, :]], axis=0).T
                for h in range(H_G):
                    out_ref[sb, kv, h] = jnp.where(lane >= new_lane0, new_t[h * HEAD_DIM:(h + 1) * HEAD_DIM],
                                                   tail_ref[sb, kv, h])

    for layer in range((len(refs) - 2) // 2):
        pl.when(pl.program_id(0) == layer)(functools.partial(patch, refs[2 * layer], refs[2 * layer + 1]))


def _patch_rolled(rolled, qkv_layers, g, t_new):
    n_layers, n_seq, _, _, _, buf_len = rolled.shape
    assert n_layers == len(qkv_layers)
    n_sb = min(n_seq, 8)
    assert n_seq % n_sb == 0
    row = lambda sec: pl.BlockSpec((n_sb * t_new, C_WIDTH), lambda l, b: (b, sec * N_GROUPS + g))
    tail_spec = pl.BlockSpec((None, n_sb, 2, H_G, HEAD_DIM, CHUNK),
                             lambda l, b: (l, b, 0, 0, 0, buf_len // CHUNK - 1))
    return pl.pallas_call(
        functools.partial(_patch_kernel, t_new=t_new),
        grid=(n_layers, n_seq // n_sb),
        in_specs=[row(1), row(2)] * n_layers + [tail_spec],
        out_specs=tail_spec,
        out_shape=jax.ShapeDtypeStruct(rolled.shape, rolled.dtype),
        input_output_aliases={2 * n_layers: 0},
        compiler_params=_params(2),
        name=f"patch_rolled_d{DILATIONS[g]}",
    )(*[q for qkv in qkv_layers for q in (qkv, qkv)], rolled)


def _merge_kernel(*refs, tm, dils):
    x_ref = refs[0]
    ol_refs = refs[1:1 + 2 * N_GROUPS]
    w_ref, g_ref, b_ref, out_ref = refs[1 + 2 * N_GROUPS:5 + 2 * N_GROUPS]
    scrs = list(refs[5 + 2 * N_GROUPS:])

    def natural(ref, d):
        if d == 1:
            return lambda rows: ref[rows, :]
        scr = scrs.pop()
        n = tm // d
        slabs = ref.shape[-1] // 128
        for r in range(d):
            for s in range(slabs):
                scr[s, pl.ds(r, n, stride=d), :] = ref[r, :, s * 128:(s + 1) * 128]
        return lambda rows: jnp.concatenate([scr[s, rows, :] for s in range(slabs)], axis=-1)

    loaders = [natural(ol_refs[k], dils[k % N_GROUPS]) for k in range(2 * N_GROUPS)]
    compact = ol_refs[N_GROUPS].shape[-1] != C_WIDTH
    if compact:
        lane = lax.broadcasted_iota(jnp.int32, (2 * LSE_LANES, C_WIDTH), 0) & (LSE_LANES - 1)
        chan = lax.broadcasted_iota(jnp.int32, (2 * LSE_LANES, C_WIDTH), 1)
        spread = jnp.where((chan >> (HEAD_DIM.bit_length() - 1)) == lane, 1.0, 0.0).astype(BF16)

        def expand(a):
            hi = a.astype(BF16)
            lo = (a - hi.astype(F32)).astype(BF16)
            return jnp.dot(jnp.concatenate([hi, lo], axis=-1), spread, preferred_element_type=F32)

    n_parts = MERGE_PARTS if tm % (8 * MERGE_PARTS) == 0 else 1
    parts = [slice(p * (tm // n_parts), (p + 1) * (tm // n_parts)) for p in range(n_parts)]

    def weights(rows):
        l0, l1, l2 = [load(rows) for load in loaders[N_GROUPS:]]
        m = jnp.maximum(jnp.maximum(l0, l1), l2)
        e0, e1, e2 = jnp.exp(l0 - m), jnp.exp(l1 - m), jnp.exp(l2 - m)
        inv = 1.0 / (e0 + e1 + e2)
        alphas = [e0 * inv, e1 * inv, e2 * inv]
        return [expand(a) for a in alphas] if compact else alphas

    def project(rows, alphas):
        o0, o1, o2 = [load(rows) for load in loaders[:N_GROUPS]]
        o = alphas[0] * o0 + alphas[1] * o1 + alphas[2] * o2
        return jnp.dot(o.astype(BF16), w_ref[...], preferred_element_type=F32)

    alphas = [weights(rows) for rows in parts]
    mixes = [project(rows, a) for rows, a in zip(parts, alphas)]
    for rows, mix in zip(parts, mixes):
        out_ref[rows, :] = _layer_norm(ALPHA * x_ref[rows, :] + mix, g_ref[...], b_ref[...])


def _merge_layer(x, outs, lses, w_out, g, b, dils):
    rows = x.shape[0]
    tm = min(MERGE_TILE, rows)
    per_tile = PERM_TILE // tm
    row_spec = lambda cols: pl.BlockSpec((tm, cols), lambda t: (t, 0))

    def group_arg(a, d):
        cols = a.shape[-1]
        if d == 1:
            return a, row_spec(cols), None
        run = PERM_TILE // d
        view = a.reshape(rows // PERM_TILE, d, run, cols)
        spec = pl.BlockSpec((None, d, tm // d, cols), lambda t: (t // per_tile, 0, t % per_tile, 0))
        return view, spec, pltpu.VMEM((cols // 128, tm, 128), F32)

    args, specs, scratch = zip(*[group_arg(a, dils[k % N_GROUPS]) for k, a in enumerate(list(outs) + list(lses))])
    scratch = [s for s in scratch if s is not None][::-1]
    weights, w_specs = _resident_all((w_out, g, b))
    return pl.pallas_call(
        functools.partial(_merge_kernel, tm=tm, dils=dils),
        grid=(rows // tm,),
        in_specs=[row_spec(D_MODEL)] + list(specs) + w_specs,
        out_specs=row_spec(D_MODEL),
        out_shape=jax.ShapeDtypeStruct((rows, D_MODEL), F32),
        scratch_shapes=scratch,
        compiler_params=_params(1),
        name="merge_layer",
    )(x, *args, *weights)


def kernel(x_prompt, x_sample, state_conv, cache_kv_w128, cache_kv_w512, cache_kv_w2048, w_in_ab, ln_v_g, ln_v_b, w_spatial, b_spatial, conv_w, w_out_ab, w_qkv_c, w_out_c, ln1_g, ln1_b, ln2_g, ln2_b, w_mlp_up, w_mlp_down):
    batch, seq, _ = x_prompt.shape
    n_seq, t_new, _ = x_sample.shape
    n_tok_s = n_seq * t_new
    xp = x_prompt.reshape(batch * seq, D_MODEL)
    xs = x_sample.reshape(n_tok_s, D_MODEL)
    caches = [jnp.transpose(c, (0, 1, 3, 4, 5, 2)) for c in (cache_kv_w128, cache_kv_w512, cache_kv_w2048)]
    rows3 = lambda a: a.reshape(a.shape[0], 1, a.shape[1])
    ln_v_g3, ln_v_b3, ln1_g3, ln1_b3, ln2_g3, ln2_b3 = map(rows3, (ln_v_g, ln_v_b, ln1_g, ln1_b, ln2_g, ln2_b))

    causal = jnp.tril(jnp.ones((CHUNK, CHUNK), dtype=bool))
    w_in_b, w_out_ab_b, w_out_c_b = w_in_ab.astype(BF16), w_out_ab.astype(BF16), w_out_c.astype(BF16)
    cast_later = (w_mlp_up.reshape(-1, D_FF), w_mlp_down.reshape(-1, D_MODEL))
    w_qkv_b = w_qkv_c.astype(BF16)
    n_c = w_qkv_b.shape[0]
    wq5 = w_qkv_b.reshape(n_c, D_MODEL, 3, N_GROUPS, C_WIDTH)
    wk_groups = jnp.transpose(wq5[:, :, 1], (0, 2, 1, 3))
    wqv_t_groups = jnp.transpose(wq5[:, :, 0::2], (0, 3, 2, 4, 1)).reshape(n_c, N_GROUPS, 2 * C_WIDTH, D_MODEL)
    wkv_t_groups = jnp.transpose(wq5[:, :, 1:], (0, 3, 2, 4, 1)).reshape(n_c, N_GROUPS, 2 * C_WIDTH, D_MODEL)
    conv_p, conv_s, chunk_v_s = [], [], []
    c_inputs = []
    qkv_s_layers, rolled = [], None
    new_caches = [None] * N_GROUPS
    hosted = [g for g in range(N_GROUPS) if caches[g].shape[-1] > SAMPLE_STEP_POSITIONS]
    mlp_steps = batch * seq // ROW_TILE
    assert DEPTH * mlp_steps == caches[0].shape[0] * n_seq * 2

    for layer in range(DEPTH):
        i = layer // 2
        if layer % 2 == 0:
            w_tril = jnp.where(causal[None], w_spatial[i], 0.0)
            bmix = jnp.repeat(b_spatial[i].T, CHUNK, axis=1)
            shared = (_Layer(w_in_b, i), _Layer(ln_v_g3, i), _Layer(ln_v_b3, i))
            tail = (_Layer(conv_w, i), _Layer(w_out_ab_b, i), _Layer(ln1_g3, layer), _Layer(ln1_b3, layer))
            xp, buf_p, cast_done = _ab_layer_prompt(xp, batch, seq, shared + (w_tril.astype(BF16), bmix) + tail,
                                                    cast_later if layer == 0 else ())
            if layer == 0:
                w_up_b = cast_done[0].reshape(w_mlp_up.shape)
                w_down_b = cast_done[1].reshape(w_mlp_down.shape)
            eye = jnp.eye(n_seq, dtype=F32)
            w_blk = jnp.stack([jnp.kron(eye, w_tril[g, :t_new, :t_new]) for g in range(G_A)])
            bmix_s = jnp.tile(bmix[:t_new], (n_seq, 1))
            st = state_conv[i]
            s1 = jnp.concatenate([st[:, 1:2], jnp.zeros((n_seq, t_new - 1, W_B), F32)], axis=1)
            s2 = jnp.concatenate([st, jnp.zeros((n_seq, t_new - 2, W_B), F32)], axis=1)
            xs, hc_s, v_s = _ab_layer_sample(xs, s1.reshape(n_tok_s, W_B), s2.reshape(n_tok_s, W_B),
                                             shared + (w_blk.astype(BF16), bmix_s) + tail)
            conv_p.append(buf_p)
            conv_s.append(hc_s.reshape(n_seq, t_new, W_B)[:, t_new - 2:])
            chunk_v_s.append(v_s.reshape(n_seq, t_new, W_A))
        else:
            wo = _Layer(w_out_c_b, i)
            g1, b1 = _Layer(ln1_g3, layer), _Layer(ln1_b3, layer)
            c_inputs.append(xp)
            k_all, qt_all, vt_all = _qkv_proj_perm(xp, wk_groups, wqv_t_groups, i, batch, seq)
            outs, lses = zip(*[_attn_prompt(k_all, qt_all, vt_all, batch, seq, g) for g in range(N_GROUPS)])
            xp = _merge_layer(xp, outs, lses, wo, g1, b1, DILATIONS)
            qkv_s = _qkv_proj(xs, _Layer(w_qkv_b, i))
            qkv_s_layers.append(qkv_s)
            outs, lses = [], []
            for g in range(N_GROUPS):
                res = _attn_sample(qkv_s, caches[g], i, g, t_new, g not in hosted, new_caches[g])
                outs.append(res[0])
                lses.append(res[1])
                if g not in hosted:
                    new_caches[g] = res[2]
            xs = _merge_layer(xs, outs, lses, wo, g1, b1, (1,) * N_GROUPS)
        w_up, w_down = _Layer(w_up_b, layer), _Layer(w_down_b, layer)
        g2, b2 = _Layer(ln2_g3, layer), _Layer(ln2_b3, layer)
        xp, xs, rolled = _mlp_layer(xp, xs, w_up, w_down, g2, b2,
                                    roll=([caches[g] for g in hosted], rolled, layer * mlp_steps, t_new))

    for k, g in enumerate(hosted):
        new_caches[g] = _patch_rolled(rolled[k], qkv_s_layers, g, t_new)
    kv_s = [jnp.transpose(nc, (0, 1, 5, 2, 3, 4)) for nc in new_caches]
    tails = _kv_tail(c_inputs, wkv_t_groups, batch, seq)
    kv_p = [jnp.transpose(t.reshape(n_c, batch, 2, H_G, HEAD_DIM, WINDOWS[g]), (0, 1, 5, 2, 3, 4))
            for g, t in enumerate(tails)]
    return (xp.reshape(batch, seq, D_MODEL), xs.reshape(n_seq, t_new, D_MODEL),
            jnp.stack(conv_p), jnp.stack(conv_s), jnp.stack(chunk_v_s),
            kv_p[0], kv_p[1], kv_p[2], kv_s[0], kv_s[1], kv_s[2])
```
